```python
import jax
import jax.numpy as jnp
from jax import lax
import numpy as np


D_MODEL = 1024
BATCH = 32
SEQ = 2048
DEPTH = 2

CTX_LEN = 256
GRID_W = 64
N_MIXERS = 2
ADA_CHUNKS = 6
NORM_EPS = 1e-6

HG_HEADS = 8
HG_DK = D_MODEL // HG_HEADS
HG_DV = D_MODEL // HG_HEADS
HG_CHUNK = 32

MLA_HEADS = 16
MLA_Q_LORA = 256
MLA_KV_LORA = 128
MLA_NOPE = 64
MLA_ROPE = 32
MLA_V = 64
MLA_QK = MLA_NOPE + MLA_ROPE
ROPE_THETA = 10000.0
Q_BLOCK = 128

MOE_GROUPS = 4
MOE_EXPERTS_PER_GROUP = 8
MOE_EXPERTS = MOE_GROUPS * MOE_EXPERTS_PER_GROUP
MOE_TOP_K = 2
MOE_FF = 512
MOE_BLOCK = 256

kernel_name = 'hybrid_hgrn2_mla_hmoe_dit'


def rms_norm(x, g):
    xf = x.astype(jnp.float32)
    y = xf * lax.rsqrt(jnp.mean(xf * xf, axis=-1, keepdims=True) + NORM_EPS)
    return (y * g.astype(jnp.float32)).astype(x.dtype)


def modulate(x, g, shift, scale):
    return rms_norm(x, g) * (1 + scale) + shift


def gla_chunkwise(q, k, v, log_f, s0):
    b_, h_, L, _ = q.shape
    n_chunks = L // HG_CHUNK

    def chunks(a):
        return jnp.moveaxis(a.reshape(b_, h_, n_chunks, HG_CHUNK, a.shape[-1]), 2, 0)

    lower_tri = jnp.tril(jnp.ones((HG_CHUNK, HG_CHUNK), dtype=bool))

    def step(s, blk):
        qc, kc, vc, gc = blk
        b = jnp.cumsum(gc, axis=-2)
        b_end = b[:, :, -1:, :]
        q_dec = qc * jnp.exp(b)
        scores = jnp.einsum('bhtc,bhsc->bhts', q_dec, kc * jnp.exp(-b))
        scores = jnp.where(lower_tri, scores, 0.0)
        o = jnp.einsum('bhts,bhsv->bhtv', scores, vc) + jnp.einsum('bhtc,bhcv->bhtv', q_dec, s)
        s_new = jnp.exp(b_end[:, :, 0, :])[..., None] * s + jnp.einsum('bhsc,bhsv->bhcv', kc * jnp.exp(b_end - b), vc)
        return s_new, o

    s_fin, o = lax.scan(step, s0, (chunks(q), chunks(k), chunks(v), chunks(log_f)))
    return jnp.moveaxis(o, 0, 2).reshape(b_, h_, L, v.shape[-1]), s_fin


def hgrn2_mixer(h_lat, h_ctx, w_in, lower_bound, out_norm_g, w_out, with_ctx_out):
    lb = lower_bound.astype(jnp.float32).reshape(2, HG_HEADS, 1, HG_DK)

    def prepare(h):
        b_, L, _ = h.shape
        p = jnp.einsum('bld,de->ble', h, w_in)
        q, z_fw, z_bw, v, g = jnp.split(p, 5, axis=-1)

        def heads(a):
            return a.reshape(b_, L, HG_HEADS, -1).transpose(0, 2, 1, 3).astype(jnp.float32)

        f_fw = lb[0] + (1 - lb[0]) * jax.nn.sigmoid(heads(z_fw))
        f_bw = lb[1] + (1 - lb[1]) * jax.nn.sigmoid(heads(z_bw))
        return jax.nn.silu(heads(q)), heads(v), (1 - f_fw, jnp.log(f_fw)), (1 - f_bw, jnp.log(f_bw)), g

    def flip(a):
        return jnp.flip(a, axis=2)

    def readout(o, g):
        b_, _, L, _ = o.shape
        o = rms_norm(o, out_norm_g).transpose(0, 2, 1, 3).reshape(b_, L, D_MODEL)
        return jnp.einsum('ble,ed->bld', o.astype(g.dtype) * jax.nn.silu(g), w_out)

    qc, vc, (kc_f, gc_f), (kc_b, gc_b), g_c = prepare(h_ctx)
    s0 = jnp.zeros(qc.shape[:2] + (HG_DK, HG_DV), jnp.float32)
    o_c_f, s_f = gla_chunkwise(qc, kc_f, vc, gc_f, s0)
    o_c_b, s_b = gla_chunkwise(flip(qc), flip(kc_b), flip(vc), flip(gc_b), s0)
    ql, vl, (kl_f, gl_f), (kl_b, gl_b), g_l = prepare(h_lat)
    o_l_f, _ = gla_chunkwise(ql, kl_f, vl, gl_f, s_f)
    o_l_b, _ = gla_chunkwise(flip(ql), flip(kl_b), flip(vl), flip(gl_b), s_b)
    y_lat = readout(o_l_f + flip(o_l_b), g_l)
    y_ctx = readout(o_c_f + flip(o_c_b), g_c) if with_ctx_out else None
    return y_lat, y_ctx


def axial_rope_tables(L):
    rows = L // GRID_W
    row = jnp.repeat(jnp.arange(rows), GRID_W)
    col = jnp.tile(jnp.arange(GRID_W), rows)
    half = MLA_ROPE // 2
    inv_freq = ROPE_THETA ** (-jnp.arange(0, half, 2, dtype=jnp.float32) / half)
    ang = jnp.stack([row, col], axis=-1).astype(jnp.float32)[..., None] * inv_freq
    ang = jnp.concatenate([ang, ang], axis=-1)
    return jnp.cos(ang), jnp.sin(ang)


def apply_axial_rope(x, cos, sin):
    b_, L, h_, _ = x.shape
    x_nope = x[..., :MLA_NOPE]
    x_rope = x[..., MLA_NOPE:].reshape(b_, L, h_, 2, MLA_ROPE // 2)
    x1, x2 = jnp.split(x_rope, 2, axis=-1)
    rot = jnp.concatenate([-x2, x1], axis=-1)
    cos = cos[:, None].astype(x.dtype)
    sin = sin[:, None].astype(x.dtype)
    x_rope = (x_rope * cos + rot * sin).reshape(b_, L, h_, MLA_ROPE)
    return jnp.concatenate([x_nope, x_rope], axis=-1)


def mla_queries(c_q, q_norm_g, w_qb, q_qk_g):
    b_, L, _ = c_q.shape
    q = jnp.einsum('blr,re->ble', rms_norm(c_q, q_norm_g), w_qb).reshape(b_, L, MLA_HEADS, MLA_QK)
    return rms_norm(q, q_qk_g)


def mla_keys_values(c_kv, k_rope, kv_norm_g, w_kvb, k_qk_g):
    b_, L, _ = c_kv.shape
    kv = jnp.einsum('blr,re->ble', rms_norm(c_kv, kv_norm_g), w_kvb).reshape(b_, L, MLA_HEADS, MLA_NOPE + MLA_V)
    k_nope, v = jnp.split(kv, [MLA_NOPE], axis=-1)
    k = jnp.concatenate([k_nope, jnp.broadcast_to(k_rope[:, :, None, :], (b_, L, MLA_HEADS, MLA_ROPE))], axis=-1)
    return rms_norm(k, k_qk_g), v


def softmax_attend(q, k, v):
    s = jnp.einsum('bqhd,bkhd->bhqk', q, k).astype(jnp.float32) * (MLA_QK ** -0.5)
    p = jax.nn.softmax(s, axis=-1).astype(v.dtype)
    return jnp.einsum('bhqk,bkhd->bqhd', p, v)


def mla_mixer(h_lat, h_ctx, w_in, q_norm_g, kv_norm_g, w_qb, w_kvb, q_qk_g, k_qk_g, w_out, with_ctx_out):
    b_, L, _ = h_lat.shape
    ctx_len = h_ctx.shape[1]
    cos, sin = axial_rope_tables(L)
    c_q, c_kv, k_rope = jnp.split(jnp.einsum('bld,de->ble', h_lat, w_in), [MLA_Q_LORA, MLA_Q_LORA + MLA_KV_LORA], axis=-1)
    q_lat = apply_axial_rope(mla_queries(c_q, q_norm_g, w_qb, q_qk_g), cos, sin)
    k_lat, v_lat = mla_keys_values(c_kv, k_rope, kv_norm_g, w_kvb, k_qk_g)
    k_lat = apply_axial_rope(k_lat, cos, sin)
    c_kv_c, k_rope_c = jnp.split(jnp.einsum('bld,de->ble', h_ctx, w_in[:, MLA_Q_LORA:]), [MLA_KV_LORA], axis=-1)
    k_ctx, v_ctx = mla_keys_values(c_kv_c, k_rope_c, kv_norm_g, w_kvb, k_qk_g)
    k_all = jnp.concatenate([k_ctx, k_lat], axis=1)
    v_all = jnp.concatenate([v_ctx, v_lat], axis=1)
    n_blk = L // Q_BLOCK
    q_blocks = q_lat.reshape(b_, n_blk, Q_BLOCK, MLA_HEADS, MLA_QK).transpose(1, 0, 2, 3, 4)
    o_lat = lax.map(lambda qb: softmax_attend(qb, k_all, v_all), q_blocks)
    o_lat = o_lat.transpose(1, 0, 2, 3, 4).reshape(b_, L, MLA_HEADS * MLA_V)
    y_lat = jnp.einsum('ble,ed->bld', o_lat, w_out)
    y_ctx = None
    if with_ctx_out:
        q_ctx = mla_queries(jnp.einsum('bld,de->ble', h_ctx, w_in[:, :MLA_Q_LORA]), q_norm_g, w_qb, q_qk_g)
        o_ctx = softmax_attend(q_ctx, k_ctx, v_ctx).reshape(b_, ctx_len, MLA_HEADS * MLA_V)
        y_ctx = jnp.einsum('ble,ed->bld', o_ctx, w_out)
    return y_lat, y_ctx


def hier_moe(h, w_group, w_expert, w_gate, w_up, w_down):
    n_tok, d = h.shape
    hf = h.astype(jnp.float32)
    g_prob = jax.nn.softmax(hf @ w_group.astype(jnp.float32), axis=-1)
    g_sel = jnp.argmax(g_prob, axis=-1)
    p_group = jnp.take_along_axis(g_prob, g_sel[:, None], axis=-1)
    e_logits = (hf @ w_expert.astype(jnp.float32)).reshape(n_tok, MOE_GROUPS, MOE_EXPERTS_PER_GROUP)
    e_logits = jnp.take_along_axis(e_logits, g_sel[:, None, None], axis=1)[:, 0]
    top_p, top_i = lax.top_k(jax.nn.softmax(e_logits, axis=-1), MOE_TOP_K)
    gate_w = p_group * top_p / jnp.sum(top_p, axis=-1, keepdims=True)
    expert_id = (g_sel[:, None] * MOE_EXPERTS_PER_GROUP + top_i).astype(jnp.int32)

    n_slot = n_tok * MOE_TOP_K
    e_flat = expert_id.reshape(n_slot)
    order = jnp.argsort(e_flat)
    e_sorted = e_flat[order]
    tok_sorted = order // MOE_TOP_K
    counts = jnp.zeros((MOE_EXPERTS,), jnp.int32).at[e_flat].add(1)
    starts = jnp.cumsum(counts) - counts
    padded = (counts + MOE_BLOCK - 1) // MOE_BLOCK * MOE_BLOCK
    pad_ends = jnp.cumsum(padded)
    pad_starts = pad_ends - padded
    dest = pad_starts[e_sorted] + jnp.arange(n_slot, dtype=jnp.int32) - starts[e_sorted]
    n_blocks = (n_slot + MOE_BLOCK - 1) // MOE_BLOCK + MOE_EXPERTS
    buf = jnp.zeros((n_blocks * MOE_BLOCK, d), h.dtype).at[dest].set(h[tok_sorted])
    block_start = jnp.arange(n_blocks, dtype=jnp.int32) * MOE_BLOCK
    block_expert = jnp.minimum(jnp.searchsorted(pad_ends, block_start, side='right'), MOE_EXPERTS - 1)

    def expert_block(args):
        xb, e = args
        return (jax.nn.silu(xb @ w_gate[e]) * (xb @ w_up[e])) @ w_down[e]

    y_buf = lax.map(expert_block, (buf.reshape(n_blocks, MOE_BLOCK, d), block_expert)).reshape(n_blocks * MOE_BLOCK, d)
    y_slot = jnp.zeros((n_slot, d), h.dtype).at[order].set(y_buf[dest])
    return jnp.einsum('nkd,nk->nd', y_slot.reshape(n_tok, MOE_TOP_K, d), gate_w.astype(h.dtype))


def setup_inputs(seed: int = 0) -> dict:
    key = jax.random.key(seed)
    ks = jax.random.split(key, 25)
    n_hg = (DEPTH + 1) // 2
    n_mla = DEPTH // 2
    D = D_MODEL

    def nrm(k, shape, scale):
        return scale * jax.random.normal(k, shape, jnp.float32)

    return {
        'x': nrm(ks[0], (BATCH, SEQ, D), 1.0),
        'c': nrm(ks[1], (BATCH, D), 1.0),
        'ctx': nrm(ks[2], (BATCH, CTX_LEN, D), 1.0),
        'c_ctx': nrm(ks[3], (D,), 1.0),
        'ada_w': nrm(ks[4], (DEPTH, D, ADA_CHUNKS * D), 0.5 * D ** -0.5),
        'ada_b': nrm(ks[5], (DEPTH, ADA_CHUNKS * D), 0.02),
        'norm_mix_g': 1.0 + nrm(ks[6], (DEPTH, D), 0.02),
        'norm_ffn_g': 1.0 + nrm(ks[7], (DEPTH, D), 0.02),
        'hg_w_in': nrm(ks[8], (n_hg, D, 5 * D), D ** -0.5),
        'hg_lower_bounds': nrm(ks[9], (2, DEPTH + 1, D), 0.1),
        'hg_out_norm_g': 1.0 + nrm(ks[10], (n_hg, HG_DV), 0.02),
        'hg_w_out': nrm(ks[11], (n_hg, D, D), D ** -0.5),
        'mla_w_in': nrm(ks[12], (n_mla, D, MLA_Q_LORA + MLA_KV_LORA + MLA_ROPE), D ** -0.5),
        'mla_q_norm_g': 1.0 + nrm(ks[13], (n_mla, MLA_Q_LORA), 0.02),
        'mla_kv_norm_g': 1.0 + nrm(ks[14], (n_mla, MLA_KV_LORA), 0.02),
        'mla_w_qb': nrm(ks[15], (n_mla, MLA_Q_LORA, MLA_HEADS * MLA_QK), MLA_Q_LORA ** -0.5),
        'mla_w_kvb': nrm(ks[16], (n_mla, MLA_KV_LORA, MLA_HEADS * (MLA_NOPE + MLA_V)), MLA_KV_LORA ** -0.5),
        'mla_q_qknorm_g': 1.0 + nrm(ks[17], (n_mla, MLA_QK), 0.02),
        'mla_k_qknorm_g': 1.0 + nrm(ks[18], (n_mla, MLA_QK), 0.02),
        'mla_w_out': nrm(ks[19], (n_mla, MLA_HEADS * MLA_V, D), (MLA_HEADS * MLA_V) ** -0.5),
        'moe_w_group': nrm(ks[20], (DEPTH, D, MOE_GROUPS), D ** -0.5),
        'moe_w_expert': nrm(ks[21], (DEPTH, D, MOE_EXPERTS), D ** -0.5),
        'moe_w_gate': nrm(ks[22], (DEPTH, MOE_EXPERTS, D, MOE_FF), D ** -0.5),
        'moe_w_up': nrm(ks[23], (DEPTH, MOE_EXPERTS, D, MOE_FF), D ** -0.5),
        'moe_w_down': nrm(ks[24], (DEPTH, MOE_EXPERTS, MOE_FF, D), MOE_FF ** -0.5),
    }


def reference(x, c, ctx, c_ctx, ada_w, ada_b, norm_mix_g, norm_ffn_g, hg_w_in, hg_lower_bounds, hg_out_norm_g, hg_w_out, mla_w_in, mla_q_norm_g, mla_kv_norm_g, mla_w_qb, mla_w_kvb, mla_q_qknorm_g, mla_k_qknorm_g, mla_w_out, moe_w_group, moe_w_expert, moe_w_gate, moe_w_up, moe_w_down):
    b_, seq, d = x.shape
    ctx_len = ctx.shape[1]
    lower_bounds = jnp.cumsum(jax.nn.softmax(hg_lower_bounds.astype(jnp.float32), axis=1), axis=1)
    x_lat, x_ctx = x, ctx
    for i in range(DEPTH):
        last = i == DEPTH - 1
        j = i // N_MIXERS
        mod_lat = (jnp.einsum('bd,de->be', jax.nn.silu(c), ada_w[i]) + ada_b[i])[:, None, :]
        mod_ctx = jax.nn.silu(c_ctx) @ ada_w[i] + ada_b[i]
        sh_m, sc_m, gt_m, sh_f, sc_f, gt_f = jnp.split(mod_lat, ADA_CHUNKS, axis=-1)
        csh_m, csc_m, cgt_m, csh_f, csc_f, cgt_f = jnp.split(mod_ctx, ADA_CHUNKS, axis=-1)
        h_lat = modulate(x_lat, norm_mix_g[i], sh_m, sc_m)
        h_ctx = modulate(x_ctx, norm_mix_g[i], csh_m, csc_m)
        if i % N_MIXERS == 0:
            y_lat, y_ctx = hgrn2_mixer(h_lat, h_ctx, hg_w_in[j], lower_bounds[:, i], hg_out_norm_g[j], hg_w_out[j], not last)
        else:
            y_lat, y_ctx = mla_mixer(h_lat, h_ctx, mla_w_in[j], mla_q_norm_g[j], mla_kv_norm_g[j], mla_w_qb[j], mla_w_kvb[j], mla_q_qknorm_g[j], mla_k_qknorm_g[j], mla_w_out[j], not last)
        x_lat = x_lat + gt_m * y_lat
        h2_lat = modulate(x_lat, norm_ffn_g[i], sh_f, sc_f).reshape(b_ * seq, d)
        if last:
            f_lat = hier_moe(h2_lat, moe_w_group[i], moe_w_expert[i], moe_w_gate[i], moe_w_up[i], moe_w_down[i])
        else:
            x_ctx = x_ctx + cgt_m * y_ctx
            h2_ctx = modulate(x_ctx, norm_ffn_g[i], csh_f, csc_f).reshape(b_ * ctx_len, d)
            f_all = hier_moe(jnp.concatenate([h2_ctx, h2_lat], axis=0), moe_w_group[i], moe_w_expert[i], moe_w_gate[i], moe_w_up[i], moe_w_down[i])
            x_ctx = x_ctx + cgt_f * f_all[: b_ * ctx_len].reshape(b_, ctx_len, d)
            f_lat = f_all[b_ * ctx_len:]
        x_lat = x_lat + gt_f * f_lat.reshape(b_, seq, d)
    return x_lat
```

```python
import functools

import jax
import jax.numpy as jnp
from jax import lax
from jax.experimental import pallas as pl
from jax.experimental.pallas import tpu as pltpu

F32 = jnp.float32
BF16 = jnp.bfloat16

DEPTH = 2
ADA_CHUNKS = 6
NORM_EPS = 1e-6
GRID_W = 64
HG_HEADS = 8
HG_CHUNK = 32
MLA_HEADS = 16
MLA_Q_LORA = 256
MLA_KV_LORA = 128
MLA_NOPE = 64
MLA_ROPE = 32
MLA_V = 64
MLA_QK = MLA_NOPE + MLA_ROPE
ROPE_THETA = 10000.0
MOE_GROUPS = 4
MOE_EPG = 8
MOE_EXPERTS = MOE_GROUPS * MOE_EPG
MOE_TOP_K = 2
MOE_FF = 512
MOE_BLOCK = 256

LANES = 128
ROUTER_LANES = 128
VMEM_LIMIT = 56 * 1024 * 1024

ROW_TILE = 256
GLA_BLOCK = 128
ATT_Q_TILE = 256
ADA_ROWS = 40


def _cparams(sem):
    return pltpu.CompilerParams(dimension_semantics=sem, vmem_limit_bytes=VMEM_LIMIT)


def _resident(shape):
    nd = len(shape)
    return pl.BlockSpec(shape, lambda *_: (0,) * nd, pipeline_mode=pl.Buffered(1))


def _dot(a, b):
    return jnp.dot(a, b, preferred_element_type=F32)


def _dot_nt(a, b):
    return lax.dot_general(a, b, (((1,), (1,)), ((), ())), preferred_element_type=F32)


def _dot_tn(a, b):
    return lax.dot_general(a, b, (((0,), (0,)), ((), ())), preferred_element_type=F32)


def _sigmoid(x):
    return 1.0 / (1.0 + jnp.exp(-x))


def _rms(x):
    return x * lax.rsqrt(jnp.mean(x * x, axis=-1, keepdims=True) + NORM_EPS)


def _ada_kernel(c_ref, w_ref, b_ref, o_ref):
    c = c_ref[...]
    a = (c * _sigmoid(c)).astype(BF16)
    o_ref[...] = _dot(a, w_ref[...].astype(BF16)) + b_ref[...]


def _ada_call(cc, ada_w, ada_b):
    depth, d, _ = ada_w.shape
    rows = cc.shape[0]
    return pl.pallas_call(
        _ada_kernel,
        grid=(depth, ADA_CHUNKS),
        in_specs=[
            pl.BlockSpec((rows, d), lambda i, j: (0, 0)),
            pl.BlockSpec((None, d, d), lambda i, j: (i, 0, j)),
            pl.BlockSpec((None, 1, d), lambda i, j: (i, 0, j)),
        ],
        out_specs=pl.BlockSpec((None, None, rows, d), lambda i, j: (i, j, 0, 0)),
        out_shape=jax.ShapeDtypeStruct((depth, ADA_CHUNKS, rows, d), F32),
        compiler_params=_cparams(("arbitrary", "arbitrary")),
        name="ada_mod",
    )(cc, ada_w, ada_b.reshape(depth, 1, ADA_CHUNKS * d))


def _mod_spec(d, n_ctx_blocks, ctx_row):
    def idx(b, j):
        return (jnp.where(j < n_ctx_blocks, ctx_row, b), 0, 0)
    return pl.BlockSpec((None, 1, d), idx)


def _hg_in_kernel(x_ref, sh_ref, sc_ref, g_ref, w_ref, lb_ref,
                  q_ref, kf_ref, gf_ref, kb_ref, gb_ref, v_ref, gs_ref):
    d = x_ref.shape[-1]
    h = _rms(x_ref[...]) * g_ref[...]
    h = (h * (1.0 + sc_ref[...]) + sh_ref[...]).astype(BF16)

    def proj(c):
        return _dot(h, w_ref[:, c * d:(c + 1) * d])

    p = proj(0)
    q_ref[...] = (p * _sigmoid(p)).astype(BF16)
    for c, k_ref, lg_ref in ((1, kf_ref, gf_ref), (2, kb_ref, gb_ref)):
        s = _sigmoid(proj(c))
        lb = lb_ref[c - 1:c, :]
        k_ref[...] = ((1.0 - lb) * (1.0 - s)).astype(BF16)
        lg_ref[...] = jnp.log(lb + (1.0 - lb) * s).astype(BF16)
    v_ref[...] = proj(3).astype(BF16)
    p = proj(4)
    gs_ref[...] = (p * _sigmoid(p)).astype(BF16)


def _hg_in_call(xcat, sh, sc, g, w_in, lb, n_ctx, tm):
    b_, t_, d = xcat.shape
    ncb = n_ctx // tm
    tok = pl.BlockSpec((None, tm, d), lambda b, j: (b, j, 0))
    out = jax.ShapeDtypeStruct((b_, t_, d), BF16)
    return pl.pallas_call(
        _hg_in_kernel,
        grid=(b_, t_ // tm),
        in_specs=[tok, _mod_spec(d, ncb, b_), _mod_spec(d, ncb, b_),
                  _resident((1, d)), _resident(w_in.shape), _resident((2, d))],
        out_specs=[tok] * 7,
        out_shape=[out] * 7,
        compiler_params=_cparams(("parallel", "arbitrary")),
        name="hg_in",
    )(xcat, sh, sc, g, w_in, lb)


def _gla_kernel(*refs, reverse, final, n_heads):
    if final:
        q_ref, k_ref, g_ref, v_ref, ob_ref, gs_ref, gn_ref, o_ref, st_ref = refs
    else:
        q_ref, k_ref, g_ref, v_ref, o_ref, st_ref = refs
    tb, d = q_ref.shape
    dh = d // n_heads
    n_chunks = tb // HG_CHUNK

    @pl.when(pl.program_id(1) == 0)
    def _():
        st_ref[...] = jnp.zeros_like(st_ref)

    row = lax.broadcasted_iota(jnp.int32, (tb, tb), 0)
    col = lax.broadcasted_iota(jnp.int32, (tb, tb), 1)
    same_chunk = (row // HG_CHUNK) == (col // HG_CHUNK)
    tri = same_chunk & ((col >= row) if reverse else (col <= row))
    cum = jnp.where(tri, 1.0, 0.0).astype(BF16)

    b = _dot(cum, g_ref[...])
    qd = (q_ref[...].astype(F32) * jnp.exp(b)).astype(BF16)
    ki = k_ref[...].astype(F32) * jnp.exp(-b)
    ki_b = ki.astype(BF16)

    chunk_order = range(n_chunks - 1, -1, -1) if reverse else range(n_chunks)
    e_end, k_end = {}, {}
    for c in range(n_chunks):
        r0 = c * HG_CHUNK
        last = r0 if reverse else r0 + HG_CHUNK - 1
        e_end[c] = jnp.exp(b[last:last + 1, :])
        k_end[c] = (ki[r0:r0 + HG_CHUNK, :] * e_end[c]).astype(BF16)

    for h in range(n_heads):
        hs = slice(h * dh, (h + 1) * dh)
        qh = qd[:, hs]
        vh = v_ref[:, hs]
        scores = jnp.where(tri, _dot_nt(qh, ki_b[:, hs]), 0.0).astype(BF16)
        o_intra = _dot(scores, vh)
        st = st_ref[h]
        for c in chunk_order:
            rs = slice(c * HG_CHUNK, (c + 1) * HG_CHUNK)
            o_c = o_intra[rs, :] + _dot_nt(qh[rs, :], st.astype(BF16))
            st = st * e_end[c][:, hs] + _dot_tn(vh[rs, :], k_end[c][:, hs])
            if final:
                o_c = o_c + ob_ref[rs, hs].astype(F32)
                o_c = _rms(o_c) * gn_ref[...]
                o_c = o_c * gs_ref[rs, hs].astype(F32)
            o_ref[rs, hs] = o_c.astype(o_ref.dtype)
        st_ref[h] = st


def _gla_call(q, k, g, v, n_ctx, reverse, final_args=None):
    b_, t_, d = q.shape
    tb = GLA_BLOCK
    nb = t_ // tb
    ncb = n_ctx // tb
    dh = d // HG_HEADS

    def blk(b, j):
        if reverse:
            jj = jnp.where(j < ncb, ncb - 1 - j, nb - 1 - (j - ncb))
        else:
            jj = j
        return (b, jj, 0)

    tok = pl.BlockSpec((None, tb, d), blk)
    in_specs = [tok] * 4
    args = [q, k, g, v]
    final = final_args is not None
    if final:
        o_b, gs, gn = final_args
        in_specs += [tok, tok, _resident((1, dh))]
        args += [o_b, gs, gn]
    return pl.pallas_call(
        functools.partial(_gla_kernel, reverse=reverse, final=final, n_heads=HG_HEADS),
        grid=(b_, nb),
        in_specs=in_specs,
        out_specs=tok,
        out_shape=jax.ShapeDtypeStruct((b_, t_, d), BF16),
        scratch_shapes=[pltpu.VMEM((HG_HEADS, dh, dh), F32)],
        compiler_params=_cparams(("parallel", "arbitrary")),
        name="gla_bwd" if reverse else "gla_fwd",
    )(*args)


def _outproj_kernel(a_ref, w_ref, x_ref, gt_ref, sh_ref, sc_ref, g_ref, wr_ref,
                    xn_ref, h2_ref, lg_ref):
    y = _dot(a_ref[...], w_ref[...])
    xn = x_ref[...] + gt_ref[...] * y
    xn_ref[...] = xn
    h2 = _rms(xn) * g_ref[...]
    h2 = h2 * (1.0 + sc_ref[...]) + sh_ref[...]
    h2_ref[...] = h2.astype(BF16)
    lg_ref[...] = jnp.dot(h2, wr_ref[...], preferred_element_type=F32,
                          precision=lax.Precision.HIGHEST)


def _outproj_call(a, w, xres, x_off, gt, sh, sc, g, wr, n_ctx, tm):
    b_, tn, d = a.shape
    ncb = n_ctx // tm
    xob = x_off // tm
    tok = pl.BlockSpec((None, tm, d), lambda b, j: (b, j, 0))
    return pl.pallas_call(
        _outproj_kernel,
        grid=(b_, tn // tm),
        in_specs=[tok, _resident(w.shape),
                  pl.BlockSpec((None, tm, d), lambda b, j: (b, j + xob, 0)),
                  _mod_spec(d, ncb, b_), _mod_spec(d, ncb, b_), _mod_spec(d, ncb, b_),
                  _resident((1, d)), _resident(wr.shape)],
        out_specs=[tok, tok, pl.BlockSpec((None, tm, ROUTER_LANES), lambda b, j: (b, j, 0))],
        out_shape=[jax.ShapeDtypeStruct((b_, tn, d), F32),
                   jax.ShapeDtypeStruct((b_, tn, d), BF16),
                   jax.ShapeDtypeStruct((b_, tn, ROUTER_LANES), F32)],
        compiler_params=_cparams(("parallel", "arbitrary")),
        name="outproj_ffnmod",
    )(a, w, xres, gt, sh, sc, g, wr)


def _expert_kernel(be_ref, nu_ref, x_ref, wg_ref, wu_ref, wd_ref, y_ref):
    i = pl.program_id(0)

    @pl.when(i < nu_ref[0])
    def _():
        x = x_ref[...]
        gate = _dot(x, wg_ref[...])
        up = _dot(x, wu_ref[...])
        act = (gate * _sigmoid(gate) * up).astype(BF16)
        y_ref[...] = _dot(act, wd_ref[...]).astype(y_ref.dtype)

    @pl.when(i >= nu_ref[0])
    def _():
        y_ref[...] = jnp.zeros_like(y_ref)


def _expert_call(buf, block_expert, n_used, wg, wu, wd):
    r_pad, d = buf.shape
    n_blocks = r_pad // MOE_BLOCK
    ff = wg.shape[-1]

    def x_idx(i, be, nu):
        return (jnp.minimum(i, nu[0] - 1), 0)

    grid_spec = pltpu.PrefetchScalarGridSpec(
        num_scalar_prefetch=2,
        grid=(n_blocks,),
        in_specs=[
            pl.BlockSpec((MOE_BLOCK, d), x_idx),
            pl.BlockSpec((None, d, ff), lambda i, be, nu: (be[i], 0, 0)),
            pl.BlockSpec((None, d, ff), lambda i, be, nu: (be[i], 0, 0)),
            pl.BlockSpec((None, ff, d), lambda i, be, nu: (be[i], 0, 0)),
        ],
        out_specs=pl.BlockSpec((MOE_BLOCK, d), lambda i, be, nu: (i, 0)),
    )
    return pl.pallas_call(
        _expert_kernel,
        grid_spec=grid_spec,
        out_shape=jax.ShapeDtypeStruct((r_pad, d), BF16),
        compiler_params=_cparams(("arbitrary",)),
        name="moe_experts",
    )(block_expert, n_used, buf, wg, wu, wd)


def _combine_kernel(*refs, with_next):
    if with_next:
        y0_ref, y1_ref, gw_ref, x_ref, gt_ref, sh_ref, sc_ref, g_ref, xo_ref, hn_ref = refs
    else:
        y0_ref, y1_ref, gw_ref, x_ref, gt_ref, xo_ref = refs
    gw = gw_ref[...]
    f = gw[:, 0:1] * y0_ref[...].astype(F32) + gw[:, 1:2] * y1_ref[...].astype(F32)
    xo = x_ref[...] + gt_ref[...] * f
    xo_ref[...] = xo
    if with_next:
        h = _rms(xo) * g_ref[...]
        hn_ref[...] = (h * (1.0 + sc_ref[...]) + sh_ref[...]).astype(BF16)


def _combine_call(y0, y1, gw, x, gt, n_ctx, tm, next_mod=None):
    b_, tn, d = x.shape
    ncb = n_ctx // tm
    tok = pl.BlockSpec((None, tm, d), lambda b, j: (b, j, 0))
    in_specs = [tok, tok, pl.BlockSpec((None, tm, MOE_TOP_K), lambda b, j: (b, j, 0)),
                tok, _mod_spec(d, ncb, b_)]
    args = [y0, y1, gw, x, gt]
    out_specs = [tok]
    out_shape = [jax.ShapeDtypeStruct((b_, tn, d), F32)]
    with_next = next_mod is not None
    if with_next:
        sh, sc, g = next_mod
        in_specs += [_mod_spec(d, ncb, b_), _mod_spec(d, ncb, b_), _resident((1, d))]
        args += [sh, sc, g]
        out_specs.append(tok)
        out_shape.append(jax.ShapeDtypeStruct((b_, tn, d), BF16))
    return pl.pallas_call(
        functools.partial(_combine_kernel, with_next=with_next),
        grid=(b_, tn // tm),
        in_specs=in_specs,
        out_specs=out_specs,
        out_shape=out_shape,
        compiler_params=_cparams(("parallel", "arbitrary")),
        name="moe_combine",
    )(*args)


def _mla_in_kernel(h_ref, win_ref, qg_ref, wq_ref, wqr_ref, tqc_ref, tqs_ref,
                   ktab_ref, wk_ref, wv_ref, kgain_ref, q_ref, k_ref, v_ref):
    tm = h_ref.shape[0]
    proj = _dot(h_ref[...], win_ref[...])

    cq = proj[:, :MLA_Q_LORA]
    qn = (_rms(cq) * qg_ref[...]).astype(BF16)
    qa = _dot(qn, wq_ref[...])
    qr = _dot(qn, wqr_ref[...])
    tqc = tqc_ref[...]
    tqs = tqs_ref[...]
    for h in range(MLA_HEADS):
        hs = slice(h * LANES, (h + 1) * LANES)
        a = qa[:, hs]
        rq = lax.rsqrt(jnp.sum(a * a, axis=-1, keepdims=True) * (1.0 / MLA_QK) + NORM_EPS)
        q_ref[h] = ((a * tqc + qr[:, hs] * tqs) * rq).astype(BF16)

    p2 = proj[:, MLA_Q_LORA:]
    lane = lax.broadcasted_iota(jnp.int32, (tm, p2.shape[1]), 1)
    is_kv = lane < MLA_KV_LORA
    sq = p2 * p2
    ms_kv = jnp.sum(jnp.where(is_kv, sq, 0.0), axis=-1, keepdims=True) * (1.0 / MLA_KV_LORA)
    is_rope = (lane >= MLA_KV_LORA) & (lane < MLA_KV_LORA + MLA_ROPE)
    ss_rope = jnp.sum(jnp.where(is_rope, sq, 0.0), axis=-1, keepdims=True)
    mult = ktab_ref[...] * jnp.where(is_kv, lax.rsqrt(ms_kv + NORM_EPS), 1.0)
    lhs = (p2 * mult).astype(BF16)
    ka = _dot(lhs, wk_ref[...])
    va = _dot(lhs[:, :MLA_KV_LORA], wv_ref[...])
    kgain = kgain_ref[...]
    lane1 = lax.broadcasted_iota(jnp.int32, (tm, LANES), 1)
    for h in range(MLA_HEADS):
        hs = slice(h * LANES, (h + 1) * LANES)
        a = ka[:, hs]
        ssn = jnp.sum(jnp.where(lane1 < MLA_NOPE, a * a, 0.0), axis=-1, keepdims=True)
        rk = lax.rsqrt((ssn + ss_rope) * (1.0 / MLA_QK) + NORM_EPS)
        k_ref[h] = (a * kgain * rk).astype(BF16)
        v_ref[h] = va[:, hs].astype(BF16)


def _mla_in_call(hm, win, qg, wq, wqr, tqc, tqs, ktab, wk, wv, kgain, tm):
    b_, t_, d = hm.shape
    tok = pl.BlockSpec((None, tm, d), lambda b, j: (b, j, 0))
    head = pl.BlockSpec((None, MLA_HEADS, tm, LANES), lambda b, j: (b, 0, j, 0))
    hshape = jax.ShapeDtypeStruct((b_, MLA_HEADS, t_, LANES), BF16)
    return pl.pallas_call(
        _mla_in_kernel,
        grid=(b_, t_ // tm),
        in_specs=[tok, _resident(win.shape), _resident(qg.shape), _resident(wq.shape),
                  _resident(wqr.shape),
                  pl.BlockSpec((tm, LANES), lambda b, j: (j, 0)),
                  pl.BlockSpec((tm, LANES), lambda b, j: (j, 0)),
                  pl.BlockSpec((tm, ktab.shape[1]), lambda b, j: (j, 0)),
                  _resident(wk.shape), _resident(wv.shape), _resident(kgain.shape)],
        out_specs=[head, head, head],
        out_shape=[hshape, hshape, hshape],
        compiler_params=_cparams(("parallel", "arbitrary")),
        name="mla_in",
    )(hm, win, qg, wq, wqr, tqc, tqs, ktab, wk, wv, kgain)


def _attn_kernel(q_ref, k_ref, v_ref, o_ref):
    acc = None
    for hh in range(2):
        s = _dot_nt(q_ref[hh], k_ref[hh])
        m = jnp.max(s, axis=-1, keepdims=True)
        p = jnp.exp(s - m)
        l = jnp.sum(p, axis=-1, keepdims=True)
        o = _dot(p.astype(BF16), v_ref[hh]) / l
        acc = o if acc is None else acc + o
    o_ref[...] = acc.astype(o_ref.dtype)


def _attn_call(q, k, v, n_ctx, tq):
    b_, nh, t_, _ = q.shape
    l_ = t_ - n_ctx
    qoff = n_ctx // tq
    return pl.pallas_call(
        _attn_kernel,
        grid=(b_, nh // 2, l_ // tq),
        in_specs=[
            pl.BlockSpec((None, 2, tq, LANES), lambda b, h, i: (b, h, i + qoff, 0)),
            pl.BlockSpec((None, 2, t_, LANES), lambda b, h, i: (b, h, 0, 0)),
            pl.BlockSpec((None, 2, t_, LANES), lambda b, h, i: (b, h, 0, 0)),
        ],
        out_specs=pl.BlockSpec((None, tq, LANES), lambda b, h, i: (b, i, h)),
        out_shape=jax.ShapeDtypeStruct((b_, l_, (nh // 2) * LANES), BF16),
        compiler_params=_cparams(("parallel", "parallel", "arbitrary")),
        name="mla_attention",
    )(q, k, v)


def _route(logits):
    n_tok = logits.shape[0]
    g_prob = jax.nn.softmax(logits[:, :MOE_GROUPS], axis=-1)
    g_sel = jnp.argmax(g_prob, axis=-1)
    p_group = jnp.take_along_axis(g_prob, g_sel[:, None], axis=-1)
    e_logits = logits[:, MOE_GROUPS:MOE_GROUPS + MOE_EXPERTS].reshape(n_tok, MOE_GROUPS, MOE_EPG)
    e_logits = jnp.take_along_axis(e_logits, g_sel[:, None, None], axis=1)[:, 0]
    top_p, top_i = lax.top_k(jax.nn.softmax(e_logits, axis=-1), MOE_TOP_K)
    gate_w = p_group * top_p / jnp.sum(top_p, axis=-1, keepdims=True)
    expert_id = (g_sel[:, None] * MOE_EPG + top_i).astype(jnp.int32)

    n_slot = n_tok * MOE_TOP_K
    e_flat = expert_id.reshape(n_slot)
    order = jnp.argsort(e_flat)
    e_sorted = e_flat[order]
    counts = jnp.zeros((MOE_EXPERTS,), jnp.int32).at[e_flat].add(1)
    starts = jnp.cumsum(counts) - counts
    padded = (counts + MOE_BLOCK - 1) // MOE_BLOCK * MOE_BLOCK
    pad_ends = jnp.cumsum(padded)
    pad_starts = pad_ends - padded
    dest_sorted = pad_starts[e_sorted] + jnp.arange(n_slot, dtype=jnp.int32) - starts[e_sorted]
    n_blocks = (n_slot + MOE_BLOCK - 1) // MOE_BLOCK + MOE_EXPERTS
    row_tok = jnp.zeros((n_blocks * MOE_BLOCK,), jnp.int32).at[dest_sorted].set(
        (order // MOE_TOP_K).astype(jnp.int32))
    slot_row = jnp.zeros((n_slot,), jnp.int32).at[order].set(dest_sorted.astype(jnp.int32))
    block_start = jnp.arange(n_blocks, dtype=jnp.int32) * MOE_BLOCK
    block_expert = jnp.minimum(jnp.searchsorted(pad_ends, block_start, side='right'),
                               MOE_EXPERTS - 1).astype(jnp.int32)
    n_used = (pad_ends[-1:] // MOE_BLOCK).astype(jnp.int32)
    return gate_w, row_tok, slot_row.reshape(n_tok, MOE_TOP_K), block_expert, n_used


def _moe(h2, logits, wg, wu, wd):
    b_, tn, d = h2.shape
    n_tok = b_ * tn
    gate_w, row_tok, slot_row, block_expert, n_used = _route(
        logits.reshape(n_tok, ROUTER_LANES)[:, :MOE_GROUPS + MOE_EXPERTS])
    buf = jnp.take(h2.reshape(n_tok, d), row_tok, axis=0)
    ybuf = _expert_call(buf, block_expert, n_used, wg, wu, wd)
    y0 = jnp.take(ybuf, slot_row[:, 0], axis=0).reshape(b_, tn, d)
    y1 = jnp.take(ybuf, slot_row[:, 1], axis=0).reshape(b_, tn, d)
    return y0, y1, gate_w.reshape(b_, tn, MOE_TOP_K)


def _rot_half_perm():
    half = MLA_ROPE // 2
    j = jnp.arange(MLA_ROPE)
    within = j % half
    base = j - within
    src = jnp.where(within < half // 2, base + within + half // 2, base + within - half // 2)
    sign = jnp.where(within < half // 2, -1.0, 1.0).astype(F32)
    return src, sign


def _rope_tables(n_ctx, seq):
    rows = seq // GRID_W
    row = jnp.repeat(jnp.arange(rows), GRID_W)
    col = jnp.tile(jnp.arange(GRID_W), rows)
    half = MLA_ROPE // 2
    inv_freq = ROPE_THETA ** (-jnp.arange(0, half, 2, dtype=F32) / half)
    ang = jnp.stack([row, col], axis=-1).astype(F32)[..., None] * inv_freq
    ang = jnp.concatenate([ang, ang], axis=-1).reshape(seq, MLA_ROPE)
    cos = jnp.concatenate([jnp.ones((n_ctx, MLA_ROPE), F32), jnp.cos(ang)], axis=0)
    sin = jnp.concatenate([jnp.zeros((n_ctx, MLA_ROPE), F32), jnp.sin(ang)], axis=0)
    return cos, sin


def _mla_prepare(w_in, w_qb, w_kvb, q_qk_g, k_qk_g, kv_norm_g, n_ctx, seq):
    d = w_in.shape[0]
    src, sign = _rot_half_perm()
    cos, sin = _rope_tables(n_ctx, seq)
    t_ = n_ctx + seq

    rope0 = MLA_Q_LORA + MLA_KV_LORA
    w_rope = w_in[:, rope0:rope0 + MLA_ROPE]
    win = jnp.concatenate(
        [w_in, w_rope[:, src] * sign, jnp.zeros((d, 512 - rope0 - 2 * MLA_ROPE), F32)], axis=1)

    wq3 = w_qb.reshape(MLA_Q_LORA, MLA_HEADS, MLA_QK)
    pad = jnp.zeros((MLA_Q_LORA, MLA_HEADS, LANES - MLA_QK), F32)
    wq = jnp.concatenate([wq3, pad], axis=-1).reshape(MLA_Q_LORA, MLA_HEADS * LANES)
    wq_rot = wq3[:, :, MLA_NOPE:][:, :, src] * sign
    wqr = jnp.concatenate([jnp.zeros((MLA_Q_LORA, MLA_HEADS, MLA_NOPE), F32), wq_rot, pad],
                          axis=-1).reshape(MLA_Q_LORA, MLA_HEADS * LANES)

    scale = MLA_QK ** -0.5
    gq_n, gq_r = q_qk_g[:MLA_NOPE], q_qk_g[MLA_NOPE:]
    zpad = jnp.zeros((t_, LANES - MLA_QK), F32)
    tqc = jnp.concatenate([jnp.broadcast_to(gq_n, (t_, MLA_NOPE)), gq_r * cos, zpad], axis=1) * scale
    tqs = jnp.concatenate([jnp.zeros((t_, MLA_NOPE), F32), gq_r[src] * sin, zpad], axis=1) * scale

    wkv3 = w_kvb.reshape(MLA_KV_LORA, MLA_HEADS, MLA_NOPE + MLA_V)
    wk_lat = jnp.concatenate(
        [wkv3[:, :, :MLA_NOPE], jnp.zeros((MLA_KV_LORA, MLA_HEADS, LANES - MLA_NOPE), F32)], axis=-1)
    place = jnp.concatenate([jnp.zeros((MLA_ROPE, MLA_NOPE), F32), jnp.eye(MLA_ROPE, dtype=F32),
                             jnp.zeros((MLA_ROPE, LANES - MLA_QK), F32)], axis=1)
    place = jnp.broadcast_to(place[:, None, :], (MLA_ROPE, MLA_HEADS, LANES))
    wk = jnp.concatenate([wk_lat, place, place,
                          jnp.zeros((256 - MLA_KV_LORA - 2 * MLA_ROPE, MLA_HEADS, LANES), F32)],
                         axis=0).reshape(256, MLA_HEADS * LANES)
    gk_n, gk_r = k_qk_g[:MLA_NOPE], k_qk_g[MLA_NOPE:]
    ktab = jnp.concatenate([jnp.broadcast_to(kv_norm_g, (t_, MLA_KV_LORA)), gk_r * cos,
                            gk_r[src] * sin, jnp.zeros((t_, 256 - MLA_KV_LORA - 2 * MLA_ROPE), F32)],
                           axis=1)
    kgain = jnp.concatenate([gk_n, jnp.ones((MLA_ROPE,), F32),
                             jnp.zeros((LANES - MLA_QK,), F32)]).reshape(1, LANES)

    wv_h = wkv3[:, :, MLA_NOPE:]
    zv = jnp.zeros_like(wv_h)
    odd = (jnp.arange(MLA_HEADS) % 2 == 1)[None, :, None]
    wv = jnp.concatenate([jnp.where(odd, zv, wv_h), jnp.where(odd, wv_h, zv)],
                         axis=-1).reshape(MLA_KV_LORA, MLA_HEADS * LANES)
    return (win.astype(BF16), wq.astype(BF16), wqr.astype(BF16), tqc, tqs, ktab,
            wk.astype(BF16), wv.astype(BF16), kgain)


def _router_weights(w_group, w_expert):
    d = w_group.shape[0]
    return jnp.concatenate(
        [w_group, w_expert, jnp.zeros((d, ROUTER_LANES - MOE_GROUPS - MOE_EXPERTS), F32)], axis=1)


def kernel(x, c, ctx, c_ctx, ada_w, ada_b, norm_mix_g, norm_ffn_g, hg_w_in, hg_lower_bounds, hg_out_norm_g, hg_w_out, mla_w_in, mla_q_norm_g, mla_kv_norm_g, mla_w_qb, mla_w_kvb, mla_q_qknorm_g, mla_k_qknorm_g, mla_w_out, moe_w_group, moe_w_expert, moe_w_gate, moe_w_up, moe_w_down):
    b_, seq, d = x.shape
    n_ctx = ctx.shape[1]
    t_ = n_ctx + seq
    tm = min(ROW_TILE, n_ctx)
    tq = min(ATT_Q_TILE, n_ctx)
    assert n_ctx % GLA_BLOCK == 0 and seq % GLA_BLOCK == 0 and n_ctx % tm == 0 and seq % tm == 0
    assert b_ + 1 <= ADA_ROWS and seq % GRID_W == 0

    cc = jnp.concatenate([c, c_ctx[None, :], jnp.zeros((ADA_ROWS - b_ - 1, d), F32)], axis=0)
    mods = _ada_call(cc, ada_w, ada_b).reshape(DEPTH, ADA_CHUNKS, ADA_ROWS, 1, d)

    def mod(i, chunk):
        return mods[i, chunk]

    row = lambda v: v.reshape(1, -1)
    xcat = jnp.concatenate([ctx, x], axis=1)

    lower = jnp.cumsum(jax.nn.softmax(hg_lower_bounds.astype(F32), axis=1), axis=1)[:, 0]
    qs, kf, gf, kb, gb, v, gs = _hg_in_call(
        xcat, mod(0, 0), mod(0, 1), row(norm_mix_g[0]), hg_w_in[0].astype(BF16), lower, n_ctx, tm)
    o_b = _gla_call(qs, kb, gb, v, n_ctx, reverse=True)
    og = _gla_call(qs, kf, gf, v, n_ctx, reverse=False,
                   final_args=(o_b, gs, row(hg_out_norm_g[0])))
    x1, h2, lg = _outproj_call(
        og, hg_w_out[0].astype(BF16), xcat, 0, mod(0, 2), mod(0, 3), mod(0, 4),
        row(norm_ffn_g[0]), _router_weights(moe_w_group[0], moe_w_expert[0]), n_ctx, tm)
    y0, y1, gw = _moe(h2, lg, moe_w_gate[0].astype(BF16), moe_w_up[0].astype(BF16),
                      moe_w_down[0].astype(BF16))
    x2, hm = _combine_call(y0, y1, gw, x1, mod(0, 5), n_ctx, tm,
                           next_mod=(mod(1, 0), mod(1, 1), row(norm_mix_g[1])))

    win, wq, wqr, tqc, tqs, ktab, wk, wv, kgain = _mla_prepare(
        mla_w_in[0], mla_w_qb[0], mla_w_kvb[0], mla_q_qknorm_g[0], mla_k_qknorm_g[0],
        mla_kv_norm_g[0], n_ctx, seq)
    q, k, vv = _mla_in_call(hm, win, row(mla_q_norm_g[0]), wq, wqr, tqc, tqs, ktab, wk, wv, kgain, tm)
    o_att = _attn_call(q, k, vv, n_ctx, tq)
    x3, h2, lg = _outproj_call(
        o_att, mla_w_out[0].astype(BF16), x2, n_ctx, mod(1, 2), mod(1, 3), mod(1, 4),
        row(norm_ffn_g[1]), _router_weights(moe_w_group[1], moe_w_expert[1]), 0, tm)
    y0, y1, gw = _moe(h2, lg, moe_w_gate[1].astype(BF16), moe_w_up[1].astype(BF16),
                      moe_w_down[1].astype(BF16))
    (x4,) = _combine_call(y0, y1, gw, x3, mod(1, 5), 0, tm)
    return x4
```

```python
import functools

import jax
import jax.numpy as jnp
from jax import lax
from jax.experimental import pallas as pl
from jax.experimental.pallas import tpu as pltpu

F32 = jnp.float32
BF16 = jnp.bfloat16

DEPTH = 2
ADA_CHUNKS = 6
NORM_EPS = 1e-6
GRID_W = 64
HG_HEADS = 8
HG_CHUNK = 32
MLA_HEADS = 16
MLA_Q_LORA = 256
MLA_KV_LORA = 128
MLA_NOPE = 64
MLA_ROPE = 32
MLA_V = 64
MLA_QK = MLA_NOPE + MLA_ROPE
ROPE_THETA = 10000.0
MOE_GROUPS = 4
MOE_EPG = 8
MOE_EXPERTS = MOE_GROUPS * MOE_EPG
MOE_TOP_K = 2
MOE_FF = 512
MOE_BLOCK = 256

LANES = 128
ROUTER_LANES = 128
VMEM_LIMIT = 56 * 1024 * 1024

ROW_TILE = 256
GLA_BLOCK = 128
ATT_Q_TILE = 512
ATT_HEADS_PER_STEP = 4
ATT_KEY_CHUNKS = (256, 128)
ADA_ROWS = 40


def _cparams(sem):
    return pltpu.CompilerParams(dimension_semantics=sem, vmem_limit_bytes=VMEM_LIMIT)


def _resident(shape):
    nd = len(shape)
    return pl.BlockSpec(shape, lambda *_: (0,) * nd, pipeline_mode=pl.Buffered(1))


def _dot(a, b):
    return jnp.dot(a, b, preferred_element_type=F32)


def _dot_nt(a, b):
    return lax.dot_general(a, b, (((1,), (1,)), ((), ())), preferred_element_type=F32)


def _dot_tn(a, b):
    return lax.dot_general(a, b, (((0,), (0,)), ((), ())), preferred_element_type=F32)


def _sigmoid(x):
    return 1.0 / (1.0 + jnp.exp(-x))


def _rms(x):
    return x * lax.rsqrt(jnp.mean(x * x, axis=-1, keepdims=True) + NORM_EPS)


def _ada_kernel(c_ref, w_ref, b_ref, o_ref):
    c = c_ref[...]
    a = (c * _sigmoid(c)).astype(BF16)
    o_ref[...] = _dot(a, w_ref[...].astype(BF16)) + b_ref[...]


def _ada_call(cc, ada_w, ada_b):
    depth, d, _ = ada_w.shape
    rows = cc.shape[0]
    return pl.pallas_call(
        _ada_kernel,
        grid=(depth, ADA_CHUNKS),
        in_specs=[
            pl.BlockSpec((rows, d), lambda i, j: (0, 0)),
            pl.BlockSpec((None, d, d), lambda i, j: (i, 0, j)),
            pl.BlockSpec((None, 1, d), lambda i, j: (i, 0, j)),
        ],
        out_specs=pl.BlockSpec((None, None, rows, d), lambda i, j: (i, j, 0, 0)),
        out_shape=jax.ShapeDtypeStruct((depth, ADA_CHUNKS, rows, d), F32),
        compiler_params=_cparams(("arbitrary", "arbitrary")),
        name="ada_mod",
    )(cc, ada_w, ada_b.reshape(depth, 1, ADA_CHUNKS * d))


def _mod_spec(d, n_ctx_blocks, ctx_row):
    def idx(b, j):
        return (jnp.where(j < n_ctx_blocks, ctx_row, b), 0, 0)
    return pl.BlockSpec((None, 1, d), idx)


def _hg_in_kernel(x_ref, sh_ref, sc_ref, g_ref, w_ref, lb_ref,
                  q_ref, kf_ref, gf_ref, kb_ref, gb_ref, v_ref, gs_ref):
    d = x_ref.shape[-1]
    h = _rms(x_ref[...]) * g_ref[...]
    h = (h * (1.0 + sc_ref[...]) + sh_ref[...]).astype(BF16)

    def proj(c):
        return _dot(h, w_ref[:, c * d:(c + 1) * d])

    p = proj(0)
    q_ref[...] = (p * _sigmoid(p)).astype(BF16)
    for c, k_ref, lg_ref in ((1, kf_ref, gf_ref), (2, kb_ref, gb_ref)):
        s = _sigmoid(proj(c))
        lb = lb_ref[c - 1:c, :]
        k_ref[...] = ((1.0 - lb) * (1.0 - s)).astype(BF16)
        lg_ref[...] = jnp.log(lb + (1.0 - lb) * s).astype(BF16)
    v_ref[...] = proj(3).astype(BF16)
    p = proj(4)
    gs_ref[...] = (p * _sigmoid(p)).astype(BF16)


def _hg_in_call(xcat, sh, sc, g, w_in, lb, n_ctx, tm):
    b_, t_, d = xcat.shape
    ncb = n_ctx // tm
    tok = pl.BlockSpec((None, tm, d), lambda b, j: (b, j, 0))
    out = jax.ShapeDtypeStruct((b_, t_, d), BF16)
    return pl.pallas_call(
        _hg_in_kernel,
        grid=(b_, t_ // tm),
        in_specs=[tok, _mod_spec(d, ncb, b_), _mod_spec(d, ncb, b_),
                  _resident((1, d)), _resident(w_in.shape), _resident((2, d))],
        out_specs=[tok] * 7,
        out_shape=[out] * 7,
        compiler_params=_cparams(("parallel", "arbitrary")),
        name="hg_in",
    )(xcat, sh, sc, g, w_in, lb)


def _gla_kernel(*refs, reverse, final, n_heads):
    if final:
        q_ref, k_ref, g_ref, v_ref, ob_ref, gs_ref, gn_ref, o_ref, st_ref = refs
    else:
        q_ref, k_ref, g_ref, v_ref, o_ref, st_ref = refs
    tb, d = q_ref.shape
    dh = d // n_heads
    n_chunks = tb // HG_CHUNK

    @pl.when(pl.program_id(1) == 0)
    def _():
        st_ref[...] = jnp.zeros_like(st_ref)

    row = lax.broadcasted_iota(jnp.int32, (tb, tb), 0)
    col = lax.broadcasted_iota(jnp.int32, (tb, tb), 1)
    same_chunk = (row // HG_CHUNK) == (col // HG_CHUNK)
    tri = same_chunk & ((col >= row) if reverse else (col <= row))
    cum = jnp.where(tri, 1.0, 0.0).astype(BF16)

    b = _dot(cum, g_ref[...])
    qd = (q_ref[...].astype(F32) * jnp.exp(b)).astype(BF16)
    ki = k_ref[...].astype(F32) * jnp.exp(-b)
    ki_b = ki.astype(BF16)

    chunk_order = range(n_chunks - 1, -1, -1) if reverse else range(n_chunks)
    e_end, k_end = {}, {}
    for c in range(n_chunks):
        r0 = c * HG_CHUNK
        last = r0 if reverse else r0 + HG_CHUNK - 1
        e_end[c] = jnp.exp(b[last:last + 1, :])
        k_end[c] = (ki[r0:r0 + HG_CHUNK, :] * e_end[c]).astype(BF16)

    for h in range(n_heads):
        hs = slice(h * dh, (h + 1) * dh)
        qh = qd[:, hs]
        vh = v_ref[:, hs]
        scores = jnp.where(tri, _dot_nt(qh, ki_b[:, hs]), 0.0).astype(BF16)
        o_intra = _dot(scores, vh)
        st = st_ref[h]
        for c in chunk_order:
            rs = slice(c * HG_CHUNK, (c + 1) * HG_CHUNK)
            o_c = o_intra[rs, :] + _dot_nt(qh[rs, :], st.astype(BF16))
            st = st * e_end[c][:, hs] + _dot_tn(vh[rs, :], k_end[c][:, hs])
            if final:
                o_c = o_c + ob_ref[rs, hs].astype(F32)
                o_c = _rms(o_c) * gn_ref[...]
                o_c = o_c * gs_ref[rs, hs].astype(F32)
            o_ref[rs, hs] = o_c.astype(o_ref.dtype)
        st_ref[h] = st


def _gla_call(q, k, g, v, n_ctx, reverse, final_args=None):
    b_, t_, d = q.shape
    tb = GLA_BLOCK
    nb = t_ // tb
    ncb = n_ctx // tb
    dh = d // HG_HEADS

    def blk(b, j):
        if reverse:
            jj = jnp.where(j < ncb, ncb - 1 - j, nb - 1 - (j - ncb))
        else:
            jj = j
        return (b, jj, 0)

    tok = pl.BlockSpec((None, tb, d), blk)
    in_specs = [tok] * 4
    args = [q, k, g, v]
    final = final_args is not None
    if final:
        o_b, gs, gn = final_args
        in_specs += [tok, tok, _resident((1, dh))]
        args += [o_b, gs, gn]
    return pl.pallas_call(
        functools.partial(_gla_kernel, reverse=reverse, final=final, n_heads=HG_HEADS),
        grid=(b_, nb),
        in_specs=in_specs,
        out_specs=tok,
        out_shape=jax.ShapeDtypeStruct((b_, t_, d), BF16),
        scratch_shapes=[pltpu.VMEM((HG_HEADS, dh, dh), F32)],
        compiler_params=_cparams(("parallel", "arbitrary")),
        name="gla_bwd" if reverse else "gla_fwd",
    )(*args)


def _outproj_kernel(a_ref, w_ref, x_ref, gt_ref, sh_ref, sc_ref, g_ref, wr_ref,
                    xn_ref, h2_ref, eid_ref, gw_ref, rank_ref, cnt_ref, run_ref):
    tm = a_ref.shape[0]
    first = (pl.program_id(0) == 0) & (pl.program_id(1) == 0)

    @pl.when(first)
    def _():
        run_ref[...] = jnp.zeros_like(run_ref)

    y = _dot(a_ref[...], w_ref[...])
    xn = x_ref[...] + gt_ref[...] * y
    xn_ref[...] = xn
    h2 = _rms(xn) * g_ref[...]
    h2 = h2 * (1.0 + sc_ref[...]) + sh_ref[...]
    h2_ref[...] = h2.astype(BF16)
    lg = jnp.dot(h2, wr_ref[...], preferred_element_type=F32,
                 precision=lax.Precision.HIGHEST)

    lane = lax.broadcasted_iota(jnp.int32, lg.shape, 1)
    neg = -jnp.inf
    gl = jnp.where(lane < MOE_GROUPS, lg, neg)
    gmax = jnp.max(gl, axis=-1, keepdims=True)
    g_idx = jnp.min(jnp.where(gl == gmax, lane, ROUTER_LANES), axis=-1, keepdims=True)
    p_group = 1.0 / jnp.sum(jnp.exp(gl - gmax), axis=-1, keepdims=True)
    lo = MOE_GROUPS + MOE_EPG * g_idx
    el = jnp.where((lane >= lo) & (lane < lo + MOE_EPG), lg, neg)
    m1 = jnp.max(el, axis=-1, keepdims=True)
    i1 = jnp.min(jnp.where(el == m1, lane, ROUTER_LANES), axis=-1, keepdims=True)
    el2 = jnp.where(lane == i1, neg, el)
    m2 = jnp.max(el2, axis=-1, keepdims=True)
    i2 = jnp.min(jnp.where(el2 == m2, lane, ROUTER_LANES), axis=-1, keepdims=True)
    r21 = jnp.exp(m2 - m1)
    w1 = p_group / (1.0 + r21)
    w2 = w1 * r21

    hot1 = lane == i1
    hot2 = lane == i2
    hot = jnp.where(hot1 | hot2, 1.0, 0.0)
    r = lax.broadcasted_iota(jnp.int32, (tm, tm), 0)
    c = lax.broadcasted_iota(jnp.int32, (tm, tm), 1)
    before = jnp.where(c < r, 1.0, 0.0).astype(BF16)
    tot = _dot(before, hot.astype(BF16)) + run_ref[...]
    rank1 = jnp.sum(jnp.where(hot1, tot, 0.0), axis=-1, keepdims=True)
    rank2 = jnp.sum(jnp.where(hot2, tot, 0.0), axis=-1, keepdims=True)
    run = run_ref[...] + jnp.sum(hot, axis=0, keepdims=True)
    run_ref[...] = run
    cnt_ref[...] = run

    k2 = lax.broadcasted_iota(jnp.int32, (tm, MOE_TOP_K), 1)
    eid_ref[...] = jnp.where(k2 == 0, i1, i2) - MOE_GROUPS
    gw_ref[...] = jnp.where(k2 == 0, w1, w2)
    rank_ref[...] = jnp.where(k2 == 0, rank1, rank2).astype(jnp.int32)


def _outproj_call(a, w, xres, x_off, gt, sh, sc, g, wr, n_ctx, tm):
    b_, tn, d = a.shape
    ncb = n_ctx // tm
    xob = x_off // tm
    tok = pl.BlockSpec((None, tm, d), lambda b, j: (b, j, 0))
    pair = pl.BlockSpec((None, tm, MOE_TOP_K), lambda b, j: (b, j, 0))
    return pl.pallas_call(
        _outproj_kernel,
        grid=(b_, tn // tm),
        in_specs=[tok, _resident(w.shape),
                  pl.BlockSpec((None, tm, d), lambda b, j: (b, j + xob, 0)),
                  _mod_spec(d, ncb, b_), _mod_spec(d, ncb, b_), _mod_spec(d, ncb, b_),
                  _resident((1, d)), _resident(wr.shape)],
        out_specs=[tok, tok, pair, pair, pair,
                   pl.BlockSpec((1, ROUTER_LANES), lambda b, j: (0, 0))],
        out_shape=[jax.ShapeDtypeStruct((b_, tn, d), F32),
                   jax.ShapeDtypeStruct((b_, tn, d), BF16),
                   jax.ShapeDtypeStruct((b_, tn, MOE_TOP_K), jnp.int32),
                   jax.ShapeDtypeStruct((b_, tn, MOE_TOP_K), F32),
                   jax.ShapeDtypeStruct((b_, tn, MOE_TOP_K), jnp.int32),
                   jax.ShapeDtypeStruct((1, ROUTER_LANES), F32)],
        scratch_shapes=[pltpu.VMEM((1, ROUTER_LANES), F32)],
        compiler_params=_cparams(("arbitrary", "arbitrary")),
        name="outproj_ffnmod",
    )(a, w, xres, gt, sh, sc, g, wr)


def _expert_kernel(be_ref, nu_ref, x_ref, wg_ref, wu_ref, wd_ref, y_ref, wgb_ref, wub_ref, wdb_ref):
    i = pl.program_id(0)
    new_expert = (i == 0) | (be_ref[i] != be_ref[jnp.maximum(i - 1, 0)])

    @pl.when(new_expert)
    def _():
        wgb_ref[...] = wg_ref[...].astype(BF16)
        wub_ref[...] = wu_ref[...].astype(BF16)
        wdb_ref[...] = wd_ref[...].astype(BF16)

    @pl.when(i < nu_ref[0])
    def _():
        x = x_ref[...]
        gate = _dot(x, wgb_ref[...])
        up = _dot(x, wub_ref[...])
        act = (gate * _sigmoid(gate) * up).astype(BF16)
        y_ref[...] = _dot(act, wdb_ref[...]).astype(y_ref.dtype)

    @pl.when(i >= nu_ref[0])
    def _():
        y_ref[...] = jnp.zeros_like(y_ref)


def _expert_call(buf, block_expert, n_used, wg, wu, wd):
    r_pad, d = buf.shape
    n_blocks = r_pad // MOE_BLOCK
    ff = wg.shape[-1]

    def x_idx(i, be, nu):
        return (jnp.minimum(i, nu[0] - 1), 0)

    grid_spec = pltpu.PrefetchScalarGridSpec(
        num_scalar_prefetch=2,
        grid=(n_blocks,),
        in_specs=[
            pl.BlockSpec((MOE_BLOCK, d), x_idx),
            pl.BlockSpec((None, d, ff), lambda i, be, nu: (be[i], 0, 0)),
            pl.BlockSpec((None, d, ff), lambda i, be, nu: (be[i], 0, 0)),
            pl.BlockSpec((None, ff, d), lambda i, be, nu: (be[i], 0, 0)),
        ],
        out_specs=pl.BlockSpec((MOE_BLOCK, d), lambda i, be, nu: (i, 0)),
        scratch_shapes=[pltpu.VMEM((d, ff), BF16), pltpu.VMEM((d, ff), BF16),
                        pltpu.VMEM((ff, d), BF16)],
    )
    return pl.pallas_call(
        _expert_kernel,
        grid_spec=grid_spec,
        out_shape=jax.ShapeDtypeStruct((r_pad, d), BF16),
        compiler_params=_cparams(("arbitrary",)),
        name="moe_experts",
    )(block_expert, n_used, buf, wg, wu, wd)


def _combine_kernel(*refs, with_next):
    if with_next:
        y0_ref, y1_ref, gw_ref, x_ref, gt_ref, sh_ref, sc_ref, g_ref, xo_ref, hn_ref = refs
    else:
        y0_ref, y1_ref, gw_ref, x_ref, gt_ref, xo_ref = refs
    gw = gw_ref[...]
    f = gw[:, 0:1] * y0_ref[...].astype(F32) + gw[:, 1:2] * y1_ref[...].astype(F32)
    xo = x_ref[...] + gt_ref[...] * f
    xo_ref[...] = xo
    if with_next:
        h = _rms(xo) * g_ref[...]
        hn_ref[...] = (h * (1.0 + sc_ref[...]) + sh_ref[...]).astype(BF16)


def _combine_call(y0, y1, gw, x, gt, n_ctx, tm, next_mod=None):
    b_, tn, d = x.shape
    ncb = n_ctx // tm
    tok = pl.BlockSpec((None, tm, d), lambda b, j: (b, j, 0))
    in_specs = [tok, tok, pl.BlockSpec((None, tm, MOE_TOP_K), lambda b, j: (b, j, 0)),
                tok, _mod_spec(d, ncb, b_)]
    args = [y0, y1, gw, x, gt]
    out_specs = [tok]
    out_shape = [jax.ShapeDtypeStruct((b_, tn, d), F32)]
    with_next = next_mod is not None
    if with_next:
        sh, sc, g = next_mod
        in_specs += [_mod_spec(d, ncb, b_), _mod_spec(d, ncb, b_), _resident((1, d))]
        args += [sh, sc, g]
        out_specs.append(tok)
        out_shape.append(jax.ShapeDtypeStruct((b_, tn, d), BF16))
    return pl.pallas_call(
        functools.partial(_combine_kernel, with_next=with_next),
        grid=(b_, tn // tm),
        in_specs=in_specs,
        out_specs=out_specs,
        out_shape=out_shape,
        compiler_params=_cparams(("parallel", "arbitrary")),
        name="moe_combine",
    )(*args)


def _mla_in_kernel(h_ref, win_ref, qg_ref, wq_ref, wqr_ref, tqc_ref, tqs_ref,
                   ktab_ref, wk_ref, wv_ref, kgain_ref, q_ref, k_ref, v_ref, *, n_ctx_blocks):
    tm = h_ref.shape[0]
    proj = _dot(h_ref[...], win_ref[...])

    @pl.when(pl.program_id(1) >= n_ctx_blocks)
    def _():
        cq = proj[:, :MLA_Q_LORA]
        qn = (_rms(cq) * qg_ref[...]).astype(BF16)
        qa = _dot(qn, wq_ref[...])
        qr = _dot(qn, wqr_ref[...])
        tqc = tqc_ref[...]
        tqs = tqs_ref[...]
        for h in range(MLA_HEADS):
            hs = slice(h * LANES, (h + 1) * LANES)
            a = qa[:, hs]
            rq = lax.rsqrt(jnp.sum(a * a, axis=-1, keepdims=True) * (1.0 / MLA_QK) + NORM_EPS)
            q_ref[h] = ((a * tqc + qr[:, hs] * tqs) * rq).astype(BF16)

    p2 = proj[:, MLA_Q_LORA:]
    lane = lax.broadcasted_iota(jnp.int32, (tm, p2.shape[1]), 1)
    is_kv = lane < MLA_KV_LORA
    sq = p2 * p2
    ms_kv = jnp.sum(jnp.where(is_kv, sq, 0.0), axis=-1, keepdims=True) * (1.0 / MLA_KV_LORA)
    is_rope = (lane >= MLA_KV_LORA) & (lane < MLA_KV_LORA + MLA_ROPE)
    ss_rope = jnp.sum(jnp.where(is_rope, sq, 0.0), axis=-1, keepdims=True)
    mult = ktab_ref[...] * jnp.where(is_kv, lax.rsqrt(ms_kv + NORM_EPS), 1.0)
    lhs = (p2 * mult).astype(BF16)
    ka = _dot(lhs, wk_ref[...])
    va = _dot(lhs[:, :MLA_KV_LORA], wv_ref[...])
    kgain = kgain_ref[...]
    lane1 = lax.broadcasted_iota(jnp.int32, (tm, LANES), 1)
    for h in range(MLA_HEADS):
        hs = slice(h * LANES, (h + 1) * LANES)
        a = ka[:, hs]
        ssn = jnp.sum(jnp.where(lane1 < MLA_NOPE, a * a, 0.0), axis=-1, keepdims=True)
        rk = lax.rsqrt((ssn + ss_rope) * (1.0 / MLA_QK) + NORM_EPS)
        k_ref[h] = (a * kgain * rk).astype(BF16)
        ones_lane = MLA_V if h % 2 == 0 else 0
        v_ref[h] = jnp.where(lane1 == ones_lane, 1.0, va[:, hs]).astype(BF16)


def _mla_in_call(hm, win, qg, wq, wqr, tqc, tqs, ktab, wk, wv, kgain, n_ctx, tm):
    b_, t_, d = hm.shape
    ncb = n_ctx // tm
    tok = pl.BlockSpec((None, tm, d), lambda b, j: (b, j, 0))
    head = pl.BlockSpec((None, MLA_HEADS, tm, LANES), lambda b, j: (b, 0, j, 0))
    hshape = jax.ShapeDtypeStruct((b_, MLA_HEADS, t_, LANES), BF16)
    qhead = pl.BlockSpec((None, MLA_HEADS, tm, LANES), lambda b, j: (b, 0, jnp.maximum(j - ncb, 0), 0))
    qshape = jax.ShapeDtypeStruct((b_, MLA_HEADS, t_ - n_ctx, LANES), BF16)
    return pl.pallas_call(
        functools.partial(_mla_in_kernel, n_ctx_blocks=ncb),
        grid=(b_, t_ // tm),
        in_specs=[tok, _resident(win.shape), _resident(qg.shape), _resident(wq.shape),
                  _resident(wqr.shape),
                  pl.BlockSpec((tm, LANES), lambda b, j: (j, 0)),
                  pl.BlockSpec((tm, LANES), lambda b, j: (j, 0)),
                  pl.BlockSpec((tm, ktab.shape[1]), lambda b, j: (j, 0)),
                  _resident(wk.shape), _resident(wv.shape), _resident(kgain.shape)],
        out_specs=[qhead, head, head],
        out_shape=[qshape, hshape, hshape],
        compiler_params=_cparams(("parallel", "arbitrary")),
        name="mla_in",
    )(hm, win, qg, wq, wqr, tqc, tqs, ktab, wk, wv, kgain)


def _attn_kernel(q_ref, k_ref, v_ref, o_ref):
    n_heads, tq, _ = q_ref.shape
    t_ = k_ref.shape[1]
    ck = next(c for c in ATT_KEY_CHUNKS if t_ % c == 0)
    outs = []
    for hh in range(n_heads):
        q = q_ref[hh]
        m = jnp.full((tq, 1), -jnp.inf, F32)
        acc = jnp.zeros((tq, LANES), F32)
        for c in range(t_ // ck):
            ks = slice(c * ck, (c + 1) * ck)
            s = _dot_nt(q, k_ref[hh, ks, :])
            m_new = jnp.maximum(m, jnp.max(s, axis=-1, keepdims=True))
            p = jnp.exp2((s - m_new).astype(BF16))
            acc = acc * jnp.exp2(m - m_new) + _dot(p, v_ref[hh, ks, :])
            m = m_new
        ones_lane = MLA_V if hh % 2 == 0 else 0
        outs.append(acc / acc[:, ones_lane:ones_lane + 1])
    lane = lax.broadcasted_iota(jnp.int32, outs[0].shape, 1)
    for pr in range(len(outs) // 2):
        o_ref[:, pr * LANES:(pr + 1) * LANES] = jnp.where(
            lane < MLA_V, outs[2 * pr], outs[2 * pr + 1]).astype(o_ref.dtype)


def _attn_call(q, k, v, tq):
    b_, nh, l_, _ = q.shape
    t_ = k.shape[2]
    hps = ATT_HEADS_PER_STEP
    return pl.pallas_call(
        _attn_kernel,
        grid=(b_, nh // hps, l_ // tq),
        in_specs=[
            pl.BlockSpec((None, hps, tq, LANES), lambda b, h, i: (b, h, i, 0)),
            pl.BlockSpec((None, hps, t_, LANES), lambda b, h, i: (b, h, 0, 0)),
            pl.BlockSpec((None, hps, t_, LANES), lambda b, h, i: (b, h, 0, 0)),
        ],
        out_specs=pl.BlockSpec((None, tq, (hps // 2) * LANES), lambda b, h, i: (b, i, h)),
        out_shape=jax.ShapeDtypeStruct((b_, l_, (nh // 2) * LANES), BF16),
        compiler_params=_cparams(("parallel", "parallel", "arbitrary")),
        name="mla_attention",
    )(q, k, v)


def _dispatch_layout(eid, rank, counts):
    n_slot = eid.size
    starts = jnp.cumsum(counts) - counts
    padded = (counts + MOE_BLOCK - 1) // MOE_BLOCK * MOE_BLOCK
    pad_ends = jnp.cumsum(padded)
    pad_starts = pad_ends - padded
    n_blocks = (n_slot + MOE_BLOCK - 1) // MOE_BLOCK + MOE_EXPERTS
    block_start = jnp.arange(n_blocks, dtype=jnp.int32) * MOE_BLOCK
    block_expert = jnp.minimum(
        jnp.sum((block_start[:, None] >= pad_ends[None, :]).astype(jnp.int32), axis=1),
        MOE_EXPERTS - 1)
    n_used = pad_ends[-1:] // MOE_BLOCK
    slot_row = jnp.take(pad_starts, eid, mode='clip') + rank
    order = jnp.argsort(eid.reshape(n_slot)).astype(jnp.int32)
    offs = (block_start - pad_starts[block_expert])[:, None] + jnp.arange(MOE_BLOCK, dtype=jnp.int32)
    valid = offs < counts[block_expert][:, None]
    pos = jnp.minimum(starts[block_expert][:, None] + offs, n_slot - 1)
    row_tok = jnp.where(valid, jnp.take(order, pos, mode='clip') // MOE_TOP_K, 0)
    return row_tok.reshape(-1), slot_row, block_expert, n_used


def _moe(h2, eid, rank, counts, wg, wu, wd):
    b_, tn, d = h2.shape
    n_tok = b_ * tn
    cnt = counts[0, MOE_GROUPS:MOE_GROUPS + MOE_EXPERTS].astype(jnp.int32)
    row_tok, slot_row, block_expert, n_used = _dispatch_layout(
        eid.reshape(n_tok, MOE_TOP_K), rank.reshape(n_tok, MOE_TOP_K), cnt)
    buf = jnp.take(h2.reshape(n_tok, d), row_tok, axis=0, mode='clip')
    ybuf = _expert_call(buf, block_expert, n_used, wg, wu, wd)
    y0 = jnp.take(ybuf, slot_row[:, 0], axis=0, mode='clip').reshape(b_, tn, d)
    y1 = jnp.take(ybuf, slot_row[:, 1], axis=0, mode='clip').reshape(b_, tn, d)
    return y0, y1


def _rot_half_perm():
    half = MLA_ROPE // 2
    j = jnp.arange(MLA_ROPE)
    within = j % half
    base = j - within
    src = jnp.where(within < half // 2, base + within + half // 2, base + within - half // 2)
    sign = jnp.where(within < half // 2, -1.0, 1.0).astype(F32)
    return src, sign


def _rope_tables(n_ctx, seq):
    rows = seq // GRID_W
    row = jnp.repeat(jnp.arange(rows), GRID_W)
    col = jnp.tile(jnp.arange(GRID_W), rows)
    half = MLA_ROPE // 2
    inv_freq = ROPE_THETA ** (-jnp.arange(0, half, 2, dtype=F32) / half)
    ang = jnp.stack([row, col], axis=-1).astype(F32)[..., None] * inv_freq
    ang = jnp.concatenate([ang, ang], axis=-1).reshape(seq, MLA_ROPE)
    cos = jnp.concatenate([jnp.ones((n_ctx, MLA_ROPE), F32), jnp.cos(ang)], axis=0)
    sin = jnp.concatenate([jnp.zeros((n_ctx, MLA_ROPE), F32), jnp.sin(ang)], axis=0)
    return cos, sin


def _mla_prepare(w_in, w_qb, w_kvb, q_qk_g, k_qk_g, kv_norm_g, n_ctx, seq):
    d = w_in.shape[0]
    src, sign = _rot_half_perm()
    cos, sin = _rope_tables(n_ctx, seq)
    t_ = n_ctx + seq

    rope0 = MLA_Q_LORA + MLA_KV_LORA
    w_rope = w_in[:, rope0:rope0 + MLA_ROPE]
    win = jnp.concatenate(
        [w_in, w_rope[:, src] * sign, jnp.zeros((d, 512 - rope0 - 2 * MLA_ROPE), F32)], axis=1)

    wq3 = w_qb.reshape(MLA_Q_LORA, MLA_HEADS, MLA_QK)
    pad = jnp.zeros((MLA_Q_LORA, MLA_HEADS, LANES - MLA_QK), F32)
    wq = jnp.concatenate([wq3, pad], axis=-1).reshape(MLA_Q_LORA, MLA_HEADS * LANES)
    wq_rot = wq3[:, :, MLA_NOPE:][:, :, src] * sign
    wqr = jnp.concatenate([jnp.zeros((MLA_Q_LORA, MLA_HEADS, MLA_NOPE), F32), wq_rot, pad],
                          axis=-1).reshape(MLA_Q_LORA, MLA_HEADS * LANES)

    scale = MLA_QK ** -0.5 * 1.4426950408889634
    gq_n, gq_r = q_qk_g[:MLA_NOPE], q_qk_g[MLA_NOPE:]
    zpad = jnp.zeros((t_, LANES - MLA_QK), F32)
    tqc = jnp.concatenate([jnp.broadcast_to(gq_n, (t_, MLA_NOPE)), gq_r * cos, zpad], axis=1) * scale
    tqs = jnp.concatenate([jnp.zeros((t_, MLA_NOPE), F32), gq_r[src] * sin, zpad], axis=1) * scale

    wkv3 = w_kvb.reshape(MLA_KV_LORA, MLA_HEADS, MLA_NOPE + MLA_V)
    wk_lat = jnp.concatenate(
        [wkv3[:, :, :MLA_NOPE], jnp.zeros((MLA_KV_LORA, MLA_HEADS, LANES - MLA_NOPE), F32)], axis=-1)
    place = jnp.concatenate([jnp.zeros((MLA_ROPE, MLA_NOPE), F32), jnp.eye(MLA_ROPE, dtype=F32),
                             jnp.zeros((MLA_ROPE, LANES - MLA_QK), F32)], axis=1)
    place = jnp.broadcast_to(place[:, None, :], (MLA_ROPE, MLA_HEADS, LANES))
    wk = jnp.concatenate([wk_lat, place, place,
                          jnp.zeros((256 - MLA_KV_LORA - 2 * MLA_ROPE, MLA_HEADS, LANES), F32)],
                         axis=0).reshape(256, MLA_HEADS * LANES)
    gk_n, gk_r = k_qk_g[:MLA_NOPE], k_qk_g[MLA_NOPE:]
    ktab = jnp.concatenate([jnp.broadcast_to(kv_norm_g, (t_, MLA_KV_LORA)), gk_r * cos,
                            gk_r[src] * sin, jnp.zeros((t_, 256 - MLA_KV_LORA - 2 * MLA_ROPE), F32)],
                           axis=1)
    kgain = jnp.concatenate([gk_n, jnp.ones((MLA_ROPE,), F32),
                             jnp.zeros((LANES - MLA_QK,), F32)]).reshape(1, LANES)

    wv_h = wkv3[:, :, MLA_NOPE:]
    zv = jnp.zeros_like(wv_h)
    odd = (jnp.arange(MLA_HEADS) % 2 == 1)[None, :, None]
    wv = jnp.concatenate([jnp.where(odd, zv, wv_h), jnp.where(odd, wv_h, zv)],
                         axis=-1).reshape(MLA_KV_LORA, MLA_HEADS * LANES)
    return (win.astype(BF16), wq.astype(BF16), wqr.astype(BF16), tqc, tqs, ktab,
            wk.astype(BF16), wv.astype(BF16), kgain)


def _router_weights(w_group, w_expert):
    d = w_group.shape[0]
    return jnp.concatenate(
        [w_group, w_expert, jnp.zeros((d, ROUTER_LANES - MOE_GROUPS - MOE_EXPERTS), F32)], axis=1)


def kernel(x, c, ctx, c_ctx, ada_w, ada_b, norm_mix_g, norm_ffn_g, hg_w_in, hg_lower_bounds, hg_out_norm_g, hg_w_out, mla_w_in, mla_q_norm_g, mla_kv_norm_g, mla_w_qb, mla_w_kvb, mla_q_qknorm_g, mla_k_qknorm_g, mla_w_out, moe_w_group, moe_w_expert, moe_w_gate, moe_w_up, moe_w_down):
    b_, seq, d = x.shape
    n_ctx = ctx.shape[1]
    t_ = n_ctx + seq
    tm = min(ROW_TILE, n_ctx)
    tq = min(ATT_Q_TILE, seq)
    assert n_ctx % GLA_BLOCK == 0 and seq % GLA_BLOCK == 0 and n_ctx % tm == 0 and seq % tm == 0
    assert b_ + 1 <= ADA_ROWS and seq % GRID_W == 0

    cc = jnp.concatenate([c, c_ctx[None, :], jnp.zeros((ADA_ROWS - b_ - 1, d), F32)], axis=0)
    mods = _ada_call(cc, ada_w, ada_b).reshape(DEPTH, ADA_CHUNKS, ADA_ROWS, 1, d)

    def mod(i, chunk):
        return mods[i, chunk]

    row = lambda v: v.reshape(1, -1)
    xcat = jnp.concatenate([ctx, x], axis=1)

    lower = jnp.cumsum(jax.nn.softmax(hg_lower_bounds.astype(F32), axis=1), axis=1)[:, 0]
    qs, kf, gf, kb, gb, v, gs = _hg_in_call(
        xcat, mod(0, 0), mod(0, 1), row(norm_mix_g[0]), hg_w_in[0].astype(BF16), lower, n_ctx, tm)
    o_b = _gla_call(qs, kb, gb, v, n_ctx, reverse=True)
    og = _gla_call(qs, kf, gf, v, n_ctx, reverse=False,
                   final_args=(o_b, gs, row(hg_out_norm_g[0])))
    x1, h2, eid, gw, rank, cnt = _outproj_call(
        og, hg_w_out[0].astype(BF16), xcat, 0, mod(0, 2), mod(0, 3), mod(0, 4),
        row(norm_ffn_g[0]), _router_weights(moe_w_group[0], moe_w_expert[0]), n_ctx, tm)
    y0, y1 = _moe(h2, eid, rank, cnt, moe_w_gate[0], moe_w_up[0], moe_w_down[0])
    x2, hm = _combine_call(y0, y1, gw, x1, mod(0, 5), n_ctx, tm,
                           next_mod=(mod(1, 0), mod(1, 1), row(norm_mix_g[1])))

    win, wq, wqr, tqc, tqs, ktab, wk, wv, kgain = _mla_prepare(
        mla_w_in[0], mla_w_qb[0], mla_w_kvb[0], mla_q_qknorm_g[0], mla_k_qknorm_g[0],
        mla_kv_norm_g[0], n_ctx, seq)
    q, k, vv = _mla_in_call(hm, win, row(mla_q_norm_g[0]), wq, wqr, tqc, tqs, ktab, wk, wv, kgain,
                            n_ctx, tm)
    o_att = _attn_call(q, k, vv, tq)
    x3, h2, eid, gw, rank, cnt = _outproj_call(
        o_att, mla_w_out[0].astype(BF16), x2, n_ctx, mod(1, 2), mod(1, 3), mod(1, 4),
        row(norm_ffn_g[1]), _router_weights(moe_w_group[1], moe_w_expert[1]), 0, tm)
    y0, y1 = _moe(h2, eid, rank, cnt, moe_w_gate[1], moe_w_up[1], moe_w_down[1])
    (x4,) = _combine_call(y0, y1, gw, x3, mod(1, 5), 0, tm)
    return x4
```

```python
import functools

import jax
import jax.numpy as jnp
from jax import lax
from jax.experimental import pallas as pl
from jax.experimental.pallas import tpu as pltpu

F32 = jnp.float32
BF16 = jnp.bfloat16

DEPTH = 2
ADA_CHUNKS = 6
NORM_EPS = 1e-6
GRID_W = 64
HG_HEADS = 8
HG_CHUNK = 32
MLA_HEADS = 16
MLA_Q_LORA = 256
MLA_KV_LORA = 128
MLA_NOPE = 64
MLA_ROPE = 32
MLA_V = 64
MLA_QK = MLA_NOPE + MLA_ROPE
ROPE_THETA = 10000.0
MOE_GROUPS = 4
MOE_EPG = 8
MOE_EXPERTS = MOE_GROUPS * MOE_EPG
MOE_TOP_K = 2
MOE_FF = 512
MOE_BLOCK = 256

LANES = 128
ROUTER_ROWS = 48
GATE_ROWS = 8
VMEM_LIMIT = 56 * 1024 * 1024

ROW_TILE = 256
GLA_BLOCK = 256
ATT_Q_TILE = 512
ATT_HEADS_PER_STEP = 4
ATT_KEY_CHUNKS = (256, 128)
ADA_ROWS = 40


def _cparams(sem):
    return pltpu.CompilerParams(dimension_semantics=sem, vmem_limit_bytes=VMEM_LIMIT)


def _resident(shape):
    nd = len(shape)
    return pl.BlockSpec(shape, lambda *_: (0,) * nd, pipeline_mode=pl.Buffered(1))


def _dot(a, b):
    return jnp.dot(a, b, preferred_element_type=F32)


def _dot_nt(a, b):
    return lax.dot_general(a, b, (((1,), (1,)), ((), ())), preferred_element_type=F32)


def _dot_tn(a, b):
    return lax.dot_general(a, b, (((0,), (0,)), ((), ())), preferred_element_type=F32)


def _sigmoid(x):
    return 1.0 / (1.0 + jnp.exp(-x))


def _rms(x):
    return x * lax.rsqrt(jnp.mean(x * x, axis=-1, keepdims=True) + NORM_EPS)


def _ada_kernel(c_ref, w_ref, b_ref, o_ref):
    c = c_ref[...]
    a = (c * _sigmoid(c)).astype(BF16)
    o_ref[...] = _dot(a, w_ref[...].astype(BF16)) + b_ref[...]


def _ada_call(cc, ada_w, ada_b):
    depth, d, _ = ada_w.shape
    rows = cc.shape[0]
    return pl.pallas_call(
        _ada_kernel,
        grid=(depth, ADA_CHUNKS),
        in_specs=[
            pl.BlockSpec((rows, d), lambda i, j: (0, 0)),
            pl.BlockSpec((None, d, d), lambda i, j: (i, 0, j)),
            pl.BlockSpec((None, 1, d), lambda i, j: (i, 0, j)),
        ],
        out_specs=pl.BlockSpec((None, None, rows, d), lambda i, j: (i, j, 0, 0)),
        out_shape=jax.ShapeDtypeStruct((depth, ADA_CHUNKS, rows, d), F32),
        compiler_params=_cparams(("arbitrary", "arbitrary")),
        name="ada_mod",
    )(cc, ada_w, ada_b.reshape(depth, 1, ADA_CHUNKS * d))


def _mod_spec(d, n_ctx_blocks, ctx_row):
    def idx(b, j):
        return (jnp.where(j < n_ctx_blocks, ctx_row, b), 0, 0)
    return pl.BlockSpec((None, 1, d), idx)


def _hg_in_kernel(xc_ref, x_ref, sh_ref, sc_ref, g_ref, w_ref, lb_ref,
                  q_ref, kf_ref, gf_ref, kb_ref, gb_ref, v_ref, gs_ref, *, n_ctx_blocks):
    d = x_ref.shape[-1]
    x = jnp.where(pl.program_id(1) < n_ctx_blocks, xc_ref[...], x_ref[...])
    h = _rms(x) * g_ref[...]
    h = (h * (1.0 + sc_ref[...]) + sh_ref[...]).astype(BF16)

    def proj(c):
        return _dot(h, w_ref[:, c * d:(c + 1) * d])

    p = proj(0)
    q_ref[...] = (p * _sigmoid(p)).astype(BF16)
    for c, k_ref, lg_ref in ((1, kf_ref, gf_ref), (2, kb_ref, gb_ref)):
        s = _sigmoid(proj(c))
        lb = lb_ref[c - 1:c, :]
        k_ref[...] = ((1.0 - lb) * (1.0 - s)).astype(BF16)
        lg_ref[...] = jnp.log(lb + (1.0 - lb) * s).astype(BF16)
    v_ref[...] = proj(3).astype(BF16)
    p = proj(4)
    gs_ref[...] = (p * _sigmoid(p)).astype(BF16)


def _hg_in_call(ctx, x, sh, sc, g, w_in, lb, tm):
    b_, n_ctx, d = ctx.shape
    t_ = n_ctx + x.shape[1]
    ncb = n_ctx // tm
    tok = pl.BlockSpec((None, tm, d), lambda b, j: (b, j, 0))
    out = jax.ShapeDtypeStruct((b_, t_, d), BF16)
    return pl.pallas_call(
        functools.partial(_hg_in_kernel, n_ctx_blocks=ncb),
        grid=(b_, t_ // tm),
        in_specs=[pl.BlockSpec((None, tm, d), lambda b, j: (b, jnp.minimum(j, ncb - 1), 0)),
                  pl.BlockSpec((None, tm, d), lambda b, j: (b, jnp.maximum(j - ncb, 0), 0)),
                  _mod_spec(d, ncb, b_), _mod_spec(d, ncb, b_),
                  _resident((1, d)), _resident(w_in.shape), _resident((2, d))],
        out_specs=[tok] * 7,
        out_shape=[out] * 7,
        compiler_params=_cparams(("parallel", "arbitrary")),
        name="hg_in",
    )(ctx, x, sh, sc, g, w_in, lb)


def _gla_kernel(*refs, reverse, final, n_heads):
    if final:
        q_ref, k_ref, g_ref, v_ref, ob_ref, gs_ref, gn_ref, o_ref, st_ref = refs
    else:
        q_ref, k_ref, g_ref, v_ref, o_ref, st_ref = refs
    tb, d = q_ref.shape
    dh = d // n_heads
    n_chunks = tb // HG_CHUNK

    @pl.when(pl.program_id(1) == 0)
    def _():
        st_ref[...] = jnp.zeros_like(st_ref)

    row = lax.broadcasted_iota(jnp.int32, (tb, tb), 0)
    col = lax.broadcasted_iota(jnp.int32, (tb, tb), 1)
    same_chunk = (row // HG_CHUNK) == (col // HG_CHUNK)
    tri = same_chunk & ((col >= row) if reverse else (col <= row))
    cum = jnp.where(tri, 1.0, 0.0).astype(BF16)

    b = _dot(cum, g_ref[...])
    qd = (q_ref[...].astype(F32) * jnp.exp(b)).astype(BF16)
    ki = k_ref[...].astype(F32) * jnp.exp(-b)
    ki_b = ki.astype(BF16)

    chunk_order = range(n_chunks - 1, -1, -1) if reverse else range(n_chunks)
    e_end, k_end = {}, {}
    for c in range(n_chunks):
        r0 = c * HG_CHUNK
        last = r0 if reverse else r0 + HG_CHUNK - 1
        e_end[c] = jnp.exp(b[last:last + 1, :])
        k_end[c] = (ki[r0:r0 + HG_CHUNK, :] * e_end[c]).astype(BF16)

    heads = range(n_heads)
    hsl = [slice(h * dh, (h + 1) * dh) for h in heads]
    scores = [jnp.where(tri, _dot_nt(qd[:, hsl[h]], ki_b[:, hsl[h]]), 0.0).astype(BF16) for h in heads]
    o_intra = [_dot(scores[h], v_ref[:, hsl[h]]) for h in heads]
    kv = {(h, c): _dot_tn(v_ref[c * HG_CHUNK:(c + 1) * HG_CHUNK, hsl[h]], k_end[c][:, hsl[h]])
          for h in heads for c in range(n_chunks)}
    st = [st_ref[h] for h in heads]
    for c in chunk_order:
        rs = slice(c * HG_CHUNK, (c + 1) * HG_CHUNK)
        for h in heads:
            hs = hsl[h]
            o_c = o_intra[h][rs, :] + _dot_nt(qd[rs, hs], st[h].astype(BF16))
            st[h] = st[h] * e_end[c][:, hs] + kv[(h, c)]
            if final:
                o_c = o_c + ob_ref[rs, hs].astype(F32)
                o_c = _rms(o_c) * gn_ref[...]
                o_c = o_c * gs_ref[rs, hs].astype(F32)
            o_ref[rs, hs] = o_c.astype(o_ref.dtype)
    for h in heads:
        st_ref[h] = st[h]


def _gla_call(q, k, g, v, n_ctx, reverse, final_args=None):
    b_, t_, d = q.shape
    tb = GLA_BLOCK
    nb = t_ // tb
    ncb = n_ctx // tb
    dh = d // HG_HEADS

    def blk(b, j):
        if reverse:
            jj = jnp.where(j < ncb, ncb - 1 - j, nb - 1 - (j - ncb))
        else:
            jj = j
        return (b, jj, 0)

    tok = pl.BlockSpec((None, tb, d), blk)
    in_specs = [tok] * 4
    args = [q, k, g, v]
    final = final_args is not None
    if final:
        o_b, gs, gn = final_args
        in_specs += [tok, tok, _resident((1, dh))]
        args += [o_b, gs, gn]
    return pl.pallas_call(
        functools.partial(_gla_kernel, reverse=reverse, final=final, n_heads=HG_HEADS),
        grid=(b_, nb),
        in_specs=in_specs,
        out_specs=tok,
        out_shape=jax.ShapeDtypeStruct((b_, t_, d), BF16),
        scratch_shapes=[pltpu.VMEM((HG_HEADS, dh, dh), F32)],
        compiler_params=_cparams(("parallel", "arbitrary")),
        name="gla_bwd" if reverse else "gla_fwd",
    )(*args)


def _outproj_kernel(*refs, n_ctx_blocks, split_residual):
    if split_residual:
        (a_ref, w_ref, xc_ref, x_ref, gt_ref, sh_ref, sc_ref, g_ref, wr_ref,
         xn_ref, h2_ref, eid_ref, gw_ref, rank_ref, cnt_ref, run_ref) = refs
    else:
        (a_ref, w_ref, x_ref, gt_ref, sh_ref, sc_ref, g_ref, wr_ref,
         xn_ref, h2_ref, eid_ref, gw_ref, rank_ref, cnt_ref, run_ref) = refs
    tm = a_ref.shape[0]
    nr = ROUTER_ROWS
    first = (pl.program_id(0) == 0) & (pl.program_id(1) == 0)

    @pl.when(first)
    def _():
        run_ref[...] = jnp.zeros_like(run_ref)

    x = x_ref[...]
    if split_residual:
        x = jnp.where(pl.program_id(1) < n_ctx_blocks, xc_ref[...], x)
    y = _dot(a_ref[...], w_ref[...])
    xn = x + gt_ref[...] * y
    xn_ref[...] = xn
    h2 = _rms(xn) * g_ref[...]
    h2 = h2 * (1.0 + sc_ref[...]) + sh_ref[...]
    h2b = h2.astype(BF16)
    h2_ref[...] = h2b
    h2r = (h2 - h2b.astype(F32)).astype(BF16)
    lg2 = _dot_nt(wr_ref[...], h2b) + _dot_nt(wr_ref[...], h2r)
    lg = lg2[:nr] + lg2[nr:]

    ridx = lax.broadcasted_iota(jnp.int32, lg.shape, 0)
    neg = -jnp.inf
    gl = jnp.where(ridx < MOE_GROUPS, lg, neg)
    gmax = jnp.max(gl, axis=0, keepdims=True)
    g_idx = jnp.min(jnp.where(gl == gmax, ridx, nr), axis=0, keepdims=True)
    p_group = 1.0 / jnp.sum(jnp.exp(gl - gmax), axis=0, keepdims=True)
    lo = MOE_GROUPS + MOE_EPG * g_idx
    el = jnp.where((ridx >= lo) & (ridx < lo + MOE_EPG), lg, neg)
    m1 = jnp.max(el, axis=0, keepdims=True)
    i1 = jnp.min(jnp.where(el == m1, ridx, nr), axis=0, keepdims=True)
    el2 = jnp.where(ridx == i1, neg, el)
    m2 = jnp.max(el2, axis=0, keepdims=True)
    i2 = jnp.min(jnp.where(el2 == m2, ridx, nr), axis=0, keepdims=True)
    r21 = jnp.exp(m2 - m1)
    w1 = p_group / (1.0 + r21)
    w2 = w1 * r21

    hot1 = ridx == i1
    hot2 = ridx == i2
    f1 = jnp.where(hot1, 1.0, 0.0)
    f2 = jnp.where(hot2, 1.0, 0.0)
    r = lax.broadcasted_iota(jnp.int32, (tm, tm), 0)
    c = lax.broadcasted_iota(jnp.int32, (tm, tm), 1)
    earlier = jnp.where(r < c, 1.0, 0.0).astype(BF16)
    pre = _dot(jnp.concatenate([f1, f2], axis=0).astype(BF16), earlier)
    run = run_ref[:, 0:1]
    n1 = jnp.sum(f1, axis=1, keepdims=True)
    n2 = jnp.sum(f2, axis=1, keepdims=True)
    rank1 = jnp.sum(jnp.where(hot1, pre[:nr] + run, 0.0), axis=0, keepdims=True)
    rank2 = jnp.sum(jnp.where(hot2, pre[nr:] + (run + n1), 0.0), axis=0, keepdims=True)
    run = jnp.broadcast_to(run + n1 + n2, run_ref.shape)
    run_ref[...] = run
    cnt_ref[...] = run

    eid_ref[...] = jnp.concatenate([i1, i2], axis=0) - MOE_GROUPS
    rank_ref[...] = jnp.concatenate([rank1, rank2], axis=0).astype(jnp.int32)
    gw_ref[...] = jnp.concatenate([w1, w2, jnp.zeros((GATE_ROWS - MOE_TOP_K, tm), F32)], axis=0)


def _outproj_call(a, w, res_lat, lat_off, res_ctx, gt, sh, sc, g, wr, n_ctx, tm):
    b_, tn, d = a.shape
    ncb = n_ctx // tm
    nt = tn // tm
    lob = lat_off // tm
    split = res_ctx is not None
    tok = pl.BlockSpec((None, tm, d), lambda b, j: (b, j, 0))
    pair = pl.BlockSpec((None, None, MOE_TOP_K, tm), lambda b, j: (b, j, 0, 0))
    res_specs, res_args = [], []
    if split:
        res_specs.append(pl.BlockSpec((None, tm, d), lambda b, j: (b, jnp.minimum(j, ncb - 1), 0)))
        res_args.append(res_ctx)
        res_specs.append(pl.BlockSpec((None, tm, d), lambda b, j: (b, jnp.maximum(j - ncb, 0) + lob, 0)))
    else:
        res_specs.append(pl.BlockSpec((None, tm, d), lambda b, j: (b, j + lob, 0)))
    res_args.append(res_lat)
    return pl.pallas_call(
        functools.partial(_outproj_kernel, n_ctx_blocks=ncb, split_residual=split),
        grid=(b_, nt),
        in_specs=[tok, _resident(w.shape)] + res_specs + [
            _mod_spec(d, ncb, b_), _mod_spec(d, ncb, b_), _mod_spec(d, ncb, b_),
            _resident((1, d)), _resident(wr.shape)],
        out_specs=[tok, tok, pair,
                   pl.BlockSpec((None, None, GATE_ROWS, tm), lambda b, j: (b, j, 0, 0)), pair,
                   pl.BlockSpec((ROUTER_ROWS, LANES), lambda b, j: (0, 0))],
        out_shape=[jax.ShapeDtypeStruct((b_, tn, d), F32),
                   jax.ShapeDtypeStruct((b_, tn, d), BF16),
                   jax.ShapeDtypeStruct((b_, nt, MOE_TOP_K, tm), jnp.int32),
                   jax.ShapeDtypeStruct((b_, nt, GATE_ROWS, tm), F32),
                   jax.ShapeDtypeStruct((b_, nt, MOE_TOP_K, tm), jnp.int32),
                   jax.ShapeDtypeStruct((ROUTER_ROWS, LANES), F32)],
        scratch_shapes=[pltpu.VMEM((ROUTER_ROWS, LANES), F32)],
        compiler_params=_cparams(("arbitrary", "arbitrary")),
        name="outproj_ffnmod",
    )(a, w, *res_args, gt, sh, sc, g, wr)


def _expert_kernel(be_ref, nu_ref, x_ref, wg_ref, wu_ref, wd_ref, y_ref, wgb_ref, wub_ref, wdb_ref):
    i = pl.program_id(0)
    new_expert = (i == 0) | (be_ref[i] != be_ref[jnp.maximum(i - 1, 0)])

    @pl.when(new_expert)
    def _():
        wgb_ref[...] = wg_ref[...].astype(BF16)
        wub_ref[...] = wu_ref[...].astype(BF16)
        wdb_ref[...] = wd_ref[...].astype(BF16)

    @pl.when(i < nu_ref[0])
    def _():
        x = x_ref[...]
        gate = _dot(x, wgb_ref[...])
        up = _dot(x, wub_ref[...])
        act = (gate * _sigmoid(gate) * up).astype(BF16)
        y_ref[...] = _dot(act, wdb_ref[...]).astype(y_ref.dtype)

    @pl.when(i >= nu_ref[0])
    def _():
        y_ref[...] = jnp.zeros_like(y_ref)


def _expert_call(buf, block_expert, n_used, wg, wu, wd, layer):
    r_pad, d = buf.shape
    n_blocks = r_pad // MOE_BLOCK
    ff = wg.shape[-1]

    def x_idx(i, be, nu):
        return (jnp.maximum(jnp.minimum(i, nu[0] - 1), 0), 0)

    grid_spec = pltpu.PrefetchScalarGridSpec(
        num_scalar_prefetch=2,
        grid=(n_blocks,),
        in_specs=[
            pl.BlockSpec((MOE_BLOCK, d), x_idx),
            pl.BlockSpec((None, None, d, ff), lambda i, be, nu: (layer, be[i], 0, 0)),
            pl.BlockSpec((None, None, d, ff), lambda i, be, nu: (layer, be[i], 0, 0)),
            pl.BlockSpec((None, None, ff, d), lambda i, be, nu: (layer, be[i], 0, 0)),
        ],
        out_specs=pl.BlockSpec((MOE_BLOCK, d), lambda i, be, nu: (i, 0)),
        scratch_shapes=[pltpu.VMEM((d, ff), BF16), pltpu.VMEM((d, ff), BF16),
                        pltpu.VMEM((ff, d), BF16)],
    )
    return pl.pallas_call(
        _expert_kernel,
        grid_spec=grid_spec,
        out_shape=jax.ShapeDtypeStruct((r_pad, d), BF16),
        compiler_params=_cparams(("arbitrary",)),
        name="moe_experts",
    )(block_expert, n_used, buf, wg, wu, wd)


def _combine_kernel(*refs, with_next):
    if with_next:
        y0_ref, y1_ref, gw_ref, x_ref, gt_ref, sh_ref, sc_ref, g_ref, xo_ref, hn_ref = refs
    else:
        y0_ref, y1_ref, gw_ref, x_ref, gt_ref, xo_ref = refs
    gw = gw_ref[...].T
    f = gw[:, 0:1] * y0_ref[...].astype(F32) + gw[:, 1:2] * y1_ref[...].astype(F32)
    xo = x_ref[...] + gt_ref[...] * f
    xo_ref[...] = xo
    if with_next:
        h = _rms(xo) * g_ref[...]
        hn_ref[...] = (h * (1.0 + sc_ref[...]) + sh_ref[...]).astype(BF16)


def _combine_call(y0, y1, gw, x, gt, n_ctx, tm, next_mod=None):
    b_, tn, d = x.shape
    ncb = n_ctx // tm
    tok = pl.BlockSpec((None, tm, d), lambda b, j: (b, j, 0))
    in_specs = [tok, tok, pl.BlockSpec((None, None, GATE_ROWS, tm), lambda b, j: (b, j, 0, 0)),
                tok, _mod_spec(d, ncb, b_)]
    args = [y0, y1, gw, x, gt]
    out_specs = [tok]
    out_shape = [jax.ShapeDtypeStruct((b_, tn, d), F32)]
    with_next = next_mod is not None
    if with_next:
        sh, sc, g = next_mod
        in_specs += [_mod_spec(d, ncb, b_), _mod_spec(d, ncb, b_), _resident((1, d))]
        args += [sh, sc, g]
        out_specs.append(tok)
        out_shape.append(jax.ShapeDtypeStruct((b_, tn, d), BF16))
    return pl.pallas_call(
        functools.partial(_combine_kernel, with_next=with_next),
        grid=(b_, tn // tm),
        in_specs=in_specs,
        out_specs=out_specs,
        out_shape=out_shape,
        compiler_params=_cparams(("parallel", "arbitrary")),
        name="moe_combine",
    )(*args)


def _mla_in_kernel(h_ref, win_ref, qg_ref, wq_ref, wqr_ref, tqc_ref, tqs_ref,
                   ktab_ref, wk_ref, wv_ref, kgain_ref, q_ref, k_ref, v_ref, *, n_ctx_blocks):
    tm = h_ref.shape[0]
    proj = _dot(h_ref[...], win_ref[...])

    @pl.when(pl.program_id(1) >= n_ctx_blocks)
    def _():
        cq = proj[:, :MLA_Q_LORA]
        qn = (_rms(cq) * qg_ref[...]).astype(BF16)
        qa = _dot(qn, wq_ref[...])
        qr = _dot(qn, wqr_ref[...])
        tqc = tqc_ref[...]
        tqs = tqs_ref[...]
        for h in range(MLA_HEADS):
            hs = slice(h * LANES, (h + 1) * LANES)
            a = qa[:, hs]
            rq = lax.rsqrt(jnp.sum(a * a, axis=-1, keepdims=True) * (1.0 / MLA_QK) + NORM_EPS)
            q_ref[h] = ((a * tqc + qr[:, hs] * tqs) * rq).astype(BF16)

    p2 = proj[:, MLA_Q_LORA:]
    lane = lax.broadcasted_iota(jnp.int32, (tm, p2.shape[1]), 1)
    is_kv = lane < MLA_KV_LORA
    sq = p2 * p2
    ms_kv = jnp.sum(jnp.where(is_kv, sq, 0.0), axis=-1, keepdims=True) * (1.0 / MLA_KV_LORA)
    is_rope = (lane >= MLA_KV_LORA) & (lane < MLA_KV_LORA + MLA_ROPE)
    ss_rope = jnp.sum(jnp.where(is_rope, sq, 0.0), axis=-1, keepdims=True)
    mult = ktab_ref[...] * jnp.where(is_kv, lax.rsqrt(ms_kv + NORM_EPS), 1.0)
    lhs = (p2 * mult).astype(BF16)
    ka = _dot(lhs, wk_ref[...])
    va = _dot(lhs[:, :MLA_KV_LORA], wv_ref[...])
    kgain = kgain_ref[...]
    lane1 = lax.broadcasted_iota(jnp.int32, (tm, LANES), 1)
    for h in range(MLA_HEADS):
        hs = slice(h * LANES, (h + 1) * LANES)
        a = ka[:, hs]
        ssn = jnp.sum(jnp.where(lane1 < MLA_NOPE, a * a, 0.0), axis=-1, keepdims=True)
        rk = lax.rsqrt((ssn + ss_rope) * (1.0 / MLA_QK) + NORM_EPS)
        k_ref[h] = (a * kgain * rk).astype(BF16)
        ones_lane = MLA_V if h % 2 == 0 else 0
        v_ref[h] = jnp.where(lane1 == ones_lane, 1.0, va[:, hs]).astype(BF16)


def _mla_in_call(hm, win, qg, wq, wqr, tqc, tqs, ktab, wk, wv, kgain, n_ctx, tm):
    b_, t_, d = hm.shape
    ncb = n_ctx // tm
    tok = pl.BlockSpec((None, tm, d), lambda b, j: (b, j, 0))
    head = pl.BlockSpec((None, MLA_HEADS, tm, LANES), lambda b, j: (b, 0, j, 0))
    hshape = jax.ShapeDtypeStruct((b_, MLA_HEADS, t_, LANES), BF16)
    qhead = pl.BlockSpec((None, MLA_HEADS, tm, LANES), lambda b, j: (b, 0, jnp.maximum(j - ncb, 0), 0))
    qshape = jax.ShapeDtypeStruct((b_, MLA_HEADS, t_ - n_ctx, LANES), BF16)
    return pl.pallas_call(
        functools.partial(_mla_in_kernel, n_ctx_blocks=ncb),
        grid=(b_, t_ // tm),
        in_specs=[tok, _resident(win.shape), _resident(qg.shape), _resident(wq.shape),
                  _resident(wqr.shape),
                  pl.BlockSpec((tm, LANES), lambda b, j: (j, 0)),
                  pl.BlockSpec((tm, LANES), lambda b, j: (j, 0)),
                  pl.BlockSpec((tm, ktab.shape[1]), lambda b, j: (j, 0)),
                  _resident(wk.shape), _resident(wv.shape), _resident(kgain.shape)],
        out_specs=[qhead, head, head],
        out_shape=[qshape, hshape, hshape],
        compiler_params=_cparams(("parallel", "arbitrary")),
        name="mla_in",
    )(hm, win, qg, wq, wqr, tqc, tqs, ktab, wk, wv, kgain)


def _attn_kernel(q_ref, k_ref, v_ref, o_ref):
    n_heads, tq, _ = q_ref.shape
    t_ = k_ref.shape[1]
    ck = next(c for c in ATT_KEY_CHUNKS if t_ % c == 0)
    outs = []
    for hh in range(n_heads):
        q = q_ref[hh]
        m = jnp.full((tq, 1), -jnp.inf, F32)
        acc = jnp.zeros((tq, LANES), F32)
        for c in range(t_ // ck):
            ks = slice(c * ck, (c + 1) * ck)
            s = _dot_nt(q, k_ref[hh, ks, :])
            m_new = jnp.maximum(m, jnp.max(s, axis=-1, keepdims=True))
            p = jnp.exp2((s - m_new).astype(BF16))
            acc = acc * jnp.exp2(m - m_new) + _dot(p, v_ref[hh, ks, :])
            m = m_new
        ones_lane = MLA_V if hh % 2 == 0 else 0
        outs.append(acc / acc[:, ones_lane:ones_lane + 1])
    lane = lax.broadcasted_iota(jnp.int32, outs[0].shape, 1)
    for pr in range(len(outs) // 2):
        o_ref[:, pr * LANES:(pr + 1) * LANES] = jnp.where(
            lane < MLA_V, outs[2 * pr], outs[2 * pr + 1]).astype(o_ref.dtype)


def _attn_call(q, k, v, tq):
    b_, nh, l_, _ = q.shape
    t_ = k.shape[2]
    hps = ATT_HEADS_PER_STEP
    return pl.pallas_call(
        _attn_kernel,
        grid=(b_, nh // hps, l_ // tq),
        in_specs=[
            pl.BlockSpec((None, hps, tq, LANES), lambda b, h, i: (b, h, i, 0)),
            pl.BlockSpec((None, hps, t_, LANES), lambda b, h, i: (b, h, 0, 0)),
            pl.BlockSpec((None, hps, t_, LANES), lambda b, h, i: (b, h, 0, 0)),
        ],
        out_specs=pl.BlockSpec((None, tq, (hps // 2) * LANES), lambda b, h, i: (b, i, h)),
        out_shape=jax.ShapeDtypeStruct((b_, l_, (nh // 2) * LANES), BF16),
        compiler_params=_cparams(("parallel", "parallel", "arbitrary")),
        name="mla_attention",
    )(q, k, v)


def _dispatch_layout(eid, rank, counts):
    tm = eid.shape[-1]
    n_slot = eid.size
    starts = jnp.cumsum(counts) - counts
    padded = (counts + MOE_BLOCK - 1) // MOE_BLOCK * MOE_BLOCK
    pad_ends = jnp.cumsum(padded)
    pad_starts = pad_ends - padded
    n_blocks = (n_slot + MOE_BLOCK - 1) // MOE_BLOCK + MOE_EXPERTS
    block_start = jnp.arange(n_blocks, dtype=jnp.int32) * MOE_BLOCK
    block_expert = jnp.minimum(
        jnp.sum((block_start[:, None] >= pad_ends[None, :]).astype(jnp.int32), axis=1),
        MOE_EXPERTS - 1)
    n_used = pad_ends[-1:] // MOE_BLOCK
    slot_row = jnp.take(pad_starts, eid, mode='clip') + rank
    order = jnp.argsort(eid.reshape(n_slot)).astype(jnp.int32)
    offs = (block_start - pad_starts[block_expert])[:, None] + jnp.arange(MOE_BLOCK, dtype=jnp.int32)
    valid = offs < counts[block_expert][:, None]
    pos = jnp.minimum(starts[block_expert][:, None] + offs, n_slot - 1)
    slot = jnp.take(order, pos, mode='clip')
    tok = (slot // (MOE_TOP_K * tm)) * tm + slot % tm
    row_tok = jnp.where(valid, tok, 0)
    return row_tok.reshape(-1), slot_row, block_expert, n_used


def _moe(h2, eid, rank, counts, wg, wu, wd, layer):
    b_, tn, d = h2.shape
    n_tok = b_ * tn
    cnt = counts[MOE_GROUPS:MOE_GROUPS + MOE_EXPERTS, 0].astype(jnp.int32)
    row_tok, slot_row, block_expert, n_used = _dispatch_layout(eid, rank, cnt)
    buf = jnp.take(h2.reshape(n_tok, d), row_tok, axis=0, mode='clip')
    ybuf = _expert_call(buf, block_expert, n_used, wg, wu, wd, layer)
    y0 = jnp.take(ybuf, slot_row[:, :, 0, :].reshape(n_tok), axis=0, mode='clip').reshape(b_, tn, d)
    y1 = jnp.take(ybuf, slot_row[:, :, 1, :].reshape(n_tok), axis=0, mode='clip').reshape(b_, tn, d)
    return y0, y1


def _rot_half_perm():
    half = MLA_ROPE // 2
    j = jnp.arange(MLA_ROPE)
    within = j % half
    base = j - within
    src = jnp.where(within < half // 2, base + within + half // 2, base + within - half // 2)
    sign = jnp.where(within < half // 2, -1.0, 1.0).astype(F32)
    return src, sign


def _rope_tables(n_ctx, seq):
    rows = seq // GRID_W
    row = jnp.repeat(jnp.arange(rows), GRID_W)
    col = jnp.tile(jnp.arange(GRID_W), rows)
    half = MLA_ROPE // 2
    inv_freq = ROPE_THETA ** (-jnp.arange(0, half, 2, dtype=F32) / half)
    ang = jnp.stack([row, col], axis=-1).astype(F32)[..., None] * inv_freq
    ang = jnp.concatenate([ang, ang], axis=-1).reshape(seq, MLA_ROPE)
    cos = jnp.concatenate([jnp.ones((n_ctx, MLA_ROPE), F32), jnp.cos(ang)], axis=0)
    sin = jnp.concatenate([jnp.zeros((n_ctx, MLA_ROPE), F32), jnp.sin(ang)], axis=0)
    return cos, sin


def _mla_prepare(w_in, w_qb, w_kvb, q_qk_g, k_qk_g, kv_norm_g, n_ctx, seq):
    d = w_in.shape[0]
    src, sign = _rot_half_perm()
    cos, sin = _rope_tables(n_ctx, seq)
    t_ = n_ctx + seq

    rope0 = MLA_Q_LORA + MLA_KV_LORA
    w_rope = w_in[:, rope0:rope0 + MLA_ROPE]
    win = jnp.concatenate(
        [w_in, w_rope[:, src] * sign, jnp.zeros((d, 512 - rope0 - 2 * MLA_ROPE), F32)], axis=1)

    wq3 = w_qb.reshape(MLA_Q_LORA, MLA_HEADS, MLA_QK)
    pad = jnp.zeros((MLA_Q_LORA, MLA_HEADS, LANES - MLA_QK), F32)
    wq = jnp.concatenate([wq3, pad], axis=-1).reshape(MLA_Q_LORA, MLA_HEADS * LANES)
    wq_rot = wq3[:, :, MLA_NOPE:][:, :, src] * sign
    wqr = jnp.concatenate([jnp.zeros((MLA_Q_LORA, MLA_HEADS, MLA_NOPE), F32), wq_rot, pad],
                          axis=-1).reshape(MLA_Q_LORA, MLA_HEADS * LANES)

    scale = MLA_QK ** -0.5 * 1.4426950408889634
    gq_n, gq_r = q_qk_g[:MLA_NOPE], q_qk_g[MLA_NOPE:]
    zpad = jnp.zeros((t_, LANES - MLA_QK), F32)
    tqc = jnp.concatenate([jnp.broadcast_to(gq_n, (t_, MLA_NOPE)), gq_r * cos, zpad], axis=1) * scale
    tqs = jnp.concatenate([jnp.zeros((t_, MLA_NOPE), F32), gq_r[src] * sin, zpad], axis=1) * scale

    wkv3 = w_kvb.reshape(MLA_KV_LORA, MLA_HEADS, MLA_NOPE + MLA_V)
    wk_lat = jnp.concatenate(
        [wkv3[:, :, :MLA_NOPE], jnp.zeros((MLA_KV_LORA, MLA_HEADS, LANES - MLA_NOPE), F32)], axis=-1)
    place = jnp.concatenate([jnp.zeros((MLA_ROPE, MLA_NOPE), F32), jnp.eye(MLA_ROPE, dtype=F32),
                             jnp.zeros((MLA_ROPE, LANES - MLA_QK), F32)], axis=1)
    place = jnp.broadcast_to(place[:, None, :], (MLA_ROPE, MLA_HEADS, LANES))
    wk = jnp.concatenate([wk_lat, place, place,
                          jnp.zeros((256 - MLA_KV_LORA - 2 * MLA_ROPE, MLA_HEADS, LANES), F32)],
                         axis=0).reshape(256, MLA_HEADS * LANES)
    gk_n, gk_r = k_qk_g[:MLA_NOPE], k_qk_g[MLA_NOPE:]
    ktab = jnp.concatenate([jnp.broadcast_to(kv_norm_g, (t_, MLA_KV_LORA)), gk_r * cos,
                            gk_r[src] * sin, jnp.zeros((t_, 256 - MLA_KV_LORA - 2 * MLA_ROPE), F32)],
                           axis=1)
    kgain = jnp.concatenate([gk_n, jnp.ones((MLA_ROPE,), F32),
                             jnp.zeros((LANES - MLA_QK,), F32)]).reshape(1, LANES)

    wv_h = wkv3[:, :, MLA_NOPE:]
    zv = jnp.zeros_like(wv_h)
    odd = (jnp.arange(MLA_HEADS) % 2 == 1)[None, :, None]
    wv = jnp.concatenate([jnp.where(odd, zv, wv_h), jnp.where(odd, wv_h, zv)],
                         axis=-1).reshape(MLA_KV_LORA, MLA_HEADS * LANES)
    return (win.astype(BF16), wq.astype(BF16), wqr.astype(BF16), tqc, tqs, ktab,
            wk.astype(BF16), wv.astype(BF16), kgain)


def _router_weights(w_group, w_expert):
    d = w_group.shape[0]
    wt = jnp.concatenate([w_group, w_expert], axis=1).T
    wt = jnp.concatenate([wt, jnp.zeros((ROUTER_ROWS - wt.shape[0], d), F32)], axis=0)
    head = wt.astype(BF16)
    rest = (wt - head.astype(F32)).astype(BF16)
    return jnp.concatenate([head, rest], axis=0)


def kernel(x, c, ctx, c_ctx, ada_w, ada_b, norm_mix_g, norm_ffn_g, hg_w_in, hg_lower_bounds, hg_out_norm_g, hg_w_out, mla_w_in, mla_q_norm_g, mla_kv_norm_g, mla_w_qb, mla_w_kvb, mla_q_qknorm_g, mla_k_qknorm_g, mla_w_out, moe_w_group, moe_w_expert, moe_w_gate, moe_w_up, moe_w_down):
    b_, seq, d = x.shape
    n_ctx = ctx.shape[1]
    tm = min(ROW_TILE, n_ctx)
    tq = min(ATT_Q_TILE, seq)
    assert n_ctx % GLA_BLOCK == 0 and seq % GLA_BLOCK == 0 and n_ctx % tm == 0 and seq % tm == 0
    assert b_ + 1 <= ADA_ROWS and seq % GRID_W == 0

    cc = jnp.concatenate([c, c_ctx[None, :], jnp.zeros((ADA_ROWS - b_ - 1, d), F32)], axis=0)
    mods = _ada_call(cc, ada_w, ada_b).reshape(DEPTH, ADA_CHUNKS, ADA_ROWS, 1, d)

    def mod(i, chunk):
        return mods[i, chunk]

    row = lambda v: v.reshape(1, -1)

    lower = jnp.cumsum(jax.nn.softmax(hg_lower_bounds.astype(F32), axis=1), axis=1)[:, 0]
    qs, kf, gf, kb, gb, v, gs = _hg_in_call(
        ctx, x, mod(0, 0), mod(0, 1), row(norm_mix_g[0]), hg_w_in[0].astype(BF16), lower, tm)
    o_b = _gla_call(qs, kb, gb, v, n_ctx, reverse=True)
    og = _gla_call(qs, kf, gf, v, n_ctx, reverse=False,
                   final_args=(o_b, gs, row(hg_out_norm_g[0])))
    x1, h2, eid, gw, rank, cnt = _outproj_call(
        og, hg_w_out[0].astype(BF16), x, 0, ctx, mod(0, 2), mod(0, 3), mod(0, 4),
        row(norm_ffn_g[0]), _router_weights(moe_w_group[0], moe_w_expert[0]), n_ctx, tm)
    y0, y1 = _moe(h2, eid, rank, cnt, moe_w_gate, moe_w_up, moe_w_down, 0)
    x2, hm = _combine_call(y0, y1, gw, x1, mod(0, 5), n_ctx, tm,
                           next_mod=(mod(1, 0), mod(1, 1), row(norm_mix_g[1])))

    win, wq, wqr, tqc, tqs, ktab, wk, wv, kgain = _mla_prepare(
        mla_w_in[0], mla_w_qb[0], mla_w_kvb[0], mla_q_qknorm_g[0], mla_k_qknorm_g[0],
        mla_kv_norm_g[0], n_ctx, seq)
    q, k, vv = _mla_in_call(hm, win, row(mla_q_norm_g[0]), wq, wqr, tqc, tqs, ktab, wk, wv, kgain,
                            n_ctx, tm)
    o_att = _attn_call(q, k, vv, tq)
    x3, h2, eid, gw, rank, cnt = _outproj_call(
        o_att, mla_w_out[0].astype(BF16), x2, n_ctx, None, mod(1, 2), mod(1, 3), mod(1, 4),
        row(norm_ffn_g[1]), _router_weights(moe_w_group[1], moe_w_expert[1]), 0, tm)
    y0, y1 = _moe(h2, eid, rank, cnt, moe_w_gate, moe_w_up, moe_w_down, 1)
    (x4,) = _combine_call(y0, y1, gw, x3, mod(1, 5), 0, tm)
    return x4
```

```python
import functools

import jax
import jax.numpy as jnp
from jax import lax
from jax.experimental import pallas as pl
from jax.experimental.pallas import tpu as pltpu

F32 = jnp.float32
BF16 = jnp.bfloat16

DEPTH = 2
ADA_CHUNKS = 6
NORM_EPS = 1e-6
GRID_W = 64
HG_HEADS = 8
HG_CHUNK = 32
MLA_HEADS = 16
MLA_Q_LORA = 256
MLA_KV_LORA = 128
MLA_NOPE = 64
MLA_ROPE = 32
MLA_V = 64
MLA_QK = MLA_NOPE + MLA_ROPE
ROPE_THETA = 10000.0
MOE_GROUPS = 4
MOE_EPG = 8
MOE_EXPERTS = MOE_GROUPS * MOE_EPG
MOE_TOP_K = 2
MOE_FF = 512

LANES = 128
ROUTER_ROWS = 48
GATE_ROWS = 8
VMEM_LIMIT = 56 * 1024 * 1024

ROW_TILE = 256
GLA_BLOCK = 256
MOE_BLOCK = 512
ATT_Q_TILE = 512
ATT_HEADS_PER_STEP = 8
ATT_KEY_CHUNKS = (256, 128)
ADA_ROWS = 40


def _cparams(sem):
    return pltpu.CompilerParams(dimension_semantics=sem, vmem_limit_bytes=VMEM_LIMIT)


def _resident(shape):
    nd = len(shape)
    return pl.BlockSpec(shape, lambda *_: (0,) * nd, pipeline_mode=pl.Buffered(1))


def _dot(a, b):
    return jnp.dot(a, b, preferred_element_type=F32)


def _dot_nt(a, b):
    return lax.dot_general(a, b, (((1,), (1,)), ((), ())), preferred_element_type=F32)


def _dot_tn(a, b):
    return lax.dot_general(a, b, (((0,), (0,)), ((), ())), preferred_element_type=F32)


def _sigmoid(x):
    return 1.0 / (1.0 + jnp.exp(-x))


def _rms(x):
    return x * lax.rsqrt(jnp.mean(x * x, axis=-1, keepdims=True) + NORM_EPS)


def _ada_kernel(c_ref, w_ref, b_ref, o_ref):
    c = c_ref[...]
    a = (c * _sigmoid(c)).astype(BF16)
    o_ref[...] = _dot(a, w_ref[...].astype(BF16)) + b_ref[...]


def _ada_call(cc, ada_w, ada_b):
    depth, d, _ = ada_w.shape
    rows = cc.shape[0]
    return pl.pallas_call(
        _ada_kernel,
        grid=(depth, ADA_CHUNKS),
        in_specs=[
            pl.BlockSpec((rows, d), lambda i, j: (0, 0)),
            pl.BlockSpec((None, d, d), lambda i, j: (i, 0, j)),
            pl.BlockSpec((None, 1, d), lambda i, j: (i, 0, j)),
        ],
        out_specs=pl.BlockSpec((None, None, rows, d), lambda i, j: (i, j, 0, 0)),
        out_shape=jax.ShapeDtypeStruct((depth, ADA_CHUNKS, rows, d), F32),
        compiler_params=_cparams(("arbitrary", "arbitrary")),
        name="ada_mod",
    )(cc, ada_w, ada_b.reshape(depth, 1, ADA_CHUNKS * d))


def _mod_spec(d, n_ctx_blocks, ctx_row):
    def idx(b, j):
        return (jnp.where(j < n_ctx_blocks, ctx_row, b), 0, 0)
    return pl.BlockSpec((None, 1, d), idx)


def _hg_in_kernel(xc_ref, x_ref, sh_ref, sc_ref, g_ref, w_ref, lb_ref,
                  q_ref, kf_ref, gf_ref, kb_ref, gb_ref, v_ref, gs_ref, *, n_ctx_blocks):
    d = x_ref.shape[-1]
    x = jnp.where(pl.program_id(1) < n_ctx_blocks, xc_ref[...], x_ref[...])
    h = _rms(x) * g_ref[...]
    h = (h * (1.0 + sc_ref[...]) + sh_ref[...]).astype(BF16)

    def proj(c):
        return _dot(h, w_ref[:, c * d:(c + 1) * d])

    p = proj(0)
    q_ref[...] = (p * _sigmoid(p)).astype(BF16)
    for c, k_ref, lg_ref in ((1, kf_ref, gf_ref), (2, kb_ref, gb_ref)):
        s = _sigmoid(proj(c))
        lb = lb_ref[c - 1:c, :]
        k_ref[...] = ((1.0 - lb) * (1.0 - s)).astype(BF16)
        lg_ref[...] = jnp.log(lb + (1.0 - lb) * s).astype(BF16)
    v_ref[...] = proj(3).astype(BF16)
    p = proj(4)
    gs_ref[...] = (p * _sigmoid(p)).astype(BF16)


def _hg_in_call(ctx, x, sh, sc, g, w_in, lb, tm):
    b_, n_ctx, d = ctx.shape
    t_ = n_ctx + x.shape[1]
    ncb = n_ctx // tm
    tok = pl.BlockSpec((None, tm, d), lambda b, j: (b, j, 0))
    out = jax.ShapeDtypeStruct((b_, t_, d), BF16)
    return pl.pallas_call(
        functools.partial(_hg_in_kernel, n_ctx_blocks=ncb),
        grid=(b_, t_ // tm),
        in_specs=[pl.BlockSpec((None, tm, d), lambda b, j: (b, jnp.minimum(j, ncb - 1), 0)),
                  pl.BlockSpec((None, tm, d), lambda b, j: (b, jnp.maximum(j - ncb, 0), 0)),
                  _mod_spec(d, ncb, b_), _mod_spec(d, ncb, b_),
                  _resident((1, d)), _resident(w_in.shape), _resident((2, d))],
        out_specs=[tok] * 7,
        out_shape=[out] * 7,
        compiler_params=_cparams(("parallel", "arbitrary")),
        name="hg_in",
    )(ctx, x, sh, sc, g, w_in, lb)


def _gla_kernel(*refs, reverse, final, n_heads):
    if final:
        q_ref, k_ref, g_ref, v_ref, ob_ref, gs_ref, gn_ref, o_ref, st_ref = refs
    else:
        q_ref, k_ref, g_ref, v_ref, o_ref, st_ref = refs
    tb, d = q_ref.shape
    dh = d // n_heads
    n_chunks = tb // HG_CHUNK

    @pl.when(pl.program_id(1) == 0)
    def _():
        st_ref[...] = jnp.zeros_like(st_ref)

    row = lax.broadcasted_iota(jnp.int32, (tb, tb), 0)
    col = lax.broadcasted_iota(jnp.int32, (tb, tb), 1)
    same_chunk = (row // HG_CHUNK) == (col // HG_CHUNK)
    tri = same_chunk & ((col >= row) if reverse else (col <= row))
    cum = jnp.where(tri, 1.0, 0.0).astype(BF16)

    b = _dot(cum, g_ref[...])
    qd = (q_ref[...].astype(F32) * jnp.exp(b)).astype(BF16)
    ki = k_ref[...].astype(F32) * jnp.exp(-b)
    ki_b = ki.astype(BF16)

    chunk_order = range(n_chunks - 1, -1, -1) if reverse else range(n_chunks)
    e_end, k_end = {}, {}
    for c in range(n_chunks):
        r0 = c * HG_CHUNK
        last = r0 if reverse else r0 + HG_CHUNK - 1
        e_end[c] = jnp.exp(b[last:last + 1, :])
        k_end[c] = (ki[r0:r0 + HG_CHUNK, :] * e_end[c]).astype(BF16)

    heads = range(n_heads)
    hsl = [slice(h * dh, (h + 1) * dh) for h in heads]
    scores = [jnp.where(tri, _dot_nt(qd[:, hsl[h]], ki_b[:, hsl[h]]), 0.0).astype(BF16) for h in heads]
    o_intra = [_dot(scores[h], v_ref[:, hsl[h]]) for h in heads]
    kv = {(h, c): _dot_tn(v_ref[c * HG_CHUNK:(c + 1) * HG_CHUNK, hsl[h]], k_end[c][:, hsl[h]])
          for h in heads for c in range(n_chunks)}
    st = [st_ref[h] for h in heads]
    for c in chunk_order:
        rs = slice(c * HG_CHUNK, (c + 1) * HG_CHUNK)
        for h in heads:
            hs = hsl[h]
            o_c = o_intra[h][rs, :] + _dot_nt(qd[rs, hs], st[h].astype(BF16))
            st[h] = st[h] * e_end[c][:, hs] + kv[(h, c)]
            if final:
                o_c = o_c + ob_ref[rs, hs].astype(F32)
                o_c = _rms(o_c) * gn_ref[...]
                o_c = o_c * gs_ref[rs, hs].astype(F32)
            o_ref[rs, hs] = o_c.astype(o_ref.dtype)
    for h in heads:
        st_ref[h] = st[h]


def _gla_call(q, k, g, v, n_ctx, reverse, final_args=None):
    b_, t_, d = q.shape
    tb = GLA_BLOCK
    nb = t_ // tb
    ncb = n_ctx // tb
    dh = d // HG_HEADS

    def blk(b, j):
        if reverse:
            jj = jnp.where(j < ncb, ncb - 1 - j, nb - 1 - (j - ncb))
        else:
            jj = j
        return (b, jj, 0)

    tok = pl.BlockSpec((None, tb, d), blk)
    in_specs = [tok] * 4
    args = [q, k, g, v]
    final = final_args is not None
    if final:
        o_b, gs, gn = final_args
        in_specs += [tok, tok, _resident((1, dh))]
        args += [o_b, gs, gn]
    return pl.pallas_call(
        functools.partial(_gla_kernel, reverse=reverse, final=final, n_heads=HG_HEADS),
        grid=(b_, nb),
        in_specs=in_specs,
        out_specs=tok,
        out_shape=jax.ShapeDtypeStruct((b_, t_, d), BF16),
        scratch_shapes=[pltpu.VMEM((HG_HEADS, dh, dh), F32)],
        compiler_params=_cparams(("parallel", "arbitrary")),
        name="gla_bwd" if reverse else "gla_fwd",
    )(*args)


def _outproj_kernel(*refs, n_ctx_blocks, split_residual):
    if split_residual:
        (a_ref, w_ref, xc_ref, x_ref, gt_ref, sh_ref, sc_ref, g_ref, wr_ref,
         xn_ref, h2_ref, eid_ref, gw_ref, rank_ref, cnt_ref, run_ref) = refs
    else:
        (a_ref, w_ref, x_ref, gt_ref, sh_ref, sc_ref, g_ref, wr_ref,
         xn_ref, h2_ref, eid_ref, gw_ref, rank_ref, cnt_ref, run_ref) = refs
    tm = a_ref.shape[0]
    nr = ROUTER_ROWS
    first = (pl.program_id(0) == 0) & (pl.program_id(1) == 0)

    @pl.when(first)
    def _():
        run_ref[...] = jnp.zeros_like(run_ref)

    x = x_ref[...]
    if split_residual:
        x = jnp.where(pl.program_id(1) < n_ctx_blocks, xc_ref[...], x)
    y = _dot(a_ref[...], w_ref[...])
    xn = x + gt_ref[...] * y
    xn_ref[...] = xn
    h2 = _rms(xn) * g_ref[...]
    h2 = h2 * (1.0 + sc_ref[...]) + sh_ref[...]
    h2b = h2.astype(BF16)
    h2_ref[...] = h2b
    h2r = (h2 - h2b.astype(F32)).astype(BF16)
    lg2 = _dot_nt(wr_ref[...], h2b) + _dot_nt(wr_ref[...], h2r)
    lg = lg2[:nr] + lg2[nr:]

    ridx = lax.broadcasted_iota(jnp.int32, lg.shape, 0)
    neg = -jnp.inf
    gl = jnp.where(ridx < MOE_GROUPS, lg, neg)
    gmax = jnp.max(gl, axis=0, keepdims=True)
    g_idx = jnp.min(jnp.where(gl == gmax, ridx, nr), axis=0, keepdims=True)
    p_group = 1.0 / jnp.sum(jnp.exp(gl - gmax), axis=0, keepdims=True)
    lo = MOE_GROUPS + MOE_EPG * g_idx
    el = jnp.where((ridx >= lo) & (ridx < lo + MOE_EPG), lg, neg)
    m1 = jnp.max(el, axis=0, keepdims=True)
    i1 = jnp.min(jnp.where(el == m1, ridx, nr), axis=0, keepdims=True)
    el2 = jnp.where(ridx == i1, neg, el)
    m2 = jnp.max(el2, axis=0, keepdims=True)
    i2 = jnp.min(jnp.where(el2 == m2, ridx, nr), axis=0, keepdims=True)
    r21 = jnp.exp(m2 - m1)
    w1 = p_group / (1.0 + r21)
    w2 = w1 * r21

    hot1 = ridx == i1
    hot2 = ridx == i2
    f1 = jnp.where(hot1, 1.0, 0.0)
    f2 = jnp.where(hot2, 1.0, 0.0)
    r = lax.broadcasted_iota(jnp.int32, (tm, tm), 0)
    c = lax.broadcasted_iota(jnp.int32, (tm, tm), 1)
    earlier = jnp.where(r < c, 1.0, 0.0).astype(BF16)
    pre = _dot(jnp.concatenate([f1, f2], axis=0).astype(BF16), earlier)
    run = run_ref[:, 0:1]
    n1 = jnp.sum(f1, axis=1, keepdims=True)
    n2 = jnp.sum(f2, axis=1, keepdims=True)
    rank1 = jnp.sum(jnp.where(hot1, pre[:nr] + run, 0.0), axis=0, keepdims=True)
    rank2 = jnp.sum(jnp.where(hot2, pre[nr:] + (run + n1), 0.0), axis=0, keepdims=True)
    run = jnp.broadcast_to(run + n1 + n2, run_ref.shape)
    run_ref[...] = run
    cnt_ref[...] = run

    eid_ref[...] = jnp.concatenate([i1, i2], axis=0) - MOE_GROUPS
    rank_ref[...] = jnp.concatenate([rank1, rank2], axis=0).astype(jnp.int32)
    gw_ref[...] = jnp.concatenate([w1, w2, jnp.zeros((GATE_ROWS - MOE_TOP_K, tm), F32)], axis=0)


def _outproj_call(a, w, res_lat, lat_off, res_ctx, gt, sh, sc, g, wr, n_ctx, tm):
    b_, tn, d = a.shape
    ncb = n_ctx // tm
    nt = tn // tm
    lob = lat_off // tm
    split = res_ctx is not None
    tok = pl.BlockSpec((None, tm, d), lambda b, j: (b, j, 0))
    pair = pl.BlockSpec((None, None, MOE_TOP_K, tm), lambda b, j: (b, j, 0, 0))
    res_specs, res_args = [], []
    if split:
        res_specs.append(pl.BlockSpec((None, tm, d), lambda b, j: (b, jnp.minimum(j, ncb - 1), 0)))
        res_args.append(res_ctx)
        res_specs.append(pl.BlockSpec((None, tm, d), lambda b, j: (b, jnp.maximum(j - ncb, 0) + lob, 0)))
    else:
        res_specs.append(pl.BlockSpec((None, tm, d), lambda b, j: (b, j + lob, 0)))
    res_args.append(res_lat)
    return pl.pallas_call(
        functools.partial(_outproj_kernel, n_ctx_blocks=ncb, split_residual=split),
        grid=(b_, nt),
        in_specs=[tok, _resident(w.shape)] + res_specs + [
            _mod_spec(d, ncb, b_), _mod_spec(d, ncb, b_), _mod_spec(d, ncb, b_),
            _resident((1, d)), _resident(wr.shape)],
        out_specs=[tok, tok, pair,
                   pl.BlockSpec((None, None, GATE_ROWS, tm), lambda b, j: (b, j, 0, 0)), pair,
                   pl.BlockSpec((ROUTER_ROWS, LANES), lambda b, j: (0, 0))],
        out_shape=[jax.ShapeDtypeStruct((b_, tn, d), F32),
                   jax.ShapeDtypeStruct((b_, tn, d), BF16),
                   jax.ShapeDtypeStruct((b_, nt, MOE_TOP_K, tm), jnp.int32),
                   jax.ShapeDtypeStruct((b_, nt, GATE_ROWS, tm), F32),
                   jax.ShapeDtypeStruct((b_, nt, MOE_TOP_K, tm), jnp.int32),
                   jax.ShapeDtypeStruct((ROUTER_ROWS, LANES), F32)],
        scratch_shapes=[pltpu.VMEM((ROUTER_ROWS, LANES), F32)],
        compiler_params=_cparams(("arbitrary", "arbitrary")),
        name="outproj_ffnmod",
    )(a, w, *res_args, gt, sh, sc, g, wr)


def _expert_kernel(be_ref, nu_ref, x_ref, wg_ref, wu_ref, wd_ref, y_ref, wgb_ref, wub_ref, wdb_ref):
    i = pl.program_id(0)
    new_expert = (i == 0) | (be_ref[i] != be_ref[jnp.maximum(i - 1, 0)])

    @pl.when(new_expert)
    def _():
        wgb_ref[...] = wg_ref[...].astype(BF16)
        wub_ref[...] = wu_ref[...].astype(BF16)
        wdb_ref[...] = wd_ref[...].astype(BF16)

    @pl.when(i < nu_ref[0])
    def _():
        rows = x_ref.shape[0] // 2
        halves = [slice(j * rows, (j + 1) * rows) for j in range(2)]
        gates = [_dot(x_ref[r, :], wgb_ref[...]) for r in halves]
        ups = [_dot(x_ref[r, :], wub_ref[...]) for r in halves]
        for j, r in enumerate(halves):
            act = (gates[j] * _sigmoid(gates[j]) * ups[j]).astype(BF16)
            y_ref[r, :] = _dot(act, wdb_ref[...]).astype(y_ref.dtype)

    @pl.when(i >= nu_ref[0])
    def _():
        y_ref[...] = jnp.zeros_like(y_ref)


def _expert_call(buf, block_expert, n_used, wg, wu, wd, layer):
    r_pad, d = buf.shape
    n_blocks = r_pad // MOE_BLOCK
    ff = wg.shape[-1]

    def x_idx(i, be, nu):
        return (jnp.maximum(jnp.minimum(i, nu[0] - 1), 0), 0)

    grid_spec = pltpu.PrefetchScalarGridSpec(
        num_scalar_prefetch=2,
        grid=(n_blocks,),
        in_specs=[
            pl.BlockSpec((MOE_BLOCK, d), x_idx),
            pl.BlockSpec((None, None, d, ff), lambda i, be, nu: (layer, be[i], 0, 0)),
            pl.BlockSpec((None, None, d, ff), lambda i, be, nu: (layer, be[i], 0, 0)),
            pl.BlockSpec((None, None, ff, d), lambda i, be, nu: (layer, be[i], 0, 0)),
        ],
        out_specs=pl.BlockSpec((MOE_BLOCK, d), lambda i, be, nu: (i, 0)),
        scratch_shapes=[pltpu.VMEM((d, ff), BF16), pltpu.VMEM((d, ff), BF16),
                        pltpu.VMEM((ff, d), BF16)],
    )
    return pl.pallas_call(
        _expert_kernel,
        grid_spec=grid_spec,
        out_shape=jax.ShapeDtypeStruct((r_pad, d), BF16),
        compiler_params=_cparams(("arbitrary",)),
        name="moe_experts",
    )(block_expert, n_used, buf, wg, wu, wd)


def _combine_kernel(*refs, with_next):
    if with_next:
        y0_ref, y1_ref, gw_ref, x_ref, gt_ref, sh_ref, sc_ref, g_ref, xo_ref, hn_ref = refs
    else:
        y0_ref, y1_ref, gw_ref, x_ref, gt_ref, xo_ref = refs
    gw = gw_ref[...].T
    f = gw[:, 0:1] * y0_ref[...].astype(F32) + gw[:, 1:2] * y1_ref[...].astype(F32)
    xo = x_ref[...] + gt_ref[...] * f
    xo_ref[...] = xo
    if with_next:
        h = _rms(xo) * g_ref[...]
        hn_ref[...] = (h * (1.0 + sc_ref[...]) + sh_ref[...]).astype(BF16)


def _combine_call(y0, y1, gw, x, gt, n_ctx, tm, next_mod=None):
    b_, tn, d = x.shape
    ncb = n_ctx // tm
    tok = pl.BlockSpec((None, tm, d), lambda b, j: (b, j, 0))
    in_specs = [tok, tok, pl.BlockSpec((None, None, GATE_ROWS, tm), lambda b, j: (b, j, 0, 0)),
                tok, _mod_spec(d, ncb, b_)]
    args = [y0, y1, gw, x, gt]
    out_specs = [tok]
    out_shape = [jax.ShapeDtypeStruct((b_, tn, d), F32)]
    with_next = next_mod is not None
    if with_next:
        sh, sc, g = next_mod
        in_specs += [_mod_spec(d, ncb, b_), _mod_spec(d, ncb, b_), _resident((1, d))]
        args += [sh, sc, g]
        out_specs.append(tok)
        out_shape.append(jax.ShapeDtypeStruct((b_, tn, d), BF16))
    return pl.pallas_call(
        functools.partial(_combine_kernel, with_next=with_next),
        grid=(b_, tn // tm),
        in_specs=in_specs,
        out_specs=out_specs,
        out_shape=out_shape,
        compiler_params=_cparams(("parallel", "arbitrary")),
        name="moe_combine",
    )(*args)


def _mla_in_kernel(h_ref, win_ref, qg_ref, wq_ref, wqr_ref, tqc_ref, tqs_ref,
                   ktab_ref, wk_ref, wv_ref, kgain_ref, q_ref, k_ref, v_ref, *, n_ctx_blocks):
    tm = h_ref.shape[0]
    proj = _dot(h_ref[...], win_ref[...])

    @pl.when(pl.program_id(1) >= n_ctx_blocks)
    def _():
        cq = proj[:, :MLA_Q_LORA]
        qn = (_rms(cq) * qg_ref[...]).astype(BF16)
        qa = _dot(qn, wq_ref[...])
        qr = _dot(qn, wqr_ref[...])
        tqc = tqc_ref[...]
        tqs = tqs_ref[...]
        for h in range(MLA_HEADS):
            hs = slice(h * LANES, (h + 1) * LANES)
            a = qa[:, hs]
            rq = lax.rsqrt(jnp.sum(a * a, axis=-1, keepdims=True) * (1.0 / MLA_QK) + NORM_EPS)
            q_ref[h] = ((a * tqc + qr[:, hs] * tqs) * rq).astype(BF16)

    p2 = proj[:, MLA_Q_LORA:]
    lane = lax.broadcasted_iota(jnp.int32, (tm, p2.shape[1]), 1)
    is_kv = lane < MLA_KV_LORA
    sq = p2 * p2
    ms_kv = jnp.sum(jnp.where(is_kv, sq, 0.0), axis=-1, keepdims=True) * (1.0 / MLA_KV_LORA)
    is_rope = (lane >= MLA_KV_LORA) & (lane < MLA_KV_LORA + MLA_ROPE)
    ss_rope = jnp.sum(jnp.where(is_rope, sq, 0.0), axis=-1, keepdims=True)
    mult = ktab_ref[...] * jnp.where(is_kv, lax.rsqrt(ms_kv + NORM_EPS), 1.0)
    lhs = (p2 * mult).astype(BF16)
    ka = _dot(lhs, wk_ref[...])
    va = _dot(lhs[:, :MLA_KV_LORA], wv_ref[...])
    kgain = kgain_ref[...]
    lane1 = lax.broadcasted_iota(jnp.int32, (tm, LANES), 1)
    for h in range(MLA_HEADS):
        hs = slice(h * LANES, (h + 1) * LANES)
        a = ka[:, hs]
        ssn = jnp.sum(jnp.where(lane1 < MLA_NOPE, a * a, 0.0), axis=-1, keepdims=True)
        rk = lax.rsqrt((ssn + ss_rope) * (1.0 / MLA_QK) + NORM_EPS)
        k_ref[h] = (a * kgain * rk).astype(BF16)
        ones_lane = MLA_V if h % 2 == 0 else 0
        v_ref[h] = jnp.where(lane1 == ones_lane, 1.0, va[:, hs]).astype(BF16)


def _mla_in_call(hm, win, qg, wq, wqr, tqc, tqs, ktab, wk, wv, kgain, n_ctx, tm):
    b_, t_, d = hm.shape
    ncb = n_ctx // tm
    tok = pl.BlockSpec((None, tm, d), lambda b, j: (b, j, 0))
    head = pl.BlockSpec((None, MLA_HEADS, tm, LANES), lambda b, j: (b, 0, j, 0))
    hshape = jax.ShapeDtypeStruct((b_, MLA_HEADS, t_, LANES), BF16)
    qhead = pl.BlockSpec((None, MLA_HEADS, tm, LANES), lambda b, j: (b, 0, jnp.maximum(j - ncb, 0), 0))
    qshape = jax.ShapeDtypeStruct((b_, MLA_HEADS, t_ - n_ctx, LANES), BF16)
    return pl.pallas_call(
        functools.partial(_mla_in_kernel, n_ctx_blocks=ncb),
        grid=(b_, t_ // tm),
        in_specs=[tok, _resident(win.shape), _resident(qg.shape), _resident(wq.shape),
                  _resident(wqr.shape),
                  pl.BlockSpec((tm, LANES), lambda b, j: (j, 0)),
                  pl.BlockSpec((tm, LANES), lambda b, j: (j, 0)),
                  pl.BlockSpec((tm, ktab.shape[1]), lambda b, j: (j, 0)),
                  _resident(wk.shape), _resident(wv.shape), _resident(kgain.shape)],
        out_specs=[qhead, head, head],
        out_shape=[qshape, hshape, hshape],
        compiler_params=_cparams(("parallel", "arbitrary")),
        name="mla_in",
    )(hm, win, qg, wq, wqr, tqc, tqs, ktab, wk, wv, kgain)


def _attn_kernel(q_ref, k_ref, v_ref, o_ref):
    n_heads, tq, _ = q_ref.shape
    t_ = k_ref.shape[1]
    ck = next(c for c in ATT_KEY_CHUNKS if t_ % c == 0)
    outs = []
    for hh in range(n_heads):
        q = q_ref[hh]
        m = jnp.full((tq, 1), -jnp.inf, F32)
        acc = jnp.zeros((tq, LANES), F32)
        for c in range(t_ // ck):
            ks = slice(c * ck, (c + 1) * ck)
            s = _dot_nt(q, k_ref[hh, ks, :])
            m_new = jnp.maximum(m, jnp.max(s, axis=-1, keepdims=True))
            p = jnp.exp2((s - m_new).astype(BF16))
            acc = acc * jnp.exp2(m - m_new) + _dot(p, v_ref[hh, ks, :])
            m = m_new
        ones_lane = MLA_V if hh % 2 == 0 else 0
        outs.append(acc / acc[:, ones_lane:ones_lane + 1])
    lane = lax.broadcasted_iota(jnp.int32, outs[0].shape, 1)
    for pr in range(len(outs) // 2):
        o_ref[:, pr * LANES:(pr + 1) * LANES] = jnp.where(
            lane < MLA_V, outs[2 * pr], outs[2 * pr + 1]).astype(o_ref.dtype)


def _attn_call(q, k, v, tq):
    b_, nh, l_, _ = q.shape
    t_ = k.shape[2]
    hps = ATT_HEADS_PER_STEP
    return pl.pallas_call(
        _attn_kernel,
        grid=(b_, nh // hps, l_ // tq),
        in_specs=[
            pl.BlockSpec((None, hps, tq, LANES), lambda b, h, i: (b, h, i, 0)),
            pl.BlockSpec((None, hps, t_, LANES), lambda b, h, i: (b, h, 0, 0)),
            pl.BlockSpec((None, hps, t_, LANES), lambda b, h, i: (b, h, 0, 0)),
        ],
        out_specs=pl.BlockSpec((None, tq, (hps // 2) * LANES), lambda b, h, i: (b, i, h)),
        out_shape=jax.ShapeDtypeStruct((b_, l_, (nh // 2) * LANES), BF16),
        compiler_params=_cparams(("parallel", "parallel", "arbitrary")),
        name="mla_attention",
    )(q, k, v)


def _dispatch_layout(eid, rank, counts):
    tm = eid.shape[-1]
    n_slot = eid.size
    starts = jnp.cumsum(counts) - counts
    padded = (counts + MOE_BLOCK - 1) // MOE_BLOCK * MOE_BLOCK
    pad_ends = jnp.cumsum(padded)
    pad_starts = pad_ends - padded
    n_blocks = (n_slot + MOE_BLOCK - 1) // MOE_BLOCK + MOE_EXPERTS
    block_start = jnp.arange(n_blocks, dtype=jnp.int32) * MOE_BLOCK
    block_expert = jnp.minimum(
        jnp.sum((block_start[:, None] >= pad_ends[None, :]).astype(jnp.int32), axis=1),
        MOE_EXPERTS - 1)
    n_used = pad_ends[-1:] // MOE_BLOCK
    slot_row = rank
    for e in range(MOE_EXPERTS):
        slot_row = slot_row + jnp.where(eid == e, pad_starts[e], 0)
    order = jnp.argsort(eid.reshape(n_slot)).astype(jnp.int32)
    offs = (block_start - pad_starts[block_expert])[:, None] + jnp.arange(MOE_BLOCK, dtype=jnp.int32)
    valid = offs < counts[block_expert][:, None]
    pos = jnp.minimum(starts[block_expert][:, None] + offs, n_slot - 1)
    slot = jnp.take(order, pos, mode='clip')
    tok = (slot // (MOE_TOP_K * tm)) * tm + slot % tm
    row_tok = jnp.where(valid, tok, 0)
    return row_tok.reshape(-1), slot_row, block_expert, n_used


def _moe(h2, eid, rank, counts, wg, wu, wd, layer):
    b_, tn, d = h2.shape
    n_tok = b_ * tn
    cnt = counts[MOE_GROUPS:MOE_GROUPS + MOE_EXPERTS, 0].astype(jnp.int32)
    row_tok, slot_row, block_expert, n_used = _dispatch_layout(eid, rank, cnt)
    buf = jnp.take(h2.reshape(n_tok, d), row_tok, axis=0, mode='clip')
    ybuf = _expert_call(buf, block_expert, n_used, wg, wu, wd, layer)
    y0 = jnp.take(ybuf, slot_row[:, :, 0, :].reshape(n_tok), axis=0, mode='clip').reshape(b_, tn, d)
    y1 = jnp.take(ybuf, slot_row[:, :, 1, :].reshape(n_tok), axis=0, mode='clip').reshape(b_, tn, d)
    return y0, y1


def _rot_half_perm():
    half = MLA_ROPE // 2
    j = jnp.arange(MLA_ROPE)
    within = j % half
    base = j - within
    src = jnp.where(within < half // 2, base + within + half // 2, base + within - half // 2)
    sign = jnp.where(within < half // 2, -1.0, 1.0).astype(F32)
    return src, sign


def _rope_tables(n_ctx, seq):
    rows = seq // GRID_W
    row = jnp.repeat(jnp.arange(rows), GRID_W)
    col = jnp.tile(jnp.arange(GRID_W), rows)
    half = MLA_ROPE // 2
    inv_freq = ROPE_THETA ** (-jnp.arange(0, half, 2, dtype=F32) / half)
    ang = jnp.stack([row, col], axis=-1).astype(F32)[..., None] * inv_freq
    ang = jnp.concatenate([ang, ang], axis=-1).reshape(seq, MLA_ROPE)
    cos = jnp.concatenate([jnp.ones((n_ctx, MLA_ROPE), F32), jnp.cos(ang)], axis=0)
    sin = jnp.concatenate([jnp.zeros((n_ctx, MLA_ROPE), F32), jnp.sin(ang)], axis=0)
    return cos, sin


def _mla_prepare(w_in, w_qb, w_kvb, q_qk_g, k_qk_g, kv_norm_g, n_ctx, seq):
    d = w_in.shape[0]
    src, sign = _rot_half_perm()
    cos, sin = _rope_tables(n_ctx, seq)
    t_ = n_ctx + seq

    rope0 = MLA_Q_LORA + MLA_KV_LORA
    w_rope = w_in[:, rope0:rope0 + MLA_ROPE]
    win = jnp.concatenate(
        [w_in, w_rope[:, src] * sign, jnp.zeros((d, 512 - rope0 - 2 * MLA_ROPE), F32)], axis=1)

    wq3 = w_qb.reshape(MLA_Q_LORA, MLA_HEADS, MLA_QK)
    pad = jnp.zeros((MLA_Q_LORA, MLA_HEADS, LANES - MLA_QK), F32)
    wq = jnp.concatenate([wq3, pad], axis=-1).reshape(MLA_Q_LORA, MLA_HEADS * LANES)
    wq_rot = wq3[:, :, MLA_NOPE:][:, :, src] * sign
    wqr = jnp.concatenate([jnp.zeros((MLA_Q_LORA, MLA_HEADS, MLA_NOPE), F32), wq_rot, pad],
                          axis=-1).reshape(MLA_Q_LORA, MLA_HEADS * LANES)

    scale = MLA_QK ** -0.5 * 1.4426950408889634
    gq_n, gq_r = q_qk_g[:MLA_NOPE], q_qk_g[MLA_NOPE:]
    zpad = jnp.zeros((t_, LANES - MLA_QK), F32)
    tqc = jnp.concatenate([jnp.broadcast_to(gq_n, (t_, MLA_NOPE)), gq_r * cos, zpad], axis=1) * scale
    tqs = jnp.concatenate([jnp.zeros((t_, MLA_NOPE), F32), gq_r[src] * sin, zpad], axis=1) * scale

    wkv3 = w_kvb.reshape(MLA_KV_LORA, MLA_HEADS, MLA_NOPE + MLA_V)
    wk_lat = jnp.concatenate(
        [wkv3[:, :, :MLA_NOPE], jnp.zeros((MLA_KV_LORA, MLA_HEADS, LANES - MLA_NOPE), F32)], axis=-1)
    place = jnp.concatenate([jnp.zeros((MLA_ROPE, MLA_NOPE), F32), jnp.eye(MLA_ROPE, dtype=F32),
                             jnp.zeros((MLA_ROPE, LANES - MLA_QK), F32)], axis=1)
    place = jnp.broadcast_to(place[:, None, :], (MLA_ROPE, MLA_HEADS, LANES))
    wk = jnp.concatenate([wk_lat, place, place,
                          jnp.zeros((256 - MLA_KV_LORA - 2 * MLA_ROPE, MLA_HEADS, LANES), F32)],
                         axis=0).reshape(256, MLA_HEADS * LANES)
    gk_n, gk_r = k_qk_g[:MLA_NOPE], k_qk_g[MLA_NOPE:]
    ktab = jnp.concatenate([jnp.broadcast_to(kv_norm_g, (t_, MLA_KV_LORA)), gk_r * cos,
                            gk_r[src] * sin, jnp.zeros((t_, 256 - MLA_KV_LORA - 2 * MLA_ROPE), F32)],
                           axis=1)
    kgain = jnp.concatenate([gk_n, jnp.ones((MLA_ROPE,), F32),
                             jnp.zeros((LANES - MLA_QK,), F32)]).reshape(1, LANES)

    wv_h = wkv3[:, :, MLA_NOPE:]
    zv = jnp.zeros_like(wv_h)
    odd = (jnp.arange(MLA_HEADS) % 2 == 1)[None, :, None]
    wv = jnp.concatenate([jnp.where(odd, zv, wv_h), jnp.where(odd, wv_h, zv)],
                         axis=-1).reshape(MLA_KV_LORA, MLA_HEADS * LANES)
    return (win.astype(BF16), wq.astype(BF16), wqr.astype(BF16), tqc, tqs, ktab,
            wk.astype(BF16), wv.astype(BF16), kgain)


def _router_weights(w_group, w_expert):
    d = w_group.shape[0]
    wt = jnp.concatenate([w_group, w_expert], axis=1).T
    wt = jnp.concatenate([wt, jnp.zeros((ROUTER_ROWS - wt.shape[0], d), F32)], axis=0)
    head = wt.astype(BF16)
    rest = (wt - head.astype(F32)).astype(BF16)
    return jnp.concatenate([head, rest], axis=0)


def kernel(x, c, ctx, c_ctx, ada_w, ada_b, norm_mix_g, norm_ffn_g, hg_w_in, hg_lower_bounds, hg_out_norm_g, hg_w_out, mla_w_in, mla_q_norm_g, mla_kv_norm_g, mla_w_qb, mla_w_kvb, mla_q_qknorm_g, mla_k_qknorm_g, mla_w_out, moe_w_group, moe_w_expert, moe_w_gate, moe_w_up, moe_w_down):
    b_, seq, d = x.shape
    n_ctx = ctx.shape[1]
    tm = min(ROW_TILE, n_ctx)
    tq = min(ATT_Q_TILE, seq)
    assert n_ctx % GLA_BLOCK == 0 and seq % GLA_BLOCK == 0 and n_ctx % tm == 0 and seq % tm == 0
    assert b_ + 1 <= ADA_ROWS and seq % GRID_W == 0

    cc = jnp.concatenate([c, c_ctx[None, :], jnp.zeros((ADA_ROWS - b_ - 1, d), F32)], axis=0)
    mods = _ada_call(cc, ada_w, ada_b).reshape(DEPTH, ADA_CHUNKS, ADA_ROWS, 1, d)

    def mod(i, chunk):
        return mods[i, chunk]

    row = lambda v: v.reshape(1, -1)

    lower = jnp.cumsum(jax.nn.softmax(hg_lower_bounds.astype(F32), axis=1), axis=1)[:, 0]
    qs, kf, gf, kb, gb, v, gs = _hg_in_call(
        ctx, x, mod(0, 0), mod(0, 1), row(norm_mix_g[0]), hg_w_in[0].astype(BF16), lower, tm)
    o_b = _gla_call(qs, kb, gb, v, n_ctx, reverse=True)
    og = _gla_call(qs, kf, gf, v, n_ctx, reverse=False,
                   final_args=(o_b, gs, row(hg_out_norm_g[0])))
    x1, h2, eid, gw, rank, cnt = _outproj_call(
        og, hg_w_out[0].astype(BF16), x, 0, ctx, mod(0, 2), mod(0, 3), mod(0, 4),
        row(norm_ffn_g[0]), _router_weights(moe_w_group[0], moe_w_expert[0]), n_ctx, tm)
    y0, y1 = _moe(h2, eid, rank, cnt, moe_w_gate, moe_w_up, moe_w_down, 0)
    x2, hm = _combine_call(y0, y1, gw, x1, mod(0, 5), n_ctx, tm,
                           next_mod=(mod(1, 0), mod(1, 1), row(norm_mix_g[1])))

    win, wq, wqr, tqc, tqs, ktab, wk, wv, kgain = _mla_prepare(
        mla_w_in[0], mla_w_qb[0], mla_w_kvb[0], mla_q_qknorm_g[0], mla_k_qknorm_g[0],
        mla_kv_norm_g[0], n_ctx, seq)
    q, k, vv = _mla_in_call(hm, win, row(mla_q_norm_g[0]), wq, wqr, tqc, tqs, ktab, wk, wv, kgain,
                            n_ctx, tm)
    o_att = _attn_call(q, k, vv, tq)
    x3, h2, eid, gw, rank, cnt = _outproj_call(
        o_att, mla_w_out[0].astype(BF16), x2, n_ctx, None, mod(1, 2), mod(1, 3), mod(1, 4),
        row(norm_ffn_g[1]), _router_weights(moe_w_group[1], moe_w_expert[1]), 0, tm)
    y0, y1 = _moe(h2, eid, rank, cnt, moe_w_gate, moe_w_up, moe_w_down, 1)
    (x4,) = _combine_call(y0, y1, gw, x3, mod(1, 5), 0, tm)
    return x4
```

```python
import functools

import jax
import jax.numpy as jnp
from jax import lax
from jax.experimental import pallas as pl
from jax.experimental.pallas import tpu as pltpu

F32 = jnp.float32
BF16 = jnp.bfloat16

DEPTH = 2
ADA_CHUNKS = 6
NORM_EPS = 1e-6
GRID_W = 64
HG_HEADS = 8
HG_CHUNK = 32
MLA_HEADS = 16
MLA_Q_LORA = 256
MLA_KV_LORA = 128
MLA_NOPE = 64
MLA_ROPE = 32
MLA_V = 64
MLA_QK = MLA_NOPE + MLA_ROPE
ROPE_THETA = 10000.0
MOE_GROUPS = 4
MOE_EPG = 8
MOE_EXPERTS = MOE_GROUPS * MOE_EPG
MOE_TOP_K = 2
MOE_FF = 512

LANES = 128
ROUTER_ROWS = 48
GATE_ROWS = 8
VMEM_LIMIT = 56 * 1024 * 1024

ROW_TILE = 256
GLA_BLOCK = 256
MOE_BLOCK = 512
ATT_Q_TILE = 512
ATT_HEADS_PER_STEP = 8
ATT_KEY_CHUNKS = (256, 128)
ADA_ROWS = 40


def _cparams(sem):
    return pltpu.CompilerParams(dimension_semantics=sem, vmem_limit_bytes=VMEM_LIMIT)


def _resident(shape):
    nd = len(shape)
    return pl.BlockSpec(shape, lambda *_: (0,) * nd, pipeline_mode=pl.Buffered(1))


def _dot(a, b):
    return jnp.dot(a, b, preferred_element_type=F32)


def _dot_nt(a, b):
    return lax.dot_general(a, b, (((1,), (1,)), ((), ())), preferred_element_type=F32)


def _dot_tn(a, b):
    return lax.dot_general(a, b, (((0,), (0,)), ((), ())), preferred_element_type=F32)


def _sigmoid(x):
    return 1.0 / (1.0 + jnp.exp(-x))


def _rms(x):
    return x * lax.rsqrt(jnp.mean(x * x, axis=-1, keepdims=True) + NORM_EPS)


def _ada_kernel(c_ref, w_ref, b_ref, o_ref):
    c = c_ref[...]
    a = (c * _sigmoid(c)).astype(BF16)
    o_ref[...] = _dot(a, w_ref[...].astype(BF16)) + b_ref[...]


def _ada_call(cc, ada_w, ada_b):
    depth, d, _ = ada_w.shape
    rows = cc.shape[0]
    return pl.pallas_call(
        _ada_kernel,
        grid=(depth, ADA_CHUNKS),
        in_specs=[
            pl.BlockSpec((rows, d), lambda i, j: (0, 0)),
            pl.BlockSpec((None, d, d), lambda i, j: (i, 0, j)),
            pl.BlockSpec((None, 1, d), lambda i, j: (i, 0, j)),
        ],
        out_specs=pl.BlockSpec((None, None, rows, d), lambda i, j: (i, j, 0, 0)),
        out_shape=jax.ShapeDtypeStruct((depth, ADA_CHUNKS, rows, d), F32),
        compiler_params=_cparams(("arbitrary", "arbitrary")),
        name="ada_mod",
    )(cc, ada_w, ada_b.reshape(depth, 1, ADA_CHUNKS * d))


def _mod_spec(d, n_ctx_blocks, ctx_row):
    def idx(b, j):
        return (jnp.where(j < n_ctx_blocks, ctx_row, b), 0, 0)
    return pl.BlockSpec((None, 1, d), idx)


def _hg_in_kernel(xc_ref, x_ref, sh_ref, sc_ref, g_ref, w_ref, lb_ref,
                  q_ref, kf_ref, gf_ref, kb_ref, gb_ref, v_ref, gs_ref, *, n_ctx_blocks):
    d = x_ref.shape[-1]
    x = jnp.where(pl.program_id(1) < n_ctx_blocks, xc_ref[...], x_ref[...])
    h = _rms(x) * g_ref[...]
    h = (h * (1.0 + sc_ref[...]) + sh_ref[...]).astype(BF16)

    def proj(c):
        return _dot(h, w_ref[:, c * d:(c + 1) * d])

    p = proj(0)
    q_ref[...] = (p * _sigmoid(p)).astype(BF16)
    for c, k_ref, lg_ref in ((1, kf_ref, gf_ref), (2, kb_ref, gb_ref)):
        s = _sigmoid(proj(c))
        lb = lb_ref[c - 1:c, :]
        k_ref[...] = ((1.0 - lb) * (1.0 - s)).astype(BF16)
        lg_ref[...] = jnp.log(lb + (1.0 - lb) * s).astype(BF16)
    v_ref[...] = proj(3).astype(BF16)
    p = proj(4)
    gs_ref[...] = (p * _sigmoid(p)).astype(BF16)


def _hg_in_call(ctx, x, sh, sc, g, w_in, lb, tm):
    b_, n_ctx, d = ctx.shape
    t_ = n_ctx + x.shape[1]
    ncb = n_ctx // tm
    tok = pl.BlockSpec((None, tm, d), lambda b, j: (b, j, 0))
    out = jax.ShapeDtypeStruct((b_, t_, d), BF16)
    return pl.pallas_call(
        functools.partial(_hg_in_kernel, n_ctx_blocks=ncb),
        grid=(b_, t_ // tm),
        in_specs=[pl.BlockSpec((None, tm, d), lambda b, j: (b, jnp.minimum(j, ncb - 1), 0)),
                  pl.BlockSpec((None, tm, d), lambda b, j: (b, jnp.maximum(j - ncb, 0), 0)),
                  _mod_spec(d, ncb, b_), _mod_spec(d, ncb, b_),
                  _resident((1, d)), _resident(w_in.shape), _resident((2, d))],
        out_specs=[tok] * 7,
        out_shape=[out] * 7,
        compiler_params=_cparams(("parallel", "arbitrary")),
        name="hg_in",
    )(ctx, x, sh, sc, g, w_in, lb)


def _gla_kernel(*refs, reverse, final, n_heads):
    if final:
        q_ref, k_ref, g_ref, v_ref, ob_ref, gs_ref, gn_ref, o_ref, st_ref = refs
    else:
        q_ref, k_ref, g_ref, v_ref, o_ref, st_ref = refs
    tb, d = q_ref.shape
    dh = d // n_heads
    n_chunks = tb // HG_CHUNK

    @pl.when(pl.program_id(1) == 0)
    def _():
        st_ref[...] = jnp.zeros_like(st_ref)

    row = lax.broadcasted_iota(jnp.int32, (tb, tb), 0)
    col = lax.broadcasted_iota(jnp.int32, (tb, tb), 1)
    same_chunk = (row // HG_CHUNK) == (col // HG_CHUNK)
    tri = same_chunk & ((col >= row) if reverse else (col <= row))
    cum = jnp.where(tri, 1.0, 0.0).astype(BF16)

    b = _dot(cum, g_ref[...])
    qd = (q_ref[...].astype(F32) * jnp.exp(b)).astype(BF16)
    ki = k_ref[...].astype(F32) * jnp.exp(-b)
    ki_b = ki.astype(BF16)

    chunk_order = range(n_chunks - 1, -1, -1) if reverse else range(n_chunks)
    e_end, k_end = {}, {}
    for c in range(n_chunks):
        r0 = c * HG_CHUNK
        last = r0 if reverse else r0 + HG_CHUNK - 1
        e_end[c] = jnp.exp(b[last:last + 1, :])
        k_end[c] = (ki[r0:r0 + HG_CHUNK, :] * e_end[c]).astype(BF16)

    heads = range(n_heads)
    hsl = [slice(h * dh, (h + 1) * dh) for h in heads]
    scores = [jnp.where(tri, _dot_nt(qd[:, hsl[h]], ki_b[:, hsl[h]]), 0.0).astype(BF16) for h in heads]
    o_intra = [_dot(scores[h], v_ref[:, hsl[h]]) for h in heads]
    kv = {(h, c): _dot_tn(v_ref[c * HG_CHUNK:(c + 1) * HG_CHUNK, hsl[h]], k_end[c][:, hsl[h]])
          for h in heads for c in range(n_chunks)}
    st = [st_ref[h] for h in heads]
    for c in chunk_order:
        rs = slice(c * HG_CHUNK, (c + 1) * HG_CHUNK)
        for h in heads:
            hs = hsl[h]
            o_c = o_intra[h][rs, :] + _dot_nt(qd[rs, hs], st[h].astype(BF16))
            st[h] = st[h] * e_end[c][:, hs] + kv[(h, c)]
            if final:
                o_c = o_c + ob_ref[rs, hs].astype(F32)
                o_c = _rms(o_c) * gn_ref[...]
                o_c = o_c * gs_ref[rs, hs].astype(F32)
            o_ref[rs, hs] = o_c.astype(o_ref.dtype)
    for h in heads:
        st_ref[h] = st[h]


def _gla_call(q, k, g, v, n_ctx, reverse, final_args=None):
    b_, t_, d = q.shape
    tb = GLA_BLOCK
    nb = t_ // tb
    ncb = n_ctx // tb
    dh = d // HG_HEADS

    def blk(b, j):
        if reverse:
            jj = jnp.where(j < ncb, ncb - 1 - j, nb - 1 - (j - ncb))
        else:
            jj = j
        return (b, jj, 0)

    tok = pl.BlockSpec((None, tb, d), blk)
    in_specs = [tok] * 4
    args = [q, k, g, v]
    final = final_args is not None
    if final:
        o_b, gs, gn = final_args
        in_specs += [tok, tok, _resident((1, dh))]
        args += [o_b, gs, gn]
    return pl.pallas_call(
        functools.partial(_gla_kernel, reverse=reverse, final=final, n_heads=HG_HEADS),
        grid=(b_, nb),
        in_specs=in_specs,
        out_specs=tok,
        out_shape=jax.ShapeDtypeStruct((b_, t_, d), BF16),
        scratch_shapes=[pltpu.VMEM((HG_HEADS, dh, dh), F32)],
        compiler_params=_cparams(("parallel", "arbitrary")),
        name="gla_bwd" if reverse else "gla_fwd",
    )(*args)


def _outproj_kernel(*refs, n_ctx_blocks, split_residual):
    if split_residual:
        (a_ref, w_ref, xc_ref, x_ref, gt_ref, sh_ref, sc_ref, g_ref, wr_ref,
         xn_ref, h2_ref, eid_ref, gw_ref, rank_ref, cnt_ref, run_ref) = refs
    else:
        (a_ref, w_ref, x_ref, gt_ref, sh_ref, sc_ref, g_ref, wr_ref,
         xn_ref, h2_ref, eid_ref, gw_ref, rank_ref, cnt_ref, run_ref) = refs
    tm = a_ref.shape[0]
    nr = ROUTER_ROWS
    first = (pl.program_id(0) == 0) & (pl.program_id(1) == 0)

    @pl.when(first)
    def _():
        run_ref[...] = jnp.zeros_like(run_ref)

    x = x_ref[...]
    if split_residual:
        x = jnp.where(pl.program_id(1) < n_ctx_blocks, xc_ref[...], x)
    y = _dot(a_ref[...], w_ref[...])
    xn = x + gt_ref[...] * y
    xn_ref[...] = xn
    h2 = _rms(xn) * g_ref[...]
    h2 = h2 * (1.0 + sc_ref[...]) + sh_ref[...]
    h2b = h2.astype(BF16)
    h2_ref[...] = h2b
    h2r = (h2 - h2b.astype(F32)).astype(BF16)
    lg2 = _dot_nt(wr_ref[...], h2b) + _dot_nt(wr_ref[...], h2r)
    lg = lg2[:nr] + lg2[nr:]

    ridx = lax.broadcasted_iota(jnp.int32, lg.shape, 0)
    neg = -jnp.inf
    gl = jnp.where(ridx < MOE_GROUPS, lg, neg)
    gmax = jnp.max(gl, axis=0, keepdims=True)
    g_idx = jnp.min(jnp.where(gl == gmax, ridx, nr), axis=0, keepdims=True)
    p_group = 1.0 / jnp.sum(jnp.exp(gl - gmax), axis=0, keepdims=True)
    lo = MOE_GROUPS + MOE_EPG * g_idx
    el = jnp.where((ridx >= lo) & (ridx < lo + MOE_EPG), lg, neg)
    m1 = jnp.max(el, axis=0, keepdims=True)
    i1 = jnp.min(jnp.where(el == m1, ridx, nr), axis=0, keepdims=True)
    el2 = jnp.where(ridx == i1, neg, el)
    m2 = jnp.max(el2, axis=0, keepdims=True)
    i2 = jnp.min(jnp.where(el2 == m2, ridx, nr), axis=0, keepdims=True)
    r21 = jnp.exp(m2 - m1)
    w1 = p_group / (1.0 + r21)
    w2 = w1 * r21

    hot1 = ridx == i1
    hot2 = ridx == i2
    f1 = jnp.where(hot1, 1.0, 0.0)
    f2 = jnp.where(hot2, 1.0, 0.0)
    r = lax.broadcasted_iota(jnp.int32, (tm, tm), 0)
    c = lax.broadcasted_iota(jnp.int32, (tm, tm), 1)
    earlier = jnp.where(r < c, 1.0, 0.0).astype(BF16)
    pre = _dot(jnp.concatenate([f1, f2], axis=0).astype(BF16), earlier)
    run = run_ref[:, 0:1]
    n1 = jnp.sum(f1, axis=1, keepdims=True)
    n2 = jnp.sum(f2, axis=1, keepdims=True)
    rank1 = jnp.sum(jnp.where(hot1, pre[:nr] + run, 0.0), axis=0, keepdims=True)
    rank2 = jnp.sum(jnp.where(hot2, pre[nr:] + (run + n1), 0.0), axis=0, keepdims=True)
    run = jnp.broadcast_to(run + n1 + n2, run_ref.shape)
    run_ref[...] = run
    cnt_ref[...] = run

    eid_ref[...] = jnp.concatenate([i1, i2], axis=0) - MOE_GROUPS
    rank_ref[...] = jnp.concatenate([rank1, rank2], axis=0).astype(jnp.int32)
    gw_ref[...] = jnp.concatenate([w1, w2, jnp.zeros((GATE_ROWS - MOE_TOP_K, tm), F32)], axis=0)


def _outproj_call(a, w, res_lat, lat_off, res_ctx, gt, sh, sc, g, wr, n_ctx, tm):
    b_, tn, d = a.shape
    ncb = n_ctx // tm
    nt = tn // tm
    lob = lat_off // tm
    split = res_ctx is not None
    tok = pl.BlockSpec((None, tm, d), lambda b, j: (b, j, 0))
    pair = pl.BlockSpec((None, None, MOE_TOP_K, tm), lambda b, j: (b, j, 0, 0))
    res_specs, res_args = [], []
    if split:
        res_specs.append(pl.BlockSpec((None, tm, d), lambda b, j: (b, jnp.minimum(j, ncb - 1), 0)))
        res_args.append(res_ctx)
        res_specs.append(pl.BlockSpec((None, tm, d), lambda b, j: (b, jnp.maximum(j - ncb, 0) + lob, 0)))
    else:
        res_specs.append(pl.BlockSpec((None, tm, d), lambda b, j: (b, j + lob, 0)))
    res_args.append(res_lat)
    return pl.pallas_call(
        functools.partial(_outproj_kernel, n_ctx_blocks=ncb, split_residual=split),
        grid=(b_, nt),
        in_specs=[tok, _resident(w.shape)] + res_specs + [
            _mod_spec(d, ncb, b_), _mod_spec(d, ncb, b_), _mod_spec(d, ncb, b_),
            _resident((1, d)), _resident(wr.shape)],
        out_specs=[tok, tok, pair,
                   pl.BlockSpec((None, None, GATE_ROWS, tm), lambda b, j: (b, j, 0, 0)), pair,
                   pl.BlockSpec((ROUTER_ROWS, LANES), lambda b, j: (0, 0))],
        out_shape=[jax.ShapeDtypeStruct((b_, tn, d), F32),
                   jax.ShapeDtypeStruct((b_, tn, d), BF16),
                   jax.ShapeDtypeStruct((b_, nt, MOE_TOP_K, tm), jnp.int32),
                   jax.ShapeDtypeStruct((b_, nt, GATE_ROWS, tm), F32),
                   jax.ShapeDtypeStruct((b_, nt, MOE_TOP_K, tm), jnp.int32),
                   jax.ShapeDtypeStruct((ROUTER_ROWS, LANES), F32)],
        scratch_shapes=[pltpu.VMEM((ROUTER_ROWS, LANES), F32)],
        compiler_params=_cparams(("arbitrary", "arbitrary")),
        name="outproj_ffnmod",
    )(a, w, *res_args, gt, sh, sc, g, wr)


def _expert_kernel(be_ref, nu_ref, x_ref, wg_ref, wu_ref, wd_ref, y_ref, wgb_ref, wub_ref, wdb_ref):
    i = pl.program_id(0)
    new_expert = (i == 0) | (be_ref[i] != be_ref[jnp.maximum(i - 1, 0)])

    @pl.when(new_expert)
    def _():
        wgb_ref[...] = wg_ref[...].astype(BF16)
        wub_ref[...] = wu_ref[...].astype(BF16)
        wdb_ref[...] = wd_ref[...].astype(BF16)

    @pl.when(i < nu_ref[0])
    def _():
        rows = x_ref.shape[0] // 2
        halves = [slice(j * rows, (j + 1) * rows) for j in range(2)]
        gates = [_dot(x_ref[r, :], wgb_ref[...]) for r in halves]
        ups = [_dot(x_ref[r, :], wub_ref[...]) for r in halves]
        for j, r in enumerate(halves):
            act = (gates[j] * _sigmoid(gates[j]) * ups[j]).astype(BF16)
            y_ref[r, :] = _dot(act, wdb_ref[...]).astype(y_ref.dtype)

    @pl.when(i >= nu_ref[0])
    def _():
        y_ref[...] = jnp.zeros_like(y_ref)


def _expert_call(buf, block_expert, n_used, wg, wu, wd, layer):
    r_pad, d = buf.shape
    n_blocks = r_pad // MOE_BLOCK
    ff = wg.shape[-1]

    def x_idx(i, be, nu):
        return (jnp.maximum(jnp.minimum(i, nu[0] - 1), 0), 0)

    grid_spec = pltpu.PrefetchScalarGridSpec(
        num_scalar_prefetch=2,
        grid=(n_blocks,),
        in_specs=[
            pl.BlockSpec((MOE_BLOCK, d), x_idx),
            pl.BlockSpec((None, None, d, ff), lambda i, be, nu: (layer, be[i], 0, 0)),
            pl.BlockSpec((None, None, d, ff), lambda i, be, nu: (layer, be[i], 0, 0)),
            pl.BlockSpec((None, None, ff, d), lambda i, be, nu: (layer, be[i], 0, 0)),
        ],
        out_specs=pl.BlockSpec((MOE_BLOCK, d), lambda i, be, nu: (i, 0)),
        scratch_shapes=[pltpu.VMEM((d, ff), BF16), pltpu.VMEM((d, ff), BF16),
                        pltpu.VMEM((ff, d), BF16)],
    )
    return pl.pallas_call(
        _expert_kernel,
        grid_spec=grid_spec,
        out_shape=jax.ShapeDtypeStruct((r_pad, d), BF16),
        compiler_params=_cparams(("arbitrary",)),
        name="moe_experts",
    )(block_expert, n_used, buf, wg, wu, wd)


def _combine_kernel(*refs, with_next):
    if with_next:
        y0_ref, y1_ref, gw_ref, x_ref, gt_ref, sh_ref, sc_ref, g_ref, xo_ref, hn_ref = refs
    else:
        y0_ref, y1_ref, gw_ref, x_ref, gt_ref, xo_ref = refs
    gw = gw_ref[...].T
    f = gw[:, 0:1] * y0_ref[...].astype(F32) + gw[:, 1:2] * y1_ref[...].astype(F32)
    xo = x_ref[...] + gt_ref[...] * f
    xo_ref[...] = xo
    if with_next:
        h = _rms(xo) * g_ref[...]
        hn_ref[...] = (h * (1.0 + sc_ref[...]) + sh_ref[...]).astype(BF16)


def _combine_call(y0, y1, gw, x, gt, n_ctx, tm, next_mod=None):
    b_, tn, d = x.shape
    ncb = n_ctx // tm
    tok = pl.BlockSpec((None, tm, d), lambda b, j: (b, j, 0))
    in_specs = [tok, tok, pl.BlockSpec((None, None, GATE_ROWS, tm), lambda b, j: (b, j, 0, 0)),
                tok, _mod_spec(d, ncb, b_)]
    args = [y0, y1, gw, x, gt]
    out_specs = [tok]
    out_shape = [jax.ShapeDtypeStruct((b_, tn, d), F32)]
    with_next = next_mod is not None
    if with_next:
        sh, sc, g = next_mod
        in_specs += [_mod_spec(d, ncb, b_), _mod_spec(d, ncb, b_), _resident((1, d))]
        args += [sh, sc, g]
        out_specs.append(tok)
        out_shape.append(jax.ShapeDtypeStruct((b_, tn, d), BF16))
    return pl.pallas_call(
        functools.partial(_combine_kernel, with_next=with_next),
        grid=(b_, tn // tm),
        in_specs=in_specs,
        out_specs=out_specs,
        out_shape=out_shape,
        compiler_params=_cparams(("parallel", "arbitrary")),
        name="moe_combine",
    )(*args)


def _mla_in_kernel(h_ref, win_ref, qg_ref, wq_ref, wqr_ref, tqc_ref, tqs_ref,
                   ktab_ref, wk_ref, wv_ref, kgain_ref, q_ref, k_ref, v_ref, *, n_ctx_blocks):
    tm = h_ref.shape[0]
    proj = _dot(h_ref[...], win_ref[...])

    @pl.when(pl.program_id(1) >= n_ctx_blocks)
    def _():
        cq = proj[:, :MLA_Q_LORA]
        qn = (_rms(cq) * qg_ref[...]).astype(BF16)
        qa = _dot(qn, wq_ref[...])
        qr = _dot(qn, wqr_ref[...])
        tqc = tqc_ref[...]
        tqs = tqs_ref[...]
        for h in range(MLA_HEADS):
            hs = slice(h * LANES, (h + 1) * LANES)
            a = qa[:, hs]
            rq = lax.rsqrt(jnp.sum(a * a, axis=-1, keepdims=True) * (1.0 / MLA_QK) + NORM_EPS)
            q_ref[h] = ((a * tqc + qr[:, hs] * tqs) * rq).astype(BF16)

    p2 = proj[:, MLA_Q_LORA:]
    lane = lax.broadcasted_iota(jnp.int32, (tm, p2.shape[1]), 1)
    is_kv = lane < MLA_KV_LORA
    sq = p2 * p2
    ms_kv = jnp.sum(jnp.where(is_kv, sq, 0.0), axis=-1, keepdims=True) * (1.0 / MLA_KV_LORA)
    is_rope = (lane >= MLA_KV_LORA) & (lane < MLA_KV_LORA + MLA_ROPE)
    ss_rope = jnp.sum(jnp.where(is_rope, sq, 0.0), axis=-1, keepdims=True)
    mult = ktab_ref[...] * jnp.where(is_kv, lax.rsqrt(ms_kv + NORM_EPS), 1.0)
    lhs = (p2 * mult).astype(BF16)
    ka = _dot(lhs, wk_ref[...])
    va = _dot(lhs[:, :MLA_KV_LORA], wv_ref[...])
    kgain = kgain_ref[...]
    lane1 = lax.broadcasted_iota(jnp.int32, (tm, LANES), 1)
    for h in range(MLA_HEADS):
        hs = slice(h * LANES, (h + 1) * LANES)
        a = ka[:, hs]
        ssn = jnp.sum(jnp.where(lane1 < MLA_NOPE, a * a, 0.0), axis=-1, keepdims=True)
        rk = lax.rsqrt((ssn + ss_rope) * (1.0 / MLA_QK) + NORM_EPS)
        k_ref[h] = (a * kgain * rk).astype(BF16)
        ones_lane = MLA_V if h % 2 == 0 else 0
        v_ref[h] = jnp.where(lane1 == ones_lane, 1.0, va[:, hs]).astype(BF16)


def _mla_in_call(hm, win, qg, wq, wqr, tqc, tqs, ktab, wk, wv, kgain, n_ctx, tm):
    b_, t_, d = hm.shape
    ncb = n_ctx // tm
    tok = pl.BlockSpec((None, tm, d), lambda b, j: (b, j, 0))
    head = pl.BlockSpec((None, MLA_HEADS, tm, LANES), lambda b, j: (b, 0, j, 0))
    hshape = jax.ShapeDtypeStruct((b_, MLA_HEADS, t_, LANES), BF16)
    qhead = pl.BlockSpec((None, MLA_HEADS, tm, LANES), lambda b, j: (b, 0, jnp.maximum(j - ncb, 0), 0))
    qshape = jax.ShapeDtypeStruct((b_, MLA_HEADS, t_ - n_ctx, LANES), BF16)
    return pl.pallas_call(
        functools.partial(_mla_in_kernel, n_ctx_blocks=ncb),
        grid=(b_, t_ // tm),
        in_specs=[tok, _resident(win.shape), _resident(qg.shape), _resident(wq.shape),
                  _resident(wqr.shape),
                  pl.BlockSpec((tm, LANES), lambda b, j: (j, 0)),
                  pl.BlockSpec((tm, LANES), lambda b, j: (j, 0)),
                  pl.BlockSpec((tm, ktab.shape[1]), lambda b, j: (j, 0)),
                  _resident(wk.shape), _resident(wv.shape), _resident(kgain.shape)],
        out_specs=[qhead, head, head],
        out_shape=[qshape, hshape, hshape],
        compiler_params=_cparams(("parallel", "arbitrary")),
        name="mla_in",
    )(hm, win, qg, wq, wqr, tqc, tqs, ktab, wk, wv, kgain)


def _attn_kernel(q_ref, k_ref, v_ref, o_ref):
    n_heads, tq, _ = q_ref.shape
    t_ = k_ref.shape[1]
    ck = next(c for c in ATT_KEY_CHUNKS if t_ % c == 0)
    outs = []
    for hh in range(n_heads):
        q = q_ref[hh]
        m = jnp.full((tq, 1), -jnp.inf, F32)
        acc = jnp.zeros((tq, LANES), F32)
        for c in range(t_ // ck):
            ks = slice(c * ck, (c + 1) * ck)
            s = _dot_nt(q, k_ref[hh, ks, :])
            m_new = jnp.maximum(m, jnp.max(s, axis=-1, keepdims=True))
            p = jnp.exp2((s - m_new).astype(BF16))
            acc = acc * jnp.exp2(m - m_new) + _dot(p, v_ref[hh, ks, :])
            m = m_new
        ones_lane = MLA_V if hh % 2 == 0 else 0
        outs.append(acc / acc[:, ones_lane:ones_lane + 1])
    lane = lax.broadcasted_iota(jnp.int32, outs[0].shape, 1)
    for pr in range(len(outs) // 2):
        o_ref[:, pr * LANES:(pr + 1) * LANES] = jnp.where(
            lane < MLA_V, outs[2 * pr], outs[2 * pr + 1]).astype(o_ref.dtype)


def _attn_call(q, k, v, tq):
    b_, nh, l_, _ = q.shape
    t_ = k.shape[2]
    hps = ATT_HEADS_PER_STEP
    return pl.pallas_call(
        _attn_kernel,
        grid=(b_, nh // hps, l_ // tq),
        in_specs=[
            pl.BlockSpec((None, hps, tq, LANES), lambda b, h, i: (b, h, i, 0)),
            pl.BlockSpec((None, hps, t_, LANES), lambda b, h, i: (b, h, 0, 0)),
            pl.BlockSpec((None, hps, t_, LANES), lambda b, h, i: (b, h, 0, 0)),
        ],
        out_specs=pl.BlockSpec((None, tq, (hps // 2) * LANES), lambda b, h, i: (b, i, h)),
        out_shape=jax.ShapeDtypeStruct((b_, l_, (nh // 2) * LANES), BF16),
        compiler_params=_cparams(("parallel", "parallel", "arbitrary")),
        name="mla_attention",
    )(q, k, v)


def _dispatch_layout(eid, rank, counts):
    tm = eid.shape[-1]
    n_slot = eid.size
    starts = jnp.cumsum(counts) - counts
    padded = (counts + MOE_BLOCK - 1) // MOE_BLOCK * MOE_BLOCK
    pad_ends = jnp.cumsum(padded)
    pad_starts = pad_ends - padded
    n_blocks = (n_slot + MOE_BLOCK - 1) // MOE_BLOCK + MOE_EXPERTS
    block_start = jnp.arange(n_blocks, dtype=jnp.int32) * MOE_BLOCK
    block_expert = jnp.minimum(
        jnp.sum((block_start[:, None] >= pad_ends[None, :]).astype(jnp.int32), axis=1),
        MOE_EXPERTS - 1)
    n_used = pad_ends[-1:] // MOE_BLOCK
    slot_row = rank
    for e in range(MOE_EXPERTS):
        slot_row = slot_row + jnp.where(eid == e, pad_starts[e], 0)
    order = jnp.argsort(eid.reshape(n_slot)).astype(jnp.int32)
    offs = (block_start - pad_starts[block_expert])[:, None] + jnp.arange(MOE_BLOCK, dtype=jnp.int32)
    valid = offs < counts[block_expert][:, None]
    pos = jnp.minimum(starts[block_expert][:, None] + offs, n_slot - 1)
    slot = jnp.take(order, pos, mode='clip')
    tok = (slot // (MOE_TOP_K * tm)) * tm + slot % tm
    n_tok = n_slot // MOE_TOP_K
    filler = (block_start[:, None] + jnp.arange(MOE_BLOCK, dtype=jnp.int32)) % n_tok
    row_tok = jnp.where(valid, tok, filler)
    return row_tok.reshape(-1), slot_row, block_expert, n_used


def _moe(h2, eid, rank, counts, wg, wu, wd, layer):
    b_, tn, d = h2.shape
    n_tok = b_ * tn
    cnt = counts[MOE_GROUPS:MOE_GROUPS + MOE_EXPERTS, 0].astype(jnp.int32)
    row_tok, slot_row, block_expert, n_used = _dispatch_layout(eid, rank, cnt)
    buf = jnp.take(h2.reshape(n_tok, d), row_tok, axis=0, mode='clip')
    ybuf = _expert_call(buf, block_expert, n_used, wg, wu, wd, layer)
    y0 = jnp.take(ybuf, slot_row[:, :, 0, :].reshape(n_tok), axis=0, mode='clip').reshape(b_, tn, d)
    y1 = jnp.take(ybuf, slot_row[:, :, 1, :].reshape(n_tok), axis=0, mode='clip').reshape(b_, tn, d)
    return y0, y1


def _rot_half_perm():
    half = MLA_ROPE // 2
    j = jnp.arange(MLA_ROPE)
    within = j % half
    base = j - within
    src = jnp.where(within < half // 2, base + within + half // 2, base + within - half // 2)
    sign = jnp.where(within < half // 2, -1.0, 1.0).astype(F32)
    return src, sign


def _rope_tables(n_ctx, seq):
    rows = seq // GRID_W
    row = jnp.repeat(jnp.arange(rows), GRID_W)
    col = jnp.tile(jnp.arange(GRID_W), rows)
    half = MLA_ROPE // 2
    inv_freq = ROPE_THETA ** (-jnp.arange(0, half, 2, dtype=F32) / half)
    ang = jnp.stack([row, col], axis=-1).astype(F32)[..., None] * inv_freq
    ang = jnp.concatenate([ang, ang], axis=-1).reshape(seq, MLA_ROPE)
    cos = jnp.concatenate([jnp.ones((n_ctx, MLA_ROPE), F32), jnp.cos(ang)], axis=0)
    sin = jnp.concatenate([jnp.zeros((n_ctx, MLA_ROPE), F32), jnp.sin(ang)], axis=0)
    return cos, sin


def _mla_prepare(w_in, w_qb, w_kvb, q_qk_g, k_qk_g, kv_norm_g, n_ctx, seq):
    d = w_in.shape[0]
    src, sign = _rot_half_perm()
    cos, sin = _rope_tables(n_ctx, seq)
    t_ = n_ctx + seq

    rope0 = MLA_Q_LORA + MLA_KV_LORA
    w_rope = w_in[:, rope0:rope0 + MLA_ROPE]
    win = jnp.concatenate(
        [w_in, w_rope[:, src] * sign, jnp.zeros((d, 512 - rope0 - 2 * MLA_ROPE), F32)], axis=1)

    wq3 = w_qb.reshape(MLA_Q_LORA, MLA_HEADS, MLA_QK)
    pad = jnp.zeros((MLA_Q_LORA, MLA_HEADS, LANES - MLA_QK), F32)
    wq = jnp.concatenate([wq3, pad], axis=-1).reshape(MLA_Q_LORA, MLA_HEADS * LANES)
    wq_rot = wq3[:, :, MLA_NOPE:][:, :, src] * sign
    wqr = jnp.concatenate([jnp.zeros((MLA_Q_LORA, MLA_HEADS, MLA_NOPE), F32), wq_rot, pad],
                          axis=-1).reshape(MLA_Q_LORA, MLA_HEADS * LANES)

    scale = MLA_QK ** -0.5 * 1.4426950408889634
    gq_n, gq_r = q_qk_g[:MLA_NOPE], q_qk_g[MLA_NOPE:]
    zpad = jnp.zeros((t_, LANES - MLA_QK), F32)
    tqc = jnp.concatenate([jnp.broadcast_to(gq_n, (t_, MLA_NOPE)), gq_r * cos, zpad], axis=1) * scale
    tqs = jnp.concatenate([jnp.zeros((t_, MLA_NOPE), F32), gq_r[src] * sin, zpad], axis=1) * scale

    wkv3 = w_kvb.reshape(MLA_KV_LORA, MLA_HEADS, MLA_NOPE + MLA_V)
    wk_lat = jnp.concatenate(
        [wkv3[:, :, :MLA_NOPE], jnp.zeros((MLA_KV_LORA, MLA_HEADS, LANES - MLA_NOPE), F32)], axis=-1)
    place = jnp.concatenate([jnp.zeros((MLA_ROPE, MLA_NOPE), F32), jnp.eye(MLA_ROPE, dtype=F32),
                             jnp.zeros((MLA_ROPE, LANES - MLA_QK), F32)], axis=1)
    place = jnp.broadcast_to(place[:, None, :], (MLA_ROPE, MLA_HEADS, LANES))
    wk = jnp.concatenate([wk_lat, place, place,
                          jnp.zeros((256 - MLA_KV_LORA - 2 * MLA_ROPE, MLA_HEADS, LANES), F32)],
                         axis=0).reshape(256, MLA_HEADS * LANES)
    gk_n, gk_r = k_qk_g[:MLA_NOPE], k_qk_g[MLA_NOPE:]
    ktab = jnp.concatenate([jnp.broadcast_to(kv_norm_g, (t_, MLA_KV_LORA)), gk_r * cos,
                            gk_r[src] * sin, jnp.zeros((t_, 256 - MLA_KV_LORA - 2 * MLA_ROPE), F32)],
                           axis=1)
    kgain = jnp.concatenate([gk_n, jnp.ones((MLA_ROPE,), F32),
                             jnp.zeros((LANES - MLA_QK,), F32)]).reshape(1, LANES)

    wv_h = wkv3[:, :, MLA_NOPE:]
    zv = jnp.zeros_like(wv_h)
    odd = (jnp.arange(MLA_HEADS) % 2 == 1)[None, :, None]
    wv = jnp.concatenate([jnp.where(odd, zv, wv_h), jnp.where(odd, wv_h, zv)],
                         axis=-1).reshape(MLA_KV_LORA, MLA_HEADS * LANES)
    return (win.astype(BF16), wq.astype(BF16), wqr.astype(BF16), tqc, tqs, ktab,
            wk.astype(BF16), wv.astype(BF16), kgain)


def _router_weights(w_group, w_expert):
    d = w_group.shape[0]
    wt = jnp.concatenate([w_group, w_expert], axis=1).T
    wt = jnp.concatenate([wt, jnp.zeros((ROUTER_ROWS - wt.shape[0], d), F32)], axis=0)
    head = wt.astype(BF16)
    rest = (wt - head.astype(F32)).astype(BF16)
    return jnp.concatenate([head, rest], axis=0)


def kernel(x, c, ctx, c_ctx, ada_w, ada_b, norm_mix_g, norm_ffn_g, hg_w_in, hg_lower_bounds, hg_out_norm_g, hg_w_out, mla_w_in, mla_q_norm_g, mla_kv_norm_g, mla_w_qb, mla_w_kvb, mla_q_qknorm_g, mla_k_qknorm_g, mla_w_out, moe_w_group, moe_w_expert, moe_w_gate, moe_w_up, moe_w_down):
    b_, seq, d = x.shape
    n_ctx = ctx.shape[1]
    tm = min(ROW_TILE, n_ctx)
    tq = min(ATT_Q_TILE, seq)
    assert n_ctx % GLA_BLOCK == 0 and seq % GLA_BLOCK == 0 and n_ctx % tm == 0 and seq % tm == 0
    assert b_ + 1 <= ADA_ROWS and seq % GRID_W == 0

    cc = jnp.concatenate([c, c_ctx[None, :], jnp.zeros((ADA_ROWS - b_ - 1, d), F32)], axis=0)
    mods = _ada_call(cc, ada_w, ada_b).reshape(DEPTH, ADA_CHUNKS, ADA_ROWS, 1, d)

    def mod(i, chunk):
        return mods[i, chunk]

    row = lambda v: v.reshape(1, -1)

    lower = jnp.cumsum(jax.nn.softmax(hg_lower_bounds.astype(F32), axis=1), axis=1)[:, 0]
    qs, kf, gf, kb, gb, v, gs = _hg_in_call(
        ctx, x, mod(0, 0), mod(0, 1), row(norm_mix_g[0]), hg_w_in[0].astype(BF16), lower, tm)
    o_b = _gla_call(qs, kb, gb, v, n_ctx, reverse=True)
    og = _gla_call(qs, kf, gf, v, n_ctx, reverse=False,
                   final_args=(o_b, gs, row(hg_out_norm_g[0])))
    x1, h2, eid, gw, rank, cnt = _outproj_call(
        og, hg_w_out[0].astype(BF16), x, 0, ctx, mod(0, 2), mod(0, 3), mod(0, 4),
        row(norm_ffn_g[0]), _router_weights(moe_w_group[0], moe_w_expert[0]), n_ctx, tm)
    y0, y1 = _moe(h2, eid, rank, cnt, moe_w_gate, moe_w_up, moe_w_down, 0)
    x2, hm = _combine_call(y0, y1, gw, x1, mod(0, 5), n_ctx, tm,
                           next_mod=(mod(1, 0), mod(1, 1), row(norm_mix_g[1])))

    win, wq, wqr, tqc, tqs, ktab, wk, wv, kgain = _mla_prepare(
        mla_w_in[0], mla_w_qb[0], mla_w_kvb[0], mla_q_qknorm_g[0], mla_k_qknorm_g[0],
        mla_kv_norm_g[0], n_ctx, seq)
    q, k, vv = _mla_in_call(hm, win, row(mla_q_norm_g[0]), wq, wqr, tqc, tqs, ktab, wk, wv, kgain,
                            n_ctx, tm)
    o_att = _attn_call(q, k, vv, tq)
    x3, h2, eid, gw, rank, cnt = _outproj_call(
        o_att, mla_w_out[0].astype(BF16), x2, n_ctx, None, mod(1, 2), mod(1, 3), mod(1, 4),
        row(norm_ffn_g[1]), _router_weights(moe_w_group[1], moe_w_expert[1]), 0, tm)
    y0, y1 = _moe(h2, eid, rank, cnt, moe_w_gate, moe_w_up, moe_w_down, 1)
    (x4,) = _combine_call(y0, y1, gw, x3, mod(1, 5), 0, tm)
    return x4
```

```python
import functools

import jax
import jax.numpy as jnp
from jax import lax
from jax.experimental import pallas as pl
from jax.experimental.pallas import tpu as pltpu

F32 = jnp.float32
BF16 = jnp.bfloat16

DEPTH = 2
ADA_CHUNKS = 6
NORM_EPS = 1e-6
GRID_W = 64
HG_HEADS = 8
HG_CHUNK = 32
MLA_HEADS = 16
MLA_Q_LORA = 256
MLA_KV_LORA = 128
MLA_NOPE = 64
MLA_ROPE = 32
MLA_V = 64
MLA_QK = MLA_NOPE + MLA_ROPE
ROPE_THETA = 10000.0
MOE_GROUPS = 4
MOE_EPG = 8
MOE_EXPERTS = MOE_GROUPS * MOE_EPG
MOE_TOP_K = 2
MOE_FF = 512

LANES = 128
ROUTER_ROWS = 48
GATE_ROWS = 8
VMEM_LIMIT = 56 * 1024 * 1024

ROW_TILE = 256
GLA_BLOCK = 256
MOE_BLOCK = 512
ATT_Q_TILE = 512
ATT_HEADS_PER_STEP = 8
ATT_KEY_CHUNKS = (256, 128)
EXPERT_PARTS = 4
COMBINE_PARTS = 2
ADA_ROWS = 40


def _cparams(sem):
    return pltpu.CompilerParams(dimension_semantics=sem, vmem_limit_bytes=VMEM_LIMIT)


def _resident(shape):
    nd = len(shape)
    return pl.BlockSpec(shape, lambda *_: (0,) * nd, pipeline_mode=pl.Buffered(1))


def _dot(a, b):
    return jnp.dot(a, b, preferred_element_type=F32)


def _dot_nt(a, b):
    return lax.dot_general(a, b, (((1,), (1,)), ((), ())), preferred_element_type=F32)


def _dot_tn(a, b):
    return lax.dot_general(a, b, (((0,), (0,)), ((), ())), preferred_element_type=F32)


def _sigmoid(x):
    return 1.0 / (1.0 + jnp.exp(-x))


def _rms(x):
    return x * lax.rsqrt(jnp.mean(x * x, axis=-1, keepdims=True) + NORM_EPS)


def _ada_kernel(c_ref, w_ref, b_ref, o_ref):
    c = c_ref[...]
    a = (c * _sigmoid(c)).astype(BF16)
    o_ref[...] = _dot(a, w_ref[...].astype(BF16)) + b_ref[...]


def _ada_call(cc, ada_w, ada_b):
    depth, d, _ = ada_w.shape
    rows = cc.shape[0]
    return pl.pallas_call(
        _ada_kernel,
        grid=(depth, ADA_CHUNKS),
        in_specs=[
            pl.BlockSpec((rows, d), lambda i, j: (0, 0)),
            pl.BlockSpec((None, d, d), lambda i, j: (i, 0, j)),
            pl.BlockSpec((None, 1, d), lambda i, j: (i, 0, j)),
        ],
        out_specs=pl.BlockSpec((None, None, rows, d), lambda i, j: (i, j, 0, 0)),
        out_shape=jax.ShapeDtypeStruct((depth, ADA_CHUNKS, rows, d), F32),
        compiler_params=_cparams(("arbitrary", "arbitrary")),
        name="ada_mod",
    )(cc, ada_w, ada_b.reshape(depth, 1, ADA_CHUNKS * d))


def _mod_spec(d, n_ctx_blocks, ctx_row, b_off=0):
    def idx(b, j):
        return (jnp.where(j < n_ctx_blocks, ctx_row, b + b_off), 0, 0)
    return pl.BlockSpec((None, 1, d), idx)


def _hg_in_kernel(xc_ref, x_ref, sh_ref, sc_ref, g_ref, w_ref, lb_ref,
                  q_ref, kf_ref, gf_ref, kb_ref, gb_ref, v_ref, gs_ref, *, n_ctx_blocks):
    d = x_ref.shape[-1]
    x = jnp.where(pl.program_id(1) < n_ctx_blocks, xc_ref[...], x_ref[...])
    h = _rms(x) * g_ref[...]
    h = (h * (1.0 + sc_ref[...]) + sh_ref[...]).astype(BF16)

    def proj(c):
        return _dot(h, w_ref[:, c * d:(c + 1) * d])

    p = proj(0)
    q_ref[...] = (p * _sigmoid(p)).astype(BF16)
    for c, k_ref, lg_ref in ((1, kf_ref, gf_ref), (2, kb_ref, gb_ref)):
        s = _sigmoid(proj(c))
        lb = lb_ref[c - 1:c, :]
        k_ref[...] = ((1.0 - lb) * (1.0 - s)).astype(BF16)
        lg_ref[...] = jnp.log(lb + (1.0 - lb) * s).astype(BF16)
    v_ref[...] = proj(3).astype(BF16)
    p = proj(4)
    gs_ref[...] = (p * _sigmoid(p)).astype(BF16)


def _hg_in_call(ctx, x, sh, sc, g, w_in, lb, tm):
    b_, n_ctx, d = ctx.shape
    t_ = n_ctx + x.shape[1]
    ncb = n_ctx // tm
    tok = pl.BlockSpec((None, tm, d), lambda b, j: (b, j, 0))
    out = jax.ShapeDtypeStruct((b_, t_, d), BF16)
    return pl.pallas_call(
        functools.partial(_hg_in_kernel, n_ctx_blocks=ncb),
        grid=(b_, t_ // tm),
        in_specs=[pl.BlockSpec((None, tm, d), lambda b, j: (b, jnp.minimum(j, ncb - 1), 0)),
                  pl.BlockSpec((None, tm, d), lambda b, j: (b, jnp.maximum(j - ncb, 0), 0)),
                  _mod_spec(d, ncb, b_), _mod_spec(d, ncb, b_),
                  _resident((1, d)), _resident(w_in.shape), _resident((2, d))],
        out_specs=[tok] * 7,
        out_shape=[out] * 7,
        compiler_params=_cparams(("parallel", "arbitrary")),
        name="hg_in",
    )(ctx, x, sh, sc, g, w_in, lb)


def _gla_kernel(*refs, reverse, final, n_heads):
    if final:
        q_ref, k_ref, g_ref, v_ref, ob_ref, gs_ref, gn_ref, o_ref, st_ref = refs
    else:
        q_ref, k_ref, g_ref, v_ref, o_ref, st_ref = refs
    tb, d = q_ref.shape
    dh = d // n_heads
    n_chunks = tb // HG_CHUNK

    @pl.when(pl.program_id(1) == 0)
    def _():
        st_ref[...] = jnp.zeros_like(st_ref)

    row = lax.broadcasted_iota(jnp.int32, (tb, tb), 0)
    col = lax.broadcasted_iota(jnp.int32, (tb, tb), 1)
    same_chunk = (row // HG_CHUNK) == (col // HG_CHUNK)
    tri = same_chunk & ((col >= row) if reverse else (col <= row))
    cum = jnp.where(tri, 1.0, 0.0).astype(BF16)

    b = _dot(cum, g_ref[...])
    qd = (q_ref[...].astype(F32) * jnp.exp(b)).astype(BF16)
    ki = k_ref[...].astype(F32) * jnp.exp(-b)
    ki_b = ki.astype(BF16)

    chunk_order = range(n_chunks - 1, -1, -1) if reverse else range(n_chunks)
    e_end, k_end = {}, {}
    for c in range(n_chunks):
        r0 = c * HG_CHUNK
        last = r0 if reverse else r0 + HG_CHUNK - 1
        e_end[c] = jnp.exp(b[last:last + 1, :])
        k_end[c] = (ki[r0:r0 + HG_CHUNK, :] * e_end[c]).astype(BF16)

    heads = range(n_heads)
    hsl = [slice(h * dh, (h + 1) * dh) for h in heads]
    scores = [jnp.where(tri, _dot_nt(qd[:, hsl[h]], ki_b[:, hsl[h]]), 0.0).astype(BF16) for h in heads]
    o_intra = [_dot(scores[h], v_ref[:, hsl[h]]) for h in heads]
    kv = {(h, c): _dot_tn(v_ref[c * HG_CHUNK:(c + 1) * HG_CHUNK, hsl[h]], k_end[c][:, hsl[h]])
          for h in heads for c in range(n_chunks)}
    st = [st_ref[h] for h in heads]
    for c in chunk_order:
        rs = slice(c * HG_CHUNK, (c + 1) * HG_CHUNK)
        for h in heads:
            hs = hsl[h]
            o_c = o_intra[h][rs, :] + _dot_nt(qd[rs, hs], st[h].astype(BF16))
            st[h] = st[h] * e_end[c][:, hs] + kv[(h, c)]
            if final:
                o_c = o_c + ob_ref[rs, hs].astype(F32)
                o_c = _rms(o_c) * gn_ref[...]
                o_c = o_c * gs_ref[rs, hs].astype(F32)
            o_ref[rs, hs] = o_c.astype(o_ref.dtype)
    for h in heads:
        st_ref[h] = st[h]


def _gla_call(q, k, g, v, n_ctx, reverse, final_args=None):
    b_, t_, d = q.shape
    tb = GLA_BLOCK
    nb = t_ // tb
    ncb = n_ctx // tb
    dh = d // HG_HEADS

    def blk(b, j):
        if reverse:
            jj = jnp.where(j < ncb, ncb - 1 - j, nb - 1 - (j - ncb))
        else:
            jj = j
        return (b, jj, 0)

    tok = pl.BlockSpec((None, tb, d), blk)
    in_specs = [tok] * 4
    args = [q, k, g, v]
    final = final_args is not None
    if final:
        o_b, gs, gn = final_args
        in_specs += [tok, tok, _resident((1, dh))]
        args += [o_b, gs, gn]
    return pl.pallas_call(
        functools.partial(_gla_kernel, reverse=reverse, final=final, n_heads=HG_HEADS),
        grid=(b_, nb),
        in_specs=in_specs,
        out_specs=tok,
        out_shape=jax.ShapeDtypeStruct((b_, t_, d), BF16),
        scratch_shapes=[pltpu.VMEM((HG_HEADS, dh, dh), F32)],
        compiler_params=_cparams(("parallel", "arbitrary")),
        name="gla_bwd" if reverse else "gla_fwd",
    )(*args)


def _outproj_kernel(*refs, n_ctx_blocks, split_residual):
    if split_residual:
        (a_ref, w_ref, xc_ref, x_ref, gt_ref, sh_ref, sc_ref, g_ref, wr_ref,
         xn_ref, h2_ref, eid_ref, gw_ref, rank_ref, cnt_ref, run_ref) = refs
    else:
        (a_ref, w_ref, x_ref, gt_ref, sh_ref, sc_ref, g_ref, wr_ref,
         xn_ref, h2_ref, eid_ref, gw_ref, rank_ref, cnt_ref, run_ref) = refs
    tm = a_ref.shape[0]
    nr = ROUTER_ROWS
    first = (pl.program_id(0) == 0) & (pl.program_id(1) == 0)

    @pl.when(first)
    def _():
        run_ref[...] = jnp.zeros_like(run_ref)

    x = x_ref[...]
    if split_residual:
        x = jnp.where(pl.program_id(1) < n_ctx_blocks, xc_ref[...], x)
    y = _dot(a_ref[...], w_ref[...])
    xn = x + gt_ref[...] * y
    xn_ref[...] = xn
    h2 = _rms(xn) * g_ref[...]
    h2 = h2 * (1.0 + sc_ref[...]) + sh_ref[...]
    h2b = h2.astype(BF16)
    h2_ref[...] = h2b
    h2r = (h2 - h2b.astype(F32)).astype(BF16)
    lg2 = _dot_nt(wr_ref[...], h2b) + _dot_nt(wr_ref[...], h2r)
    lg = lg2[:nr] + lg2[nr:]

    ridx = lax.broadcasted_iota(jnp.int32, lg.shape, 0)
    neg = -jnp.inf
    gl = jnp.where(ridx < MOE_GROUPS, lg, neg)
    gmax = jnp.max(gl, axis=0, keepdims=True)
    g_idx = jnp.min(jnp.where(gl == gmax, ridx, nr), axis=0, keepdims=True)
    p_group = 1.0 / jnp.sum(jnp.exp(gl - gmax), axis=0, keepdims=True)
    lo = MOE_GROUPS + MOE_EPG * g_idx
    el = jnp.where((ridx >= lo) & (ridx < lo + MOE_EPG), lg, neg)
    m1 = jnp.max(el, axis=0, keepdims=True)
    i1 = jnp.min(jnp.where(el == m1, ridx, nr), axis=0, keepdims=True)
    el2 = jnp.where(ridx == i1, neg, el)
    m2 = jnp.max(el2, axis=0, keepdims=True)
    i2 = jnp.min(jnp.where(el2 == m2, ridx, nr), axis=0, keepdims=True)
    r21 = jnp.exp(m2 - m1)
    w1 = p_group / (1.0 + r21)
    w2 = w1 * r21

    hot1 = ridx == i1
    hot2 = ridx == i2
    f1 = jnp.where(hot1, 1.0, 0.0)
    f2 = jnp.where(hot2, 1.0, 0.0)
    r = lax.broadcasted_iota(jnp.int32, (tm, tm), 0)
    c = lax.broadcasted_iota(jnp.int32, (tm, tm), 1)
    earlier = jnp.where(r < c, 1.0, 0.0).astype(BF16)
    pre = _dot(jnp.concatenate([f1, f2], axis=0).astype(BF16), earlier)
    run = run_ref[:, 0:1]
    n1 = jnp.sum(f1, axis=1, keepdims=True)
    n2 = jnp.sum(f2, axis=1, keepdims=True)
    rank1 = jnp.sum(jnp.where(hot1, pre[:nr] + run, 0.0), axis=0, keepdims=True)
    rank2 = jnp.sum(jnp.where(hot2, pre[nr:] + (run + n1), 0.0), axis=0, keepdims=True)
    run = jnp.broadcast_to(run + n1 + n2, run_ref.shape)
    run_ref[...] = run
    cnt_ref[...] = run

    eid_ref[...] = jnp.concatenate([i1, i2], axis=0) - MOE_GROUPS
    rank_ref[...] = jnp.concatenate([rank1, rank2], axis=0).astype(jnp.int32)
    gw_ref[...] = jnp.concatenate([w1, w2, jnp.zeros((GATE_ROWS - MOE_TOP_K, tm), F32)], axis=0)


def _outproj_call(a, w, res_lat, lat_off, res_ctx, gt, sh, sc, g, wr, n_ctx, tm):
    b_, tn, d = a.shape
    ncb = n_ctx // tm
    nt = tn // tm
    lob = lat_off // tm
    split = res_ctx is not None
    tok = pl.BlockSpec((None, tm, d), lambda b, j: (b, j, 0))
    pair = pl.BlockSpec((None, None, MOE_TOP_K, tm), lambda b, j: (b, j, 0, 0))
    res_specs, res_args = [], []
    if split:
        res_specs.append(pl.BlockSpec((None, tm, d), lambda b, j: (b, jnp.minimum(j, ncb - 1), 0)))
        res_args.append(res_ctx)
        res_specs.append(pl.BlockSpec((None, tm, d), lambda b, j: (b, jnp.maximum(j - ncb, 0) + lob, 0)))
    else:
        res_specs.append(pl.BlockSpec((None, tm, d), lambda b, j: (b, j + lob, 0)))
    res_args.append(res_lat)
    return pl.pallas_call(
        functools.partial(_outproj_kernel, n_ctx_blocks=ncb, split_residual=split),
        grid=(b_, nt),
        in_specs=[tok, _resident(w.shape)] + res_specs + [
            _mod_spec(d, ncb, b_), _mod_spec(d, ncb, b_), _mod_spec(d, ncb, b_),
            _resident((1, d)), _resident(wr.shape)],
        out_specs=[tok, tok, pair,
                   pl.BlockSpec((None, None, GATE_ROWS, tm), lambda b, j: (b, j, 0, 0)), pair,
                   pl.BlockSpec((ROUTER_ROWS, LANES), lambda b, j: (0, 0))],
        out_shape=[jax.ShapeDtypeStruct((b_, tn, d), F32),
                   jax.ShapeDtypeStruct((b_, tn, d), BF16),
                   jax.ShapeDtypeStruct((b_, nt, MOE_TOP_K, tm), jnp.int32),
                   jax.ShapeDtypeStruct((b_, nt, GATE_ROWS, tm), F32),
                   jax.ShapeDtypeStruct((b_, nt, MOE_TOP_K, tm), jnp.int32),
                   jax.ShapeDtypeStruct((ROUTER_ROWS, LANES), F32)],
        scratch_shapes=[pltpu.VMEM((ROUTER_ROWS, LANES), F32)],
        compiler_params=_cparams(("arbitrary", "arbitrary")),
        name="outproj_ffnmod",
    )(a, w, *res_args, gt, sh, sc, g, wr)


def _expert_kernel(*refs, blk_off, aliased):
    if aliased:
        be_ref, nu_ref, x_ref, wg_ref, wu_ref, wd_ref, _, y_ref, wgb_ref, wub_ref, wdb_ref = refs
    else:
        be_ref, nu_ref, x_ref, wg_ref, wu_ref, wd_ref, y_ref, wgb_ref, wub_ref, wdb_ref = refs
    step = pl.program_id(0)
    i = step + blk_off
    new_expert = (step == 0) | (be_ref[i] != be_ref[jnp.maximum(i - 1, 0)])

    @pl.when(new_expert)
    def _():
        wgb_ref[...] = wg_ref[...].astype(BF16)
        wub_ref[...] = wu_ref[...].astype(BF16)
        wdb_ref[...] = wd_ref[...].astype(BF16)

    @pl.when(i < nu_ref[0])
    def _():
        rows = x_ref.shape[0] // 2
        halves = [slice(j * rows, (j + 1) * rows) for j in range(2)]
        gates = [_dot(x_ref[r, :], wgb_ref[...]) for r in halves]
        ups = [_dot(x_ref[r, :], wub_ref[...]) for r in halves]
        for j, r in enumerate(halves):
            act = (gates[j] * _sigmoid(gates[j]) * ups[j]).astype(BF16)
            y_ref[r, :] = _dot(act, wdb_ref[...]).astype(y_ref.dtype)

    @pl.when(i >= nu_ref[0])
    def _():
        y_ref[...] = jnp.zeros_like(y_ref)


def _expert_call(buf, block_expert, n_used, wg, wu, wd, layer, blk_off, r_pad, ybuf=None):
    rows, d = buf.shape
    n_part = rows // MOE_BLOCK
    ff = wg.shape[-1]
    aliased = ybuf is not None

    def x_idx(s, be, nu):
        return (jnp.clip(s, 0, jnp.maximum(nu[0] - blk_off - 1, 0)), 0)

    in_specs = [
        pl.BlockSpec((MOE_BLOCK, d), x_idx),
        pl.BlockSpec((None, None, d, ff), lambda s, be, nu: (layer, be[s + blk_off], 0, 0)),
        pl.BlockSpec((None, None, d, ff), lambda s, be, nu: (layer, be[s + blk_off], 0, 0)),
        pl.BlockSpec((None, None, ff, d), lambda s, be, nu: (layer, be[s + blk_off], 0, 0)),
    ]
    args = [block_expert, n_used, buf, wg, wu, wd]
    if aliased:
        in_specs.append(pl.BlockSpec(memory_space=pl.ANY))
        args.append(ybuf)
    grid_spec = pltpu.PrefetchScalarGridSpec(
        num_scalar_prefetch=2,
        grid=(n_part,),
        in_specs=in_specs,
        out_specs=pl.BlockSpec((MOE_BLOCK, d), lambda s, be, nu: (s + blk_off, 0)),
        scratch_shapes=[pltpu.VMEM((d, ff), BF16), pltpu.VMEM((d, ff), BF16),
                        pltpu.VMEM((ff, d), BF16)],
    )
    return pl.pallas_call(
        functools.partial(_expert_kernel, blk_off=blk_off, aliased=aliased),
        grid_spec=grid_spec,
        out_shape=jax.ShapeDtypeStruct((r_pad, d), BF16),
        input_output_aliases={len(args) - 1: 0} if aliased else {},
        compiler_params=_cparams(("arbitrary",)),
        name="moe_experts",
    )(*args)


def _combine_kernel(*refs, with_next, n_prev):
    if with_next:
        y0_ref, y1_ref, gw_ref, x_ref, gt_ref, sh_ref, sc_ref, g_ref = refs[:8]
        xo_ref, hn_ref = refs[8 + n_prev:]
    else:
        y0_ref, y1_ref, gw_ref, x_ref, gt_ref = refs[:5]
        (xo_ref,) = refs[5 + n_prev:]
    gw = gw_ref[...].T
    f = gw[:, 0:1] * y0_ref[...].astype(F32) + gw[:, 1:2] * y1_ref[...].astype(F32)
    xo = x_ref[...] + gt_ref[...] * f
    xo_ref[...] = xo
    if with_next:
        h = _rms(xo) * g_ref[...]
        hn_ref[...] = (h * (1.0 + sc_ref[...]) + sh_ref[...]).astype(BF16)


def _combine_call(y0, y1, gw, x, gt, n_ctx, tm, next_mod, b_off, prev):
    b_, tn, d = x.shape
    bp = y0.shape[0]
    ncb = n_ctx // tm
    part_tok = pl.BlockSpec((None, tm, d), lambda b, j: (b, j, 0))
    tok = pl.BlockSpec((None, tm, d), lambda b, j: (b + b_off, j, 0))
    in_specs = [part_tok, part_tok,
                pl.BlockSpec((None, None, GATE_ROWS, tm), lambda b, j: (b + b_off, j, 0, 0)),
                tok, _mod_spec(d, ncb, b_, b_off)]
    args = [y0, y1, gw, x, gt]
    out_specs = [tok]
    out_shape = [jax.ShapeDtypeStruct((b_, tn, d), F32)]
    with_next = next_mod is not None
    if with_next:
        sh, sc, g = next_mod
        in_specs += [_mod_spec(d, ncb, b_, b_off), _mod_spec(d, ncb, b_, b_off), _resident((1, d))]
        args += [sh, sc, g]
        out_specs.append(tok)
        out_shape.append(jax.ShapeDtypeStruct((b_, tn, d), BF16))
    aliases = {}
    n_prev = 0
    if prev is not None:
        n_prev = len(prev)
        for k, arr in enumerate(prev):
            aliases[len(args)] = k
            in_specs.append(pl.BlockSpec(memory_space=pl.ANY))
            args.append(arr)
    return pl.pallas_call(
        functools.partial(_combine_kernel, with_next=with_next, n_prev=n_prev),
        grid=(bp, tn // tm),
        in_specs=in_specs,
        out_specs=out_specs,
        out_shape=out_shape,
        input_output_aliases=aliases,
        compiler_params=_cparams(("parallel", "arbitrary")),
        name="moe_combine",
    )(*args)


def _mla_in_kernel(h_ref, win_ref, qg_ref, wq_ref, wqr_ref, tqc_ref, tqs_ref,
                   ktab_ref, wk_ref, wv_ref, kgain_ref, q_ref, k_ref, v_ref, *, n_ctx_blocks):
    tm = h_ref.shape[0]
    proj = _dot(h_ref[...], win_ref[...])

    @pl.when(pl.program_id(1) >= n_ctx_blocks)
    def _():
        cq = proj[:, :MLA_Q_LORA]
        qn = (_rms(cq) * qg_ref[...]).astype(BF16)
        qa = _dot(qn, wq_ref[...])
        qr = _dot(qn, wqr_ref[...])
        tqc = tqc_ref[...]
        tqs = tqs_ref[...]
        for h in range(MLA_HEADS):
            hs = slice(h * LANES, (h + 1) * LANES)
            a = qa[:, hs]
            rq = lax.rsqrt(jnp.sum(a * a, axis=-1, keepdims=True) * (1.0 / MLA_QK) + NORM_EPS)
            q_ref[h] = ((a * tqc + qr[:, hs] * tqs) * rq).astype(BF16)

    p2 = proj[:, MLA_Q_LORA:]
    lane = lax.broadcasted_iota(jnp.int32, (tm, p2.shape[1]), 1)
    is_kv = lane < MLA_KV_LORA
    sq = p2 * p2
    ms_kv = jnp.sum(jnp.where(is_kv, sq, 0.0), axis=-1, keepdims=True) * (1.0 / MLA_KV_LORA)
    is_rope = (lane >= MLA_KV_LORA) & (lane < MLA_KV_LORA + MLA_ROPE)
    ss_rope = jnp.sum(jnp.where(is_rope, sq, 0.0), axis=-1, keepdims=True)
    mult = ktab_ref[...] * jnp.where(is_kv, lax.rsqrt(ms_kv + NORM_EPS), 1.0)
    lhs = (p2 * mult).astype(BF16)
    ka = _dot(lhs, wk_ref[...])
    va = _dot(lhs[:, :MLA_KV_LORA], wv_ref[...])
    kgain = kgain_ref[...]
    lane1 = lax.broadcasted_iota(jnp.int32, (tm, LANES), 1)
    for h in range(MLA_HEADS):
        hs = slice(h * LANES, (h + 1) * LANES)
        a = ka[:, hs]
        ssn = jnp.sum(jnp.where(lane1 < MLA_NOPE, a * a, 0.0), axis=-1, keepdims=True)
        rk = lax.rsqrt((ssn + ss_rope) * (1.0 / MLA_QK) + NORM_EPS)
        k_ref[h] = (a * kgain * rk).astype(BF16)
        ones_lane = MLA_V if h % 2 == 0 else 0
        v_ref[h] = jnp.where(lane1 == ones_lane, 1.0, va[:, hs]).astype(BF16)


def _mla_in_call(hm, win, qg, wq, wqr, tqc, tqs, ktab, wk, wv, kgain, n_ctx, tm):
    b_, t_, d = hm.shape
    ncb = n_ctx // tm
    tok = pl.BlockSpec((None, tm, d), lambda b, j: (b, j, 0))
    head = pl.BlockSpec((None, MLA_HEADS, tm, LANES), lambda b, j: (b, 0, j, 0))
    hshape = jax.ShapeDtypeStruct((b_, MLA_HEADS, t_, LANES), BF16)
    qhead = pl.BlockSpec((None, MLA_HEADS, tm, LANES), lambda b, j: (b, 0, jnp.maximum(j - ncb, 0), 0))
    qshape = jax.ShapeDtypeStruct((b_, MLA_HEADS, t_ - n_ctx, LANES), BF16)
    return pl.pallas_call(
        functools.partial(_mla_in_kernel, n_ctx_blocks=ncb),
        grid=(b_, t_ // tm),
        in_specs=[tok, _resident(win.shape), _resident(qg.shape), _resident(wq.shape),
                  _resident(wqr.shape),
                  pl.BlockSpec((tm, LANES), lambda b, j: (j, 0)),
                  pl.BlockSpec((tm, LANES), lambda b, j: (j, 0)),
                  pl.BlockSpec((tm, ktab.shape[1]), lambda b, j: (j, 0)),
                  _resident(wk.shape), _resident(wv.shape), _resident(kgain.shape)],
        out_specs=[qhead, head, head],
        out_shape=[qshape, hshape, hshape],
        compiler_params=_cparams(("parallel", "arbitrary")),
        name="mla_in",
    )(hm, win, qg, wq, wqr, tqc, tqs, ktab, wk, wv, kgain)


def _attn_kernel(q_ref, k_ref, v_ref, o_ref):
    n_heads, tq, _ = q_ref.shape
    t_ = k_ref.shape[1]
    ck = next(c for c in ATT_KEY_CHUNKS if t_ % c == 0)
    outs = []
    for hh in range(n_heads):
        q = q_ref[hh]
        m = jnp.full((tq, 1), -jnp.inf, F32)
        acc = jnp.zeros((tq, LANES), F32)
        for c in range(t_ // ck):
            ks = slice(c * ck, (c + 1) * ck)
            s = _dot_nt(q, k_ref[hh, ks, :])
            m_new = jnp.maximum(m, jnp.max(s, axis=-1, keepdims=True))
            p = jnp.exp2((s - m_new).astype(BF16))
            acc = acc * jnp.exp2(m - m_new) + _dot(p, v_ref[hh, ks, :])
            m = m_new
        ones_lane = MLA_V if hh % 2 == 0 else 0
        outs.append(acc / acc[:, ones_lane:ones_lane + 1])
    lane = lax.broadcasted_iota(jnp.int32, outs[0].shape, 1)
    for pr in range(len(outs) // 2):
        o_ref[:, pr * LANES:(pr + 1) * LANES] = jnp.where(
            lane < MLA_V, outs[2 * pr], outs[2 * pr + 1]).astype(o_ref.dtype)


def _attn_call(q, k, v, tq):
    b_, nh, l_, _ = q.shape
    t_ = k.shape[2]
    hps = ATT_HEADS_PER_STEP
    return pl.pallas_call(
        _attn_kernel,
        grid=(b_, nh // hps, l_ // tq),
        in_specs=[
            pl.BlockSpec((None, hps, tq, LANES), lambda b, h, i: (b, h, i, 0)),
            pl.BlockSpec((None, hps, t_, LANES), lambda b, h, i: (b, h, 0, 0)),
            pl.BlockSpec((None, hps, t_, LANES), lambda b, h, i: (b, h, 0, 0)),
        ],
        out_specs=pl.BlockSpec((None, tq, (hps // 2) * LANES), lambda b, h, i: (b, i, h)),
        out_shape=jax.ShapeDtypeStruct((b_, l_, (nh // 2) * LANES), BF16),
        compiler_params=_cparams(("parallel", "parallel", "arbitrary")),
        name="mla_attention",
    )(q, k, v)


def _dispatch_layout(eid, rank, counts):
    tm = eid.shape[-1]
    n_slot = eid.size
    starts = jnp.cumsum(counts) - counts
    padded = (counts + MOE_BLOCK - 1) // MOE_BLOCK * MOE_BLOCK
    pad_ends = jnp.cumsum(padded)
    pad_starts = pad_ends - padded
    n_blocks = (n_slot + MOE_BLOCK - 1) // MOE_BLOCK + MOE_EXPERTS
    block_start = jnp.arange(n_blocks, dtype=jnp.int32) * MOE_BLOCK
    block_expert = jnp.minimum(
        jnp.sum((block_start[:, None] >= pad_ends[None, :]).astype(jnp.int32), axis=1),
        MOE_EXPERTS - 1)
    n_used = pad_ends[-1:] // MOE_BLOCK
    slot_row = rank
    for e in range(MOE_EXPERTS):
        slot_row = slot_row + jnp.where(eid == e, pad_starts[e], 0)
    order = jnp.argsort(eid.reshape(n_slot)).astype(jnp.int32)
    offs = (block_start - pad_starts[block_expert])[:, None] + jnp.arange(MOE_BLOCK, dtype=jnp.int32)
    valid = offs < counts[block_expert][:, None]
    pos = jnp.minimum(starts[block_expert][:, None] + offs, n_slot - 1)
    slot = jnp.take(order, pos, mode='clip')
    tok = (slot // (MOE_TOP_K * tm)) * tm + slot % tm
    n_tok = n_slot // MOE_TOP_K
    filler = (block_start[:, None] + jnp.arange(MOE_BLOCK, dtype=jnp.int32)) % n_tok
    row_tok = jnp.where(valid, tok, filler)
    return row_tok.reshape(-1), slot_row, block_expert, n_used


def _moe(h2, eid, rank, counts, gw, x, gt, wg, wu, wd, layer, n_ctx, tm, next_mod=None):
    b_, tn, d = h2.shape
    n_tok = b_ * tn
    cnt = counts[MOE_GROUPS:MOE_GROUPS + MOE_EXPERTS, 0].astype(jnp.int32)
    row_tok, slot_row, block_expert, n_used = _dispatch_layout(eid, rank, cnt)
    r_pad = row_tok.shape[0]
    n_blocks = r_pad // MOE_BLOCK
    bounds = [n_blocks * p // EXPERT_PARTS for p in range(EXPERT_PARTS + 1)]
    h2_rows = h2.reshape(n_tok, d)
    ybuf = None
    for lo, hi in zip(bounds[:-1], bounds[1:]):
        part = jnp.take(h2_rows, row_tok[lo * MOE_BLOCK:hi * MOE_BLOCK], axis=0, mode='clip')
        ybuf = _expert_call(part, block_expert, n_used, wg, wu, wd, layer, lo, r_pad, ybuf)
    bp = b_ // COMBINE_PARTS
    outs = None
    for p in range(COMBINE_PARTS):
        rows = slot_row[p * bp:(p + 1) * bp]
        y0 = jnp.take(ybuf, rows[:, :, 0, :].reshape(bp * tn), axis=0, mode='clip').reshape(bp, tn, d)
        y1 = jnp.take(ybuf, rows[:, :, 1, :].reshape(bp * tn), axis=0, mode='clip').reshape(bp, tn, d)
        outs = _combine_call(y0, y1, gw, x, gt, n_ctx, tm, next_mod, p * bp, outs)
    return outs


def _rot_half_perm():
    half = MLA_ROPE // 2
    j = jnp.arange(MLA_ROPE)
    within = j % half
    base = j - within
    src = jnp.where(within < half // 2, base + within + half // 2, base + within - half // 2)
    sign = jnp.where(within < half // 2, -1.0, 1.0).astype(F32)
    return src, sign


def _rope_tables(n_ctx, seq):
    rows = seq // GRID_W
    row = jnp.repeat(jnp.arange(rows), GRID_W)
    col = jnp.tile(jnp.arange(GRID_W), rows)
    half = MLA_ROPE // 2
    inv_freq = ROPE_THETA ** (-jnp.arange(0, half, 2, dtype=F32) / half)
    ang = jnp.stack([row, col], axis=-1).astype(F32)[..., None] * inv_freq
    ang = jnp.concatenate([ang, ang], axis=-1).reshape(seq, MLA_ROPE)
    cos = jnp.concatenate([jnp.ones((n_ctx, MLA_ROPE), F32), jnp.cos(ang)], axis=0)
    sin = jnp.concatenate([jnp.zeros((n_ctx, MLA_ROPE), F32), jnp.sin(ang)], axis=0)
    return cos, sin


def _mla_prepare(w_in, w_qb, w_kvb, q_qk_g, k_qk_g, kv_norm_g, n_ctx, seq):
    d = w_in.shape[0]
    src, sign = _rot_half_perm()
    cos, sin = _rope_tables(n_ctx, seq)
    t_ = n_ctx + seq

    rope0 = MLA_Q_LORA + MLA_KV_LORA
    w_rope = w_in[:, rope0:rope0 + MLA_ROPE]
    win = jnp.concatenate(
        [w_in, w_rope[:, src] * sign, jnp.zeros((d, 512 - rope0 - 2 * MLA_ROPE), F32)], axis=1)

    wq3 = w_qb.reshape(MLA_Q_LORA, MLA_HEADS, MLA_QK)
    pad = jnp.zeros((MLA_Q_LORA, MLA_HEADS, LANES - MLA_QK), F32)
    wq = jnp.concatenate([wq3, pad], axis=-1).reshape(MLA_Q_LORA, MLA_HEADS * LANES)
    wq_rot = wq3[:, :, MLA_NOPE:][:, :, src] * sign
    wqr = jnp.concatenate([jnp.zeros((MLA_Q_LORA, MLA_HEADS, MLA_NOPE), F32), wq_rot, pad],
                          axis=-1).reshape(MLA_Q_LORA, MLA_HEADS * LANES)

    scale = MLA_QK ** -0.5 * 1.4426950408889634
    gq_n, gq_r = q_qk_g[:MLA_NOPE], q_qk_g[MLA_NOPE:]
    zpad = jnp.zeros((t_, LANES - MLA_QK), F32)
    tqc = jnp.concatenate([jnp.broadcast_to(gq_n, (t_, MLA_NOPE)), gq_r * cos, zpad], axis=1) * scale
    tqs = jnp.concatenate([jnp.zeros((t_, MLA_NOPE), F32), gq_r[src] * sin, zpad], axis=1) * scale

    wkv3 = w_kvb.reshape(MLA_KV_LORA, MLA_HEADS, MLA_NOPE + MLA_V)
    wk_lat = jnp.concatenate(
        [wkv3[:, :, :MLA_NOPE], jnp.zeros((MLA_KV_LORA, MLA_HEADS, LANES - MLA_NOPE), F32)], axis=-1)
    place = jnp.concatenate([jnp.zeros((MLA_ROPE, MLA_NOPE), F32), jnp.eye(MLA_ROPE, dtype=F32),
                             jnp.zeros((MLA_ROPE, LANES - MLA_QK), F32)], axis=1)
    place = jnp.broadcast_to(place[:, None, :], (MLA_ROPE, MLA_HEADS, LANES))
    wk = jnp.concatenate([wk_lat, place, place,
                          jnp.zeros((256 - MLA_KV_LORA - 2 * MLA_ROPE, MLA_HEADS, LANES), F32)],
                         axis=0).reshape(256, MLA_HEADS * LANES)
    gk_n, gk_r = k_qk_g[:MLA_NOPE], k_qk_g[MLA_NOPE:]
    ktab = jnp.concatenate([jnp.broadcast_to(kv_norm_g, (t_, MLA_KV_LORA)), gk_r * cos,
                            gk_r[src] * sin, jnp.zeros((t_, 256 - MLA_KV_LORA - 2 * MLA_ROPE), F32)],
                           axis=1)
    kgain = jnp.concatenate([gk_n, jnp.ones((MLA_ROPE,), F32),
                             jnp.zeros((LANES - MLA_QK,), F32)]).reshape(1, LANES)

    wv_h = wkv3[:, :, MLA_NOPE:]
    zv = jnp.zeros_like(wv_h)
    odd = (jnp.arange(MLA_HEADS) % 2 == 1)[None, :, None]
    wv = jnp.concatenate([jnp.where(odd, zv, wv_h), jnp.where(odd, wv_h, zv)],
                         axis=-1).reshape(MLA_KV_LORA, MLA_HEADS * LANES)
    return (win.astype(BF16), wq.astype(BF16), wqr.astype(BF16), tqc, tqs, ktab,
            wk.astype(BF16), wv.astype(BF16), kgain)


def _router_weights(w_group, w_expert):
    d = w_group.shape[0]
    wt = jnp.concatenate([w_group, w_expert], axis=1).T
    wt = jnp.concatenate([wt, jnp.zeros((ROUTER_ROWS - wt.shape[0], d), F32)], axis=0)
    head = wt.astype(BF16)
    rest = (wt - head.astype(F32)).astype(BF16)
    return jnp.concatenate([head, rest], axis=0)


def kernel(x, c, ctx, c_ctx, ada_w, ada_b, norm_mix_g, norm_ffn_g, hg_w_in, hg_lower_bounds, hg_out_norm_g, hg_w_out, mla_w_in, mla_q_norm_g, mla_kv_norm_g, mla_w_qb, mla_w_kvb, mla_q_qknorm_g, mla_k_qknorm_g, mla_w_out, moe_w_group, moe_w_expert, moe_w_gate, moe_w_up, moe_w_down):
    b_, seq, d = x.shape
    n_ctx = ctx.shape[1]
    tm = min(ROW_TILE, n_ctx)
    tq = min(ATT_Q_TILE, seq)
    assert n_ctx % GLA_BLOCK == 0 and seq % GLA_BLOCK == 0 and n_ctx % tm == 0 and seq % tm == 0
    assert b_ + 1 <= ADA_ROWS and seq % GRID_W == 0

    cc = jnp.concatenate([c, c_ctx[None, :], jnp.zeros((ADA_ROWS - b_ - 1, d), F32)], axis=0)
    mods = _ada_call(cc, ada_w, ada_b).reshape(DEPTH, ADA_CHUNKS, ADA_ROWS, 1, d)

    def mod(i, chunk):
        return mods[i, chunk]

    row = lambda v: v.reshape(1, -1)

    lower = jnp.cumsum(jax.nn.softmax(hg_lower_bounds.astype(F32), axis=1), axis=1)[:, 0]
    qs, kf, gf, kb, gb, v, gs = _hg_in_call(
        ctx, x, mod(0, 0), mod(0, 1), row(norm_mix_g[0]), hg_w_in[0].astype(BF16), lower, tm)
    o_b = _gla_call(qs, kb, gb, v, n_ctx, reverse=True)
    og = _gla_call(qs, kf, gf, v, n_ctx, reverse=False,
                   final_args=(o_b, gs, row(hg_out_norm_g[0])))
    x1, h2, eid, gw, rank, cnt = _outproj_call(
        og, hg_w_out[0].astype(BF16), x, 0, ctx, mod(0, 2), mod(0, 3), mod(0, 4),
        row(norm_ffn_g[0]), _router_weights(moe_w_group[0], moe_w_expert[0]), n_ctx, tm)
    x2, hm = _moe(h2, eid, rank, cnt, gw, x1, mod(0, 5), moe_w_gate, moe_w_up, moe_w_down, 0,
                  n_ctx, tm, next_mod=(mod(1, 0), mod(1, 1), row(norm_mix_g[1])))

    win, wq, wqr, tqc, tqs, ktab, wk, wv, kgain = _mla_prepare(
        mla_w_in[0], mla_w_qb[0], mla_w_kvb[0], mla_q_qknorm_g[0], mla_k_qknorm_g[0],
        mla_kv_norm_g[0], n_ctx, seq)
    q, k, vv = _mla_in_call(hm, win, row(mla_q_norm_g[0]), wq, wqr, tqc, tqs, ktab, wk, wv, kgain,
                            n_ctx, tm)
    o_att = _attn_call(q, k, vv, tq)
    x3, h2, eid, gw, rank, cnt = _outproj_call(
        o_att, mla_w_out[0].astype(BF16), x2, n_ctx, None, mod(1, 2), mod(1, 3), mod(1, 4),
        row(norm_ffn_g[1]), _router_weights(moe_w_group[1], moe_w_expert[1]), 0, tm)
    (x4,) = _moe(h2, eid, rank, cnt, gw, x3, mod(1, 5), moe_w_gate, moe_w_up, moe_w_down, 1, 0, tm)
    return x4
```

```python
import functools

import jax
import jax.numpy as jnp
from jax import lax
from jax.experimental import pallas as pl
from jax.experimental.pallas import tpu as pltpu

F32 = jnp.float32
BF16 = jnp.bfloat16

DEPTH = 2
ADA_CHUNKS = 6
NORM_EPS = 1e-6
GRID_W = 64
HG_HEADS = 8
HG_CHUNK = 32
MLA_HEADS = 16
MLA_Q_LORA = 256
MLA_KV_LORA = 128
MLA_NOPE = 64
MLA_ROPE = 32
MLA_V = 64
MLA_QK = MLA_NOPE + MLA_ROPE
ROPE_THETA = 10000.0
MOE_GROUPS = 4
MOE_EPG = 8
MOE_EXPERTS = MOE_GROUPS * MOE_EPG
MOE_TOP_K = 2
MOE_FF = 512

LANES = 128
ROUTER_ROWS = 48
GATE_ROWS = 8
VMEM_LIMIT = 56 * 1024 * 1024

ROW_TILE = 256
GLA_BLOCK = 128
GLA_BATCH = 4
MOE_BLOCK = 512
ATT_Q_TILE = 512
ATT_HEADS_PER_STEP = 8
ATT_KEY_CHUNKS = (256, 128)
EXPERT_PARTS = 4
COMBINE_PARTS = 2
ADA_ROWS = 40


def _cparams(sem):
    return pltpu.CompilerParams(dimension_semantics=sem, vmem_limit_bytes=VMEM_LIMIT)


def _resident(shape):
    nd = len(shape)
    return pl.BlockSpec(shape, lambda *_: (0,) * nd, pipeline_mode=pl.Buffered(1))


def _dot(a, b):
    return jnp.dot(a, b, preferred_element_type=F32)


def _dot_nt(a, b):
    return lax.dot_general(a, b, (((1,), (1,)), ((), ())), preferred_element_type=F32)


def _dot_tn(a, b):
    return lax.dot_general(a, b, (((0,), (0,)), ((), ())), preferred_element_type=F32)


def _sigmoid(x):
    return 1.0 / (1.0 + jnp.exp(-x))


def _rms(x):
    return x * lax.rsqrt(jnp.mean(x * x, axis=-1, keepdims=True) + NORM_EPS)


def _ada_kernel(c_ref, w_ref, b_ref, o_ref):
    c = c_ref[...]
    a = (c * _sigmoid(c)).astype(BF16)
    o_ref[...] = _dot(a, w_ref[...].astype(BF16)) + b_ref[...]


def _ada_call(cc, ada_w, ada_b):
    depth, d, _ = ada_w.shape
    rows = cc.shape[0]
    return pl.pallas_call(
        _ada_kernel,
        grid=(depth, ADA_CHUNKS),
        in_specs=[
            pl.BlockSpec((rows, d), lambda i, j: (0, 0)),
            pl.BlockSpec((None, d, d), lambda i, j: (i, 0, j)),
            pl.BlockSpec((None, 1, d), lambda i, j: (i, 0, j)),
        ],
        out_specs=pl.BlockSpec((None, None, rows, d), lambda i, j: (i, j, 0, 0)),
        out_shape=jax.ShapeDtypeStruct((depth, ADA_CHUNKS, rows, d), F32),
        compiler_params=_cparams(("arbitrary", "arbitrary")),
        name="ada_mod",
    )(cc, ada_w, ada_b.reshape(depth, 1, ADA_CHUNKS * d))


def _mod_spec(d, n_ctx_blocks, ctx_row, b_off=0):
    def idx(b, j):
        return (jnp.where(j < n_ctx_blocks, ctx_row, b + b_off), 0, 0)
    return pl.BlockSpec((None, 1, d), idx)


def _hg_in_kernel(xc_ref, x_ref, sh_ref, sc_ref, g_ref, w_ref, lb_ref,
                  q_ref, kf_ref, gf_ref, kb_ref, gb_ref, v_ref, gs_ref, *, n_ctx_blocks):
    d = x_ref.shape[-1]
    x = jnp.where(pl.program_id(1) < n_ctx_blocks, xc_ref[...], x_ref[...])
    h = _rms(x) * g_ref[...]
    h = (h * (1.0 + sc_ref[...]) + sh_ref[...]).astype(BF16)

    def proj(c):
        return _dot(h, w_ref[:, c * d:(c + 1) * d])

    p = proj(0)
    q_ref[...] = (p * _sigmoid(p)).astype(BF16)
    for c, k_ref, lg_ref in ((1, kf_ref, gf_ref), (2, kb_ref, gb_ref)):
        s = _sigmoid(proj(c))
        lb = lb_ref[c - 1:c, :]
        k_ref[...] = ((1.0 - lb) * (1.0 - s)).astype(BF16)
        lg_ref[...] = jnp.log(lb + (1.0 - lb) * s).astype(BF16)
    v_ref[...] = proj(3).astype(BF16)
    p = proj(4)
    gs_ref[...] = (p * _sigmoid(p)).astype(BF16)


def _hg_in_call(ctx, x, sh, sc, g, w_in, lb, tm):
    b_, n_ctx, d = ctx.shape
    t_ = n_ctx + x.shape[1]
    ncb = n_ctx // tm
    tok = pl.BlockSpec((None, tm, d), lambda b, j: (b, j, 0))
    out = jax.ShapeDtypeStruct((b_, t_, d), BF16)
    return pl.pallas_call(
        functools.partial(_hg_in_kernel, n_ctx_blocks=ncb),
        grid=(b_, t_ // tm),
        in_specs=[pl.BlockSpec((None, tm, d), lambda b, j: (b, jnp.minimum(j, ncb - 1), 0)),
                  pl.BlockSpec((None, tm, d), lambda b, j: (b, jnp.maximum(j - ncb, 0), 0)),
                  _mod_spec(d, ncb, b_), _mod_spec(d, ncb, b_),
                  _resident((1, d)), _resident(w_in.shape), _resident((2, d))],
        out_specs=[tok] * 7,
        out_shape=[out] * 7,
        compiler_params=_cparams(("parallel", "arbitrary")),
        name="hg_in",
    )(ctx, x, sh, sc, g, w_in, lb)


def _gla_kernel(*refs, reverse, final, n_heads):
    if final:
        q_ref, k_ref, g_ref, v_ref, ob_ref, gs_ref, gn_ref, o_ref, st_ref = refs
    else:
        q_ref, k_ref, g_ref, v_ref, o_ref, st_ref = refs
    nb, tb, d = q_ref.shape
    dh = d // n_heads
    n_chunks = tb // HG_CHUNK

    @pl.when(pl.program_id(1) == 0)
    def _():
        st_ref[...] = jnp.zeros_like(st_ref)

    row = lax.broadcasted_iota(jnp.int32, (tb, tb), 0)
    col = lax.broadcasted_iota(jnp.int32, (tb, tb), 1)
    same_chunk = (row // HG_CHUNK) == (col // HG_CHUNK)
    tri = same_chunk & ((col >= row) if reverse else (col <= row))
    cum = jnp.where(tri, 1.0, 0.0).astype(BF16)
    chunk_order = range(n_chunks - 1, -1, -1) if reverse else range(n_chunks)

    qd, ki_b, e_end, k_end = [], [], [], []
    for n in range(nb):
        b = _dot(cum, g_ref[n])
        qd.append((q_ref[n].astype(F32) * jnp.exp(b)).astype(BF16))
        ki = k_ref[n].astype(F32) * jnp.exp(-b)
        ki_b.append(ki.astype(BF16))
        ee, ke = {}, {}
        for c in range(n_chunks):
            r0 = c * HG_CHUNK
            last = r0 if reverse else r0 + HG_CHUNK - 1
            ee[c] = jnp.exp(b[last:last + 1, :])
            ke[c] = (ki[r0:r0 + HG_CHUNK, :] * ee[c]).astype(BF16)
        e_end.append(ee)
        k_end.append(ke)

    chains = [(n, h) for n in range(nb) for h in range(n_heads)]
    hsl = [slice(h * dh, (h + 1) * dh) for h in range(n_heads)]
    scores = {(n, h): jnp.where(tri, _dot_nt(qd[n][:, hsl[h]], ki_b[n][:, hsl[h]]), 0.0).astype(BF16)
              for n, h in chains}
    o_intra = {(n, h): _dot(scores[n, h], v_ref[n, :, hsl[h]]) for n, h in chains}
    kv = {(n, h, c): _dot_tn(v_ref[n, c * HG_CHUNK:(c + 1) * HG_CHUNK, hsl[h]], k_end[n][c][:, hsl[h]])
          for n, h in chains for c in range(n_chunks)}
    st = {(n, h): st_ref[n, h] for n, h in chains}
    for c in chunk_order:
        rs = slice(c * HG_CHUNK, (c + 1) * HG_CHUNK)
        for n, h in chains:
            hs = hsl[h]
            o_c = o_intra[n, h][rs, :] + _dot_nt(qd[n][rs, hs], st[n, h].astype(BF16))
            st[n, h] = st[n, h] * e_end[n][c][:, hs] + kv[n, h, c]
            if final:
                o_c = o_c + ob_ref[n, rs, hs].astype(F32)
                o_c = _rms(o_c) * gn_ref[...]
                o_c = o_c * gs_ref[n, rs, hs].astype(F32)
            o_ref[n, rs, hs] = o_c.astype(o_ref.dtype)
    for n, h in chains:
        st_ref[n, h] = st[n, h]


def _gla_call(q, k, g, v, n_ctx, reverse, final_args=None):
    b_, t_, d = q.shape
    tb = GLA_BLOCK
    gb = min(GLA_BATCH, b_)
    nb = t_ // tb
    ncb = n_ctx // tb
    dh = d // HG_HEADS

    def blk(b, j):
        if reverse:
            jj = jnp.where(j < ncb, ncb - 1 - j, nb - 1 - (j - ncb))
        else:
            jj = j
        return (b, jj, 0)

    tok = pl.BlockSpec((gb, tb, d), blk)
    in_specs = [tok] * 4
    args = [q, k, g, v]
    final = final_args is not None
    if final:
        o_b, gs, gn = final_args
        in_specs += [tok, tok, _resident((1, dh))]
        args += [o_b, gs, gn]
    return pl.pallas_call(
        functools.partial(_gla_kernel, reverse=reverse, final=final, n_heads=HG_HEADS),
        grid=(b_ // gb, nb),
        in_specs=in_specs,
        out_specs=tok,
        out_shape=jax.ShapeDtypeStruct((b_, t_, d), BF16),
        scratch_shapes=[pltpu.VMEM((gb, HG_HEADS, dh, dh), F32)],
        compiler_params=_cparams(("parallel", "arbitrary")),
        name="gla_bwd" if reverse else "gla_fwd",
    )(*args)


def _outproj_kernel(*refs, n_ctx_blocks, split_residual):
    if split_residual:
        (a_ref, w_ref, xc_ref, x_ref, gt_ref, sh_ref, sc_ref, g_ref, wr_ref,
         xn_ref, h2_ref, eid_ref, gw_ref, rank_ref, cnt_ref, run_ref) = refs
    else:
        (a_ref, w_ref, x_ref, gt_ref, sh_ref, sc_ref, g_ref, wr_ref,
         xn_ref, h2_ref, eid_ref, gw_ref, rank_ref, cnt_ref, run_ref) = refs
    tm = a_ref.shape[0]
    nr = ROUTER_ROWS
    first = (pl.program_id(0) == 0) & (pl.program_id(1) == 0)

    @pl.when(first)
    def _():
        run_ref[...] = jnp.zeros_like(run_ref)

    x = x_ref[...]
    if split_residual:
        x = jnp.where(pl.program_id(1) < n_ctx_blocks, xc_ref[...], x)
    y = _dot(a_ref[...], w_ref[...])
    xn = x + gt_ref[...] * y
    xn_ref[...] = xn
    h2 = _rms(xn) * g_ref[...]
    h2 = h2 * (1.0 + sc_ref[...]) + sh_ref[...]
    h2b = h2.astype(BF16)
    h2_ref[...] = h2b
    h2r = (h2 - h2b.astype(F32)).astype(BF16)
    lg2 = _dot_nt(wr_ref[...], h2b) + _dot_nt(wr_ref[...], h2r)
    lg = lg2[:nr] + lg2[nr:]

    ridx = lax.broadcasted_iota(jnp.int32, lg.shape, 0)
    neg = -jnp.inf
    gl = jnp.where(ridx < MOE_GROUPS, lg, neg)
    gmax = jnp.max(gl, axis=0, keepdims=True)
    g_idx = jnp.min(jnp.where(gl == gmax, ridx, nr), axis=0, keepdims=True)
    p_group = 1.0 / jnp.sum(jnp.exp(gl - gmax), axis=0, keepdims=True)
    lo = MOE_GROUPS + MOE_EPG * g_idx
    el = jnp.where((ridx >= lo) & (ridx < lo + MOE_EPG), lg, neg)
    m1 = jnp.max(el, axis=0, keepdims=True)
    i1 = jnp.min(jnp.where(el == m1, ridx, nr), axis=0, keepdims=True)
    el2 = jnp.where(ridx == i1, neg, el)
    m2 = jnp.max(el2, axis=0, keepdims=True)
    i2 = jnp.min(jnp.where(el2 == m2, ridx, nr), axis=0, keepdims=True)
    r21 = jnp.exp(m2 - m1)
    w1 = p_group / (1.0 + r21)
    w2 = w1 * r21

    hot1 = ridx == i1
    hot2 = ridx == i2
    f1 = jnp.where(hot1, 1.0, 0.0)
    f2 = jnp.where(hot2, 1.0, 0.0)
    r = lax.broadcasted_iota(jnp.int32, (tm, tm), 0)
    c = lax.broadcasted_iota(jnp.int32, (tm, tm), 1)
    earlier = jnp.where(r < c, 1.0, 0.0).astype(BF16)
    pre = _dot(jnp.concatenate([f1, f2], axis=0).astype(BF16), earlier)
    run = run_ref[:, 0:1]
    n1 = jnp.sum(f1, axis=1, keepdims=True)
    n2 = jnp.sum(f2, axis=1, keepdims=True)
    rank1 = jnp.sum(jnp.where(hot1, pre[:nr] + run, 0.0), axis=0, keepdims=True)
    rank2 = jnp.sum(jnp.where(hot2, pre[nr:] + (run + n1), 0.0), axis=0, keepdims=True)
    run = jnp.broadcast_to(run + n1 + n2, run_ref.shape)
    run_ref[...] = run
    cnt_ref[...] = run

    eid_ref[...] = jnp.concatenate([i1, i2], axis=0) - MOE_GROUPS
    rank_ref[...] = jnp.concatenate([rank1, rank2], axis=0).astype(jnp.int32)
    gw_ref[...] = jnp.concatenate([w1, w2, jnp.zeros((GATE_ROWS - MOE_TOP_K, tm), F32)], axis=0)


def _outproj_call(a, w, res_lat, lat_off, res_ctx, gt, sh, sc, g, wr, n_ctx, tm):
    b_, tn, d = a.shape
    ncb = n_ctx // tm
    nt = tn // tm
    lob = lat_off // tm
    split = res_ctx is not None
    tok = pl.BlockSpec((None, tm, d), lambda b, j: (b, j, 0))
    pair = pl.BlockSpec((None, None, MOE_TOP_K, tm), lambda b, j: (b, j, 0, 0))
    res_specs, res_args = [], []
    if split:
        res_specs.append(pl.BlockSpec((None, tm, d), lambda b, j: (b, jnp.minimum(j, ncb - 1), 0)))
        res_args.append(res_ctx)
        res_specs.append(pl.BlockSpec((None, tm, d), lambda b, j: (b, jnp.maximum(j - ncb, 0) + lob, 0)))
    else:
        res_specs.append(pl.BlockSpec((None, tm, d), lambda b, j: (b, j + lob, 0)))
    res_args.append(res_lat)
    return pl.pallas_call(
        functools.partial(_outproj_kernel, n_ctx_blocks=ncb, split_residual=split),
        grid=(b_, nt),
        in_specs=[tok, _resident(w.shape)] + res_specs + [
            _mod_spec(d, ncb, b_), _mod_spec(d, ncb, b_), _mod_spec(d, ncb, b_),
            _resident((1, d)), _resident(wr.shape)],
        out_specs=[tok, tok, pair,
                   pl.BlockSpec((None, None, GATE_ROWS, tm), lambda b, j: (b, j, 0, 0)), pair,
                   pl.BlockSpec((ROUTER_ROWS, LANES), lambda b, j: (0, 0))],
        out_shape=[jax.ShapeDtypeStruct((b_, tn, d), F32),
                   jax.ShapeDtypeStruct((b_, tn, d), BF16),
                   jax.ShapeDtypeStruct((b_, nt, MOE_TOP_K, tm), jnp.int32),
                   jax.ShapeDtypeStruct((b_, nt, GATE_ROWS, tm), F32),
                   jax.ShapeDtypeStruct((b_, nt, MOE_TOP_K, tm), jnp.int32),
                   jax.ShapeDtypeStruct((ROUTER_ROWS, LANES), F32)],
        scratch_shapes=[pltpu.VMEM((ROUTER_ROWS, LANES), F32)],
        compiler_params=_cparams(("arbitrary", "arbitrary")),
        name="outproj_ffnmod",
    )(a, w, *res_args, gt, sh, sc, g, wr)


def _expert_kernel(*refs, blk_off, aliased):
    if aliased:
        be_ref, nu_ref, x_ref, wg_ref, wu_ref, wd_ref, _, y_ref, wgb_ref, wub_ref, wdb_ref = refs
    else:
        be_ref, nu_ref, x_ref, wg_ref, wu_ref, wd_ref, y_ref, wgb_ref, wub_ref, wdb_ref = refs
    step = pl.program_id(0)
    i = step + blk_off
    new_expert = (step == 0) | (be_ref[i] != be_ref[jnp.maximum(i - 1, 0)])

    @pl.when(new_expert)
    def _():
        wgb_ref[...] = wg_ref[...].astype(BF16)
        wub_ref[...] = wu_ref[...].astype(BF16)
        wdb_ref[...] = wd_ref[...].astype(BF16)

    @pl.when(i < nu_ref[0])
    def _():
        rows = x_ref.shape[0] // 2
        halves = [slice(j * rows, (j + 1) * rows) for j in range(2)]
        gates = [_dot(x_ref[r, :], wgb_ref[...]) for r in halves]
        ups = [_dot(x_ref[r, :], wub_ref[...]) for r in halves]
        for j, r in enumerate(halves):
            act = (gates[j] * _sigmoid(gates[j]) * ups[j]).astype(BF16)
            y_ref[r, :] = _dot(act, wdb_ref[...]).astype(y_ref.dtype)

    @pl.when(i >= nu_ref[0])
    def _():
        y_ref[...] = jnp.zeros_like(y_ref)


def _expert_call(buf, block_expert, n_used, wg, wu, wd, layer, blk_off, r_pad, ybuf=None):
    rows, d = buf.shape
    n_part = rows // MOE_BLOCK
    ff = wg.shape[-1]
    aliased = ybuf is not None

    def x_idx(s, be, nu):
        return (jnp.clip(s, 0, jnp.maximum(nu[0] - blk_off - 1, 0)), 0)

    in_specs = [
        pl.BlockSpec((MOE_BLOCK, d), x_idx),
        pl.BlockSpec((None, None, d, ff), lambda s, be, nu: (layer, be[s + blk_off], 0, 0)),
        pl.BlockSpec((None, None, d, ff), lambda s, be, nu: (layer, be[s + blk_off], 0, 0)),
        pl.BlockSpec((None, None, ff, d), lambda s, be, nu: (layer, be[s + blk_off], 0, 0)),
    ]
    args = [block_expert, n_used, buf, wg, wu, wd]
    if aliased:
        in_specs.append(pl.BlockSpec(memory_space=pl.ANY))
        args.append(ybuf)
    grid_spec = pltpu.PrefetchScalarGridSpec(
        num_scalar_prefetch=2,
        grid=(n_part,),
        in_specs=in_specs,
        out_specs=pl.BlockSpec((MOE_BLOCK, d), lambda s, be, nu: (s + blk_off, 0)),
        scratch_shapes=[pltpu.VMEM((d, ff), BF16), pltpu.VMEM((d, ff), BF16),
                        pltpu.VMEM((ff, d), BF16)],
    )
    return pl.pallas_call(
        functools.partial(_expert_kernel, blk_off=blk_off, aliased=aliased),
        grid_spec=grid_spec,
        out_shape=jax.ShapeDtypeStruct((r_pad, d), BF16),
        input_output_aliases={len(args) - 1: 0} if aliased else {},
        compiler_params=_cparams(("arbitrary",)),
        name="moe_experts",
    )(*args)


def _combine_kernel(*refs, with_next, n_prev):
    if with_next:
        y0_ref, y1_ref, gw_ref, x_ref, gt_ref, sh_ref, sc_ref, g_ref = refs[:8]
        xo_ref, hn_ref = refs[8 + n_prev:]
    else:
        y0_ref, y1_ref, gw_ref, x_ref, gt_ref = refs[:5]
        (xo_ref,) = refs[5 + n_prev:]
    gw = gw_ref[...].T
    f = gw[:, 0:1] * y0_ref[...].astype(F32) + gw[:, 1:2] * y1_ref[...].astype(F32)
    xo = x_ref[...] + gt_ref[...] * f
    xo_ref[...] = xo
    if with_next:
        h = _rms(xo) * g_ref[...]
        hn_ref[...] = (h * (1.0 + sc_ref[...]) + sh_ref[...]).astype(BF16)


def _combine_call(y0, y1, gw, x, gt, n_ctx, tm, next_mod, b_off, prev):
    b_, tn, d = x.shape
    bp = y0.shape[0]
    ncb = n_ctx // tm
    part_tok = pl.BlockSpec((None, tm, d), lambda b, j: (b, j, 0))
    tok = pl.BlockSpec((None, tm, d), lambda b, j: (b + b_off, j, 0))
    in_specs = [part_tok, part_tok,
                pl.BlockSpec((None, None, GATE_ROWS, tm), lambda b, j: (b + b_off, j, 0, 0)),
                tok, _mod_spec(d, ncb, b_, b_off)]
    args = [y0, y1, gw, x, gt]
    out_specs = [tok]
    out_shape = [jax.ShapeDtypeStruct((b_, tn, d), F32)]
    with_next = next_mod is not None
    if with_next:
        sh, sc, g = next_mod
        in_specs += [_mod_spec(d, ncb, b_, b_off), _mod_spec(d, ncb, b_, b_off), _resident((1, d))]
        args += [sh, sc, g]
        out_specs.append(tok)
        out_shape.append(jax.ShapeDtypeStruct((b_, tn, d), BF16))
    aliases = {}
    n_prev = 0
    if prev is not None:
        n_prev = len(prev)
        for k, arr in enumerate(prev):
            aliases[len(args)] = k
            in_specs.append(pl.BlockSpec(memory_space=pl.ANY))
            args.append(arr)
    return pl.pallas_call(
        functools.partial(_combine_kernel, with_next=with_next, n_prev=n_prev),
        grid=(bp, tn // tm),
        in_specs=in_specs,
        out_specs=out_specs,
        out_shape=out_shape,
        input_output_aliases=aliases,
        compiler_params=_cparams(("parallel", "arbitrary")),
        name="moe_combine",
    )(*args)


def _mla_in_kernel(h_ref, win_ref, qg_ref, wq_ref, wqr_ref, tqc_ref, tqs_ref,
                   ktab_ref, wk_ref, wv_ref, kgain_ref, q_ref, k_ref, v_ref, *, n_ctx_blocks):
    tm = h_ref.shape[0]
    proj = _dot(h_ref[...], win_ref[...])

    @pl.when(pl.program_id(1) >= n_ctx_blocks)
    def _():
        cq = proj[:, :MLA_Q_LORA]
        qn = (_rms(cq) * qg_ref[...]).astype(BF16)
        qa = _dot(qn, wq_ref[...])
        qr = _dot(qn, wqr_ref[...])
        tqc = tqc_ref[...]
        tqs = tqs_ref[...]
        for h in range(MLA_HEADS):
            hs = slice(h * LANES, (h + 1) * LANES)
            a = qa[:, hs]
            rq = lax.rsqrt(jnp.sum(a * a, axis=-1, keepdims=True) * (1.0 / MLA_QK) + NORM_EPS)
            q_ref[h] = ((a * tqc + qr[:, hs] * tqs) * rq).astype(BF16)

    p2 = proj[:, MLA_Q_LORA:]
    lane = lax.broadcasted_iota(jnp.int32, (tm, p2.shape[1]), 1)
    is_kv = lane < MLA_KV_LORA
    sq = p2 * p2
    ms_kv = jnp.sum(jnp.where(is_kv, sq, 0.0), axis=-1, keepdims=True) * (1.0 / MLA_KV_LORA)
    is_rope = (lane >= MLA_KV_LORA) & (lane < MLA_KV_LORA + MLA_ROPE)
    ss_rope = jnp.sum(jnp.where(is_rope, sq, 0.0), axis=-1, keepdims=True)
    mult = ktab_ref[...] * jnp.where(is_kv, lax.rsqrt(ms_kv + NORM_EPS), 1.0)
    lhs = (p2 * mult).astype(BF16)
    ka = _dot(lhs, wk_ref[...])
    va = _dot(lhs[:, :MLA_KV_LORA], wv_ref[...])
    kgain = kgain_ref[...]
    lane1 = lax.broadcasted_iota(jnp.int32, (tm, LANES), 1)
    for h in range(MLA_HEADS):
        hs = slice(h * LANES, (h + 1) * LANES)
        a = ka[:, hs]
        ssn = jnp.sum(jnp.where(lane1 < MLA_NOPE, a * a, 0.0), axis=-1, keepdims=True)
        rk = lax.rsqrt((ssn + ss_rope) * (1.0 / MLA_QK) + NORM_EPS)
        k_ref[h] = (a * kgain * rk).astype(BF16)
        ones_lane = MLA_V if h % 2 == 0 else 0
        v_ref[h] = jnp.where(lane1 == ones_lane, 1.0, va[:, hs]).astype(BF16)


def _mla_in_call(hm, win, qg, wq, wqr, tqc, tqs, ktab, wk, wv, kgain, n_ctx, tm):
    b_, t_, d = hm.shape
    ncb = n_ctx // tm
    tok = pl.BlockSpec((None, tm, d), lambda b, j: (b, j, 0))
    head = pl.BlockSpec((None, MLA_HEADS, tm, LANES), lambda b, j: (b, 0, j, 0))
    hshape = jax.ShapeDtypeStruct((b_, MLA_HEADS, t_, LANES), BF16)
    qhead = pl.BlockSpec((None, MLA_HEADS, tm, LANES), lambda b, j: (b, 0, jnp.maximum(j - ncb, 0), 0))
    qshape = jax.ShapeDtypeStruct((b_, MLA_HEADS, t_ - n_ctx, LANES), BF16)
    return pl.pallas_call(
        functools.partial(_mla_in_kernel, n_ctx_blocks=ncb),
        grid=(b_, t_ // tm),
        in_specs=[tok, _resident(win.shape), _resident(qg.shape), _resident(wq.shape),
                  _resident(wqr.shape),
                  pl.BlockSpec((tm, LANES), lambda b, j: (j, 0)),
                  pl.BlockSpec((tm, LANES), lambda b, j: (j, 0)),
                  pl.BlockSpec((tm, ktab.shape[1]), lambda b, j: (j, 0)),
                  _resident(wk.shape), _resident(wv.shape), _resident(kgain.shape)],
        out_specs=[qhead, head, head],
        out_shape=[qshape, hshape, hshape],
        compiler_params=_cparams(("parallel", "arbitrary")),
        name="mla_in",
    )(hm, win, qg, wq, wqr, tqc, tqs, ktab, wk, wv, kgain)


def _attn_kernel(q_ref, k_ref, v_ref, o_ref):
    n_heads, tq, _ = q_ref.shape
    t_ = k_ref.shape[1]
    ck = next(c for c in ATT_KEY_CHUNKS if t_ % c == 0)
    outs = []
    for hh in range(n_heads):
        q = q_ref[hh]
        m = jnp.full((tq, 1), -jnp.inf, F32)
        acc = jnp.zeros((tq, LANES), F32)
        for c in range(t_ // ck):
            ks = slice(c * ck, (c + 1) * ck)
            s = _dot_nt(q, k_ref[hh, ks, :])
            m_new = jnp.maximum(m, jnp.max(s, axis=-1, keepdims=True))
            p = jnp.exp2((s - m_new).astype(BF16))
            acc = acc * jnp.exp2(m - m_new) + _dot(p, v_ref[hh, ks, :])
            m = m_new
        ones_lane = MLA_V if hh % 2 == 0 else 0
        outs.append(acc / acc[:, ones_lane:ones_lane + 1])
    lane = lax.broadcasted_iota(jnp.int32, outs[0].shape, 1)
    for pr in range(len(outs) // 2):
        o_ref[:, pr * LANES:(pr + 1) * LANES] = jnp.where(
            lane < MLA_V, outs[2 * pr], outs[2 * pr + 1]).astype(o_ref.dtype)


def _attn_call(q, k, v, tq):
    b_, nh, l_, _ = q.shape
    t_ = k.shape[2]
    hps = ATT_HEADS_PER_STEP
    return pl.pallas_call(
        _attn_kernel,
        grid=(b_, nh // hps, l_ // tq),
        in_specs=[
            pl.BlockSpec((None, hps, tq, LANES), lambda b, h, i: (b, h, i, 0)),
            pl.BlockSpec((None, hps, t_, LANES), lambda b, h, i: (b, h, 0, 0)),
            pl.BlockSpec((None, hps, t_, LANES), lambda b, h, i: (b, h, 0, 0)),
        ],
        out_specs=pl.BlockSpec((None, tq, (hps // 2) * LANES), lambda b, h, i: (b, i, h)),
        out_shape=jax.ShapeDtypeStruct((b_, l_, (nh // 2) * LANES), BF16),
        compiler_params=_cparams(("parallel", "parallel", "arbitrary")),
        name="mla_attention",
    )(q, k, v)


def _dispatch_layout(eid, rank, counts):
    tm = eid.shape[-1]
    n_slot = eid.size
    starts = jnp.cumsum(counts) - counts
    padded = (counts + MOE_BLOCK - 1) // MOE_BLOCK * MOE_BLOCK
    pad_ends = jnp.cumsum(padded)
    pad_starts = pad_ends - padded
    n_blocks = (n_slot + MOE_BLOCK - 1) // MOE_BLOCK + MOE_EXPERTS
    block_start = jnp.arange(n_blocks, dtype=jnp.int32) * MOE_BLOCK
    block_expert = jnp.minimum(
        jnp.sum((block_start[:, None] >= pad_ends[None, :]).astype(jnp.int32), axis=1),
        MOE_EXPERTS - 1)
    n_used = pad_ends[-1:] // MOE_BLOCK
    slot_row = rank
    for e in range(MOE_EXPERTS):
        slot_row = slot_row + jnp.where(eid == e, pad_starts[e], 0)
    order = jnp.argsort(eid.reshape(n_slot)).astype(jnp.int32)
    offs = (block_start - pad_starts[block_expert])[:, None] + jnp.arange(MOE_BLOCK, dtype=jnp.int32)
    valid = offs < counts[block_expert][:, None]
    pos = jnp.minimum(starts[block_expert][:, None] + offs, n_slot - 1)
    slot = jnp.take(order, pos, mode='clip')
    tok = (slot // (MOE_TOP_K * tm)) * tm + slot % tm
    n_tok = n_slot // MOE_TOP_K
    filler = (block_start[:, None] + jnp.arange(MOE_BLOCK, dtype=jnp.int32)) % n_tok
    row_tok = jnp.where(valid, tok, filler)
    return row_tok.reshape(-1), slot_row, block_expert, n_used


def _moe(h2, eid, rank, counts, gw, x, gt, wg, wu, wd, layer, n_ctx, tm, next_mod=None):
    b_, tn, d = h2.shape
    n_tok = b_ * tn
    cnt = counts[MOE_GROUPS:MOE_GROUPS + MOE_EXPERTS, 0].astype(jnp.int32)
    row_tok, slot_row, block_expert, n_used = _dispatch_layout(eid, rank, cnt)
    r_pad = row_tok.shape[0]
    n_blocks = r_pad // MOE_BLOCK
    bounds = [n_blocks * p // EXPERT_PARTS for p in range(EXPERT_PARTS + 1)]
    h2_rows = h2.reshape(n_tok, d)
    ybuf = None
    for lo, hi in zip(bounds[:-1], bounds[1:]):
        part = jnp.take(h2_rows, row_tok[lo * MOE_BLOCK:hi * MOE_BLOCK], axis=0, mode='clip')
        ybuf = _expert_call(part, block_expert, n_used, wg, wu, wd, layer, lo, r_pad, ybuf)
    bp = b_ // COMBINE_PARTS
    outs = None
    for p in range(COMBINE_PARTS):
        rows = slot_row[p * bp:(p + 1) * bp]
        y0 = jnp.take(ybuf, rows[:, :, 0, :].reshape(bp * tn), axis=0, mode='clip').reshape(bp, tn, d)
        y1 = jnp.take(ybuf, rows[:, :, 1, :].reshape(bp * tn), axis=0, mode='clip').reshape(bp, tn, d)
        outs = _combine_call(y0, y1, gw, x, gt, n_ctx, tm, next_mod, p * bp, outs)
    return outs


def _rot_half_perm():
    half = MLA_ROPE // 2
    j = jnp.arange(MLA_ROPE)
    within = j % half
    base = j - within
    src = jnp.where(within < half // 2, base + within + half // 2, base + within - half // 2)
    sign = jnp.where(within < half // 2, -1.0, 1.0).astype(F32)
    return src, sign


def _rope_tables(n_ctx, seq):
    rows = seq // GRID_W
    row = jnp.repeat(jnp.arange(rows), GRID_W)
    col = jnp.tile(jnp.arange(GRID_W), rows)
    half = MLA_ROPE // 2
    inv_freq = ROPE_THETA ** (-jnp.arange(0, half, 2, dtype=F32) / half)
    ang = jnp.stack([row, col], axis=-1).astype(F32)[..., None] * inv_freq
    ang = jnp.concatenate([ang, ang], axis=-1).reshape(seq, MLA_ROPE)
    cos = jnp.concatenate([jnp.ones((n_ctx, MLA_ROPE), F32), jnp.cos(ang)], axis=0)
    sin = jnp.concatenate([jnp.zeros((n_ctx, MLA_ROPE), F32), jnp.sin(ang)], axis=0)
    return cos, sin


def _mla_prepare(w_in, w_qb, w_kvb, q_qk_g, k_qk_g, kv_norm_g, n_ctx, seq):
    d = w_in.shape[0]
    src, sign = _rot_half_perm()
    cos, sin = _rope_tables(n_ctx, seq)
    t_ = n_ctx + seq

    rope0 = MLA_Q_LORA + MLA_KV_LORA
    w_rope = w_in[:, rope0:rope0 + MLA_ROPE]
    win = jnp.concatenate(
        [w_in, w_rope[:, src] * sign, jnp.zeros((d, 512 - rope0 - 2 * MLA_ROPE), F32)], axis=1)

    wq3 = w_qb.reshape(MLA_Q_LORA, MLA_HEADS, MLA_QK)
    pad = jnp.zeros((MLA_Q_LORA, MLA_HEADS, LANES - MLA_QK), F32)
    wq = jnp.concatenate([wq3, pad], axis=-1).reshape(MLA_Q_LORA, MLA_HEADS * LANES)
    wq_rot = wq3[:, :, MLA_NOPE:][:, :, src] * sign
    wqr = jnp.concatenate([jnp.zeros((MLA_Q_LORA, MLA_HEADS, MLA_NOPE), F32), wq_rot, pad],
                          axis=-1).reshape(MLA_Q_LORA, MLA_HEADS * LANES)

    scale = MLA_QK ** -0.5 * 1.4426950408889634
    gq_n, gq_r = q_qk_g[:MLA_NOPE], q_qk_g[MLA_NOPE:]
    zpad = jnp.zeros((t_, LANES - MLA_QK), F32)
    tqc = jnp.concatenate([jnp.broadcast_to(gq_n, (t_, MLA_NOPE)), gq_r * cos, zpad], axis=1) * scale
    tqs = jnp.concatenate([jnp.zeros((t_, MLA_NOPE), F32), gq_r[src] * sin, zpad], axis=1) * scale

    wkv3 = w_kvb.reshape(MLA_KV_LORA, MLA_HEADS, MLA_NOPE + MLA_V)
    wk_lat = jnp.concatenate(
        [wkv3[:, :, :MLA_NOPE], jnp.zeros((MLA_KV_LORA, MLA_HEADS, LANES - MLA_NOPE), F32)], axis=-1)
    place = jnp.concatenate([jnp.zeros((MLA_ROPE, MLA_NOPE), F32), jnp.eye(MLA_ROPE, dtype=F32),
                             jnp.zeros((MLA_ROPE, LANES - MLA_QK), F32)], axis=1)
    place = jnp.broadcast_to(place[:, None, :], (MLA_ROPE, MLA_HEADS, LANES))
    wk = jnp.concatenate([wk_lat, place, place,
                          jnp.zeros((256 - MLA_KV_LORA - 2 * MLA_ROPE, MLA_HEADS, LANES), F32)],
                         axis=0).reshape(256, MLA_HEADS * LANES)
    gk_n, gk_r = k_qk_g[:MLA_NOPE], k_qk_g[MLA_NOPE:]
    ktab = jnp.concatenate([jnp.broadcast_to(kv_norm_g, (t_, MLA_KV_LORA)), gk_r * cos,
                            gk_r[src] * sin, jnp.zeros((t_, 256 - MLA_KV_LORA - 2 * MLA_ROPE), F32)],
                           axis=1)
    kgain = jnp.concatenate([gk_n, jnp.ones((MLA_ROPE,), F32),
                             jnp.zeros((LANES - MLA_QK,), F32)]).reshape(1, LANES)

    wv_h = wkv3[:, :, MLA_NOPE:]
    zv = jnp.zeros_like(wv_h)
    odd = (jnp.arange(MLA_HEADS) % 2 == 1)[None, :, None]
    wv = jnp.concatenate([jnp.where(odd, zv, wv_h), jnp.where(odd, wv_h, zv)],
                         axis=-1).reshape(MLA_KV_LORA, MLA_HEADS * LANES)
    return (win.astype(BF16), wq.astype(BF16), wqr.astype(BF16), tqc, tqs, ktab,
            wk.astype(BF16), wv.astype(BF16), kgain)


def _router_weights(w_group, w_expert):
    d = w_group.shape[0]
    wt = jnp.concatenate([w_group, w_expert], axis=1).T
    wt = jnp.concatenate([wt, jnp.zeros((ROUTER_ROWS - wt.shape[0], d), F32)], axis=0)
    head = wt.astype(BF16)
    rest = (wt - head.astype(F32)).astype(BF16)
    return jnp.concatenate([head, rest], axis=0)


def kernel(x, c, ctx, c_ctx, ada_w, ada_b, norm_mix_g, norm_ffn_g, hg_w_in, hg_lower_bounds, hg_out_norm_g, hg_w_out, mla_w_in, mla_q_norm_g, mla_kv_norm_g, mla_w_qb, mla_w_kvb, mla_q_qknorm_g, mla_k_qknorm_g, mla_w_out, moe_w_group, moe_w_expert, moe_w_gate, moe_w_up, moe_w_down):
    b_, seq, d = x.shape
    n_ctx = ctx.shape[1]
    tm = min(ROW_TILE, n_ctx)
    tq = min(ATT_Q_TILE, seq)
    assert n_ctx % GLA_BLOCK == 0 and seq % GLA_BLOCK == 0 and n_ctx % tm == 0 and seq % tm == 0
    assert b_ + 1 <= ADA_ROWS and seq % GRID_W == 0
    assert b_ % min(GLA_BATCH, b_) == 0 and b_ % COMBINE_PARTS == 0

    cc = jnp.concatenate([c, c_ctx[None, :], jnp.zeros((ADA_ROWS - b_ - 1, d), F32)], axis=0)
    mods = _ada_call(cc, ada_w, ada_b).reshape(DEPTH, ADA_CHUNKS, ADA_ROWS, 1, d)

    def mod(i, chunk):
        return mods[i, chunk]

    row = lambda v: v.reshape(1, -1)

    lower = jnp.cumsum(jax.nn.softmax(hg_lower_bounds.astype(F32), axis=1), axis=1)[:, 0]
    qs, kf, gf, kb, gb, v, gs = _hg_in_call(
        ctx, x, mod(0, 0), mod(0, 1), row(norm_mix_g[0]), hg_w_in[0].astype(BF16), lower, tm)
    o_b = _gla_call(qs, kb, gb, v, n_ctx, reverse=True)
    og = _gla_call(qs, kf, gf, v, n_ctx, reverse=False,
                   final_args=(o_b, gs, row(hg_out_norm_g[0])))
    x1, h2, eid, gw, rank, cnt = _outproj_call(
        og, hg_w_out[0].astype(BF16), x, 0, ctx, mod(0, 2), mod(0, 3), mod(0, 4),
        row(norm_ffn_g[0]), _router_weights(moe_w_group[0], moe_w_expert[0]), n_ctx, tm)
    x2, hm = _moe(h2, eid, rank, cnt, gw, x1, mod(0, 5), moe_w_gate, moe_w_up, moe_w_down, 0,
                  n_ctx, tm, next_mod=(mod(1, 0), mod(1, 1), row(norm_mix_g[1])))

    win, wq, wqr, tqc, tqs, ktab, wk, wv, kgain = _mla_prepare(
        mla_w_in[0], mla_w_qb[0], mla_w_kvb[0], mla_q_qknorm_g[0], mla_k_qknorm_g[0],
        mla_kv_norm_g[0], n_ctx, seq)
    q, k, vv = _mla_in_call(hm, win, row(mla_q_norm_g[0]), wq, wqr, tqc, tqs, ktab, wk, wv, kgain,
                            n_ctx, tm)
    o_att = _attn_call(q, k, vv, tq)
    x3, h2, eid, gw, rank, cnt = _outproj_call(
        o_att, mla_w_out[0].astype(BF16), x2, n_ctx, None, mod(1, 2), mod(1, 3), mod(1, 4),
        row(norm_ffn_g[1]), _router_weights(moe_w_group[1], moe_w_expert[1]), 0, tm)
    (x4,) = _moe(h2, eid, rank, cnt, gw, x3, mod(1, 5), moe_w_gate, moe_w_up, moe_w_down, 1, 0, tm)
    return x4
```

```python
import functools

import jax
import jax.numpy as jnp
from jax import lax
from jax.experimental import pallas as pl
from jax.experimental.pallas import tpu as pltpu

F32 = jnp.float32
BF16 = jnp.bfloat16

DEPTH = 2
ADA_CHUNKS = 6
NORM_EPS = 1e-6
GRID_W = 64
HG_HEADS = 8
HG_CHUNK = 32
MLA_HEADS = 16
MLA_Q_LORA = 256
MLA_KV_LORA = 128
MLA_NOPE = 64
MLA_ROPE = 32
MLA_V = 64
MLA_QK = MLA_NOPE + MLA_ROPE
ROPE_THETA = 10000.0
MOE_GROUPS = 4
MOE_EPG = 8
MOE_EXPERTS = MOE_GROUPS * MOE_EPG
MOE_TOP_K = 2
MOE_FF = 512

LANES = 128
ROUTER_ROWS = 48
GATE_ROWS = 8
VMEM_LIMIT = 56 * 1024 * 1024

ROW_TILE = 256
GLA_BLOCK = 128
GLA_BATCH = 8
MOE_BLOCK = 512
ATT_Q_TILE = 512
ATT_HEADS_PER_STEP = 8
ATT_KEY_CHUNKS = (256, 128)
EXPERT_PARTS = 4
COMBINE_PARTS = 2
ADA_ROWS = 40


def _cparams(sem):
    return pltpu.CompilerParams(dimension_semantics=sem, vmem_limit_bytes=VMEM_LIMIT)


def _resident(shape):
    nd = len(shape)
    return pl.BlockSpec(shape, lambda *_: (0,) * nd, pipeline_mode=pl.Buffered(1))


def _dot(a, b):
    return jnp.dot(a, b, preferred_element_type=F32)


def _dot_nt(a, b):
    return lax.dot_general(a, b, (((1,), (1,)), ((), ())), preferred_element_type=F32)


def _dot_tn(a, b):
    return lax.dot_general(a, b, (((0,), (0,)), ((), ())), preferred_element_type=F32)


def _sigmoid(x):
    return 1.0 / (1.0 + jnp.exp(-x))


def _rms(x):
    return x * lax.rsqrt(jnp.mean(x * x, axis=-1, keepdims=True) + NORM_EPS)


def _ada_kernel(c_ref, w_ref, b_ref, o_ref):
    c = c_ref[...]
    a = (c * _sigmoid(c)).astype(BF16)
    o_ref[...] = _dot(a, w_ref[...].astype(BF16)) + b_ref[...]


def _ada_call(cc, ada_w, ada_b):
    depth, d, _ = ada_w.shape
    rows = cc.shape[0]
    return pl.pallas_call(
        _ada_kernel,
        grid=(depth, ADA_CHUNKS),
        in_specs=[
            pl.BlockSpec((rows, d), lambda i, j: (0, 0)),
            pl.BlockSpec((None, d, d), lambda i, j: (i, 0, j)),
            pl.BlockSpec((None, 1, d), lambda i, j: (i, 0, j)),
        ],
        out_specs=pl.BlockSpec((None, None, rows, d), lambda i, j: (i, j, 0, 0)),
        out_shape=jax.ShapeDtypeStruct((depth, ADA_CHUNKS, rows, d), F32),
        compiler_params=_cparams(("arbitrary", "arbitrary")),
        name="ada_mod",
    )(cc, ada_w, ada_b.reshape(depth, 1, ADA_CHUNKS * d))


def _mod_spec(d, n_ctx_blocks, ctx_row, b_off=0):
    def idx(b, j):
        return (jnp.where(j < n_ctx_blocks, ctx_row, b + b_off), 0, 0)
    return pl.BlockSpec((None, 1, d), idx)


def _hg_in_kernel(xc_ref, x_ref, sh_ref, sc_ref, g_ref, w_ref, lb_ref,
                  q_ref, kf_ref, gf_ref, kb_ref, gb_ref, v_ref, gs_ref, *, n_ctx_blocks):
    d = x_ref.shape[-1]
    x = jnp.where(pl.program_id(1) < n_ctx_blocks, xc_ref[...], x_ref[...])
    h = _rms(x) * g_ref[...]
    h = (h * (1.0 + sc_ref[...]) + sh_ref[...]).astype(BF16)

    def proj(c):
        return _dot(h, w_ref[:, c * d:(c + 1) * d])

    p = proj(0)
    q_ref[...] = (p * _sigmoid(p)).astype(BF16)
    for c, k_ref, lg_ref in ((1, kf_ref, gf_ref), (2, kb_ref, gb_ref)):
        s = _sigmoid(proj(c))
        lb = lb_ref[c - 1:c, :]
        k_ref[...] = ((1.0 - lb) * (1.0 - s)).astype(BF16)
        lg_ref[...] = jnp.log(lb + (1.0 - lb) * s).astype(BF16)
    v_ref[...] = proj(3).astype(BF16)
    p = proj(4)
    gs_ref[...] = (p * _sigmoid(p)).astype(BF16)


def _hg_in_call(ctx, x, sh, sc, g, w_in, lb, tm):
    b_, n_ctx, d = ctx.shape
    t_ = n_ctx + x.shape[1]
    ncb = n_ctx // tm
    tok = pl.BlockSpec((None, tm, d), lambda b, j: (b, j, 0))
    out = jax.ShapeDtypeStruct((b_, t_, d), BF16)
    return pl.pallas_call(
        functools.partial(_hg_in_kernel, n_ctx_blocks=ncb),
        grid=(b_, t_ // tm),
        in_specs=[pl.BlockSpec((None, tm, d), lambda b, j: (b, jnp.minimum(j, ncb - 1), 0)),
                  pl.BlockSpec((None, tm, d), lambda b, j: (b, jnp.maximum(j - ncb, 0), 0)),
                  _mod_spec(d, ncb, b_), _mod_spec(d, ncb, b_),
                  _resident((1, d)), _resident(w_in.shape), _resident((2, d))],
        out_specs=[tok] * 7,
        out_shape=[out] * 7,
        compiler_params=_cparams(("parallel", "arbitrary")),
        name="hg_in",
    )(ctx, x, sh, sc, g, w_in, lb)


def _gla_kernel(*refs, reverse, final, n_heads):
    if final:
        q_ref, k_ref, g_ref, v_ref, ob_ref, gs_ref, gn_ref, o_ref, st_ref = refs
    else:
        q_ref, k_ref, g_ref, v_ref, o_ref, st_ref = refs
    nb, tb, d = q_ref.shape
    dh = d // n_heads
    n_chunks = tb // HG_CHUNK

    @pl.when(pl.program_id(1) == 0)
    def _():
        st_ref[...] = jnp.zeros_like(st_ref)

    row = lax.broadcasted_iota(jnp.int32, (tb, tb), 0)
    col = lax.broadcasted_iota(jnp.int32, (tb, tb), 1)
    same_chunk = (row // HG_CHUNK) == (col // HG_CHUNK)
    tri = same_chunk & ((col >= row) if reverse else (col <= row))
    cum = jnp.where(tri, 1.0, 0.0).astype(BF16)
    chunk_order = range(n_chunks - 1, -1, -1) if reverse else range(n_chunks)

    qd, ki_b, e_end, k_end = [], [], [], []
    for n in range(nb):
        b = _dot(cum, g_ref[n])
        qd.append((q_ref[n].astype(F32) * jnp.exp(b)).astype(BF16))
        ki = k_ref[n].astype(F32) * jnp.exp(-b)
        ki_b.append(ki.astype(BF16))
        ee, ke = {}, {}
        for c in range(n_chunks):
            r0 = c * HG_CHUNK
            last = r0 if reverse else r0 + HG_CHUNK - 1
            ee[c] = jnp.exp(b[last:last + 1, :])
            ke[c] = (ki[r0:r0 + HG_CHUNK, :] * ee[c]).astype(BF16)
        e_end.append(ee)
        k_end.append(ke)

    chains = [(n, h) for n in range(nb) for h in range(n_heads)]
    hsl = [slice(h * dh, (h + 1) * dh) for h in range(n_heads)]
    scores = {(n, h): jnp.where(tri, _dot_nt(qd[n][:, hsl[h]], ki_b[n][:, hsl[h]]), 0.0).astype(BF16)
              for n, h in chains}
    o_intra = {(n, h): _dot(scores[n, h], v_ref[n, :, hsl[h]]) for n, h in chains}
    kv = {(n, h, c): _dot_tn(v_ref[n, c * HG_CHUNK:(c + 1) * HG_CHUNK, hsl[h]], k_end[n][c][:, hsl[h]])
          for n, h in chains for c in range(n_chunks)}
    st = {(n, h): st_ref[n, h] for n, h in chains}
    for c in chunk_order:
        rs = slice(c * HG_CHUNK, (c + 1) * HG_CHUNK)
        for n, h in chains:
            hs = hsl[h]
            o_c = o_intra[n, h][rs, :] + _dot_nt(qd[n][rs, hs], st[n, h].astype(BF16))
            st[n, h] = st[n, h] * e_end[n][c][:, hs] + kv[n, h, c]
            if final:
                o_c = o_c + ob_ref[n, rs, hs].astype(F32)
                o_c = _rms(o_c) * gn_ref[...]
                o_c = o_c * gs_ref[n, rs, hs].astype(F32)
            o_ref[n, rs, hs] = o_c.astype(o_ref.dtype)
    for n, h in chains:
        st_ref[n, h] = st[n, h]


def _gla_call(q, k, g, v, n_ctx, reverse, final_args=None):
    b_, t_, d = q.shape
    tb = GLA_BLOCK
    gb = min(GLA_BATCH, b_)
    nb = t_ // tb
    ncb = n_ctx // tb
    dh = d // HG_HEADS

    def blk(b, j):
        if reverse:
            jj = jnp.where(j < ncb, ncb - 1 - j, nb - 1 - (j - ncb))
        else:
            jj = j
        return (b, jj, 0)

    tok = pl.BlockSpec((gb, tb, d), blk)
    in_specs = [tok] * 4
    args = [q, k, g, v]
    final = final_args is not None
    if final:
        o_b, gs, gn = final_args
        in_specs += [tok, tok, _resident((1, dh))]
        args += [o_b, gs, gn]
    return pl.pallas_call(
        functools.partial(_gla_kernel, reverse=reverse, final=final, n_heads=HG_HEADS),
        grid=(b_ // gb, nb),
        in_specs=in_specs,
        out_specs=tok,
        out_shape=jax.ShapeDtypeStruct((b_, t_, d), BF16),
        scratch_shapes=[pltpu.VMEM((gb, HG_HEADS, dh, dh), F32)],
        compiler_params=_cparams(("parallel", "arbitrary")),
        name="gla_bwd" if reverse else "gla_fwd",
    )(*args)


def _outproj_kernel(*refs, n_ctx_blocks, split_residual):
    if split_residual:
        (a_ref, w_ref, xc_ref, x_ref, gt_ref, sh_ref, sc_ref, g_ref, wr_ref,
         xn_ref, h2_ref, eid_ref, gw_ref, rank_ref, cnt_ref, run_ref) = refs
    else:
        (a_ref, w_ref, x_ref, gt_ref, sh_ref, sc_ref, g_ref, wr_ref,
         xn_ref, h2_ref, eid_ref, gw_ref, rank_ref, cnt_ref, run_ref) = refs
    tm = a_ref.shape[0]
    nr = ROUTER_ROWS
    first = (pl.program_id(0) == 0) & (pl.program_id(1) == 0)

    @pl.when(first)
    def _():
        run_ref[...] = jnp.zeros_like(run_ref)

    x = x_ref[...]
    if split_residual:
        x = jnp.where(pl.program_id(1) < n_ctx_blocks, xc_ref[...], x)
    y = _dot(a_ref[...], w_ref[...])
    xn = x + gt_ref[...] * y
    xn_ref[...] = xn
    h2 = _rms(xn) * g_ref[...]
    h2 = h2 * (1.0 + sc_ref[...]) + sh_ref[...]
    h2b = h2.astype(BF16)
    h2_ref[...] = h2b
    h2r = (h2 - h2b.astype(F32)).astype(BF16)
    lg2 = _dot_nt(wr_ref[...], h2b) + _dot_nt(wr_ref[...], h2r)
    lg = lg2[:nr] + lg2[nr:]

    ridx = lax.broadcasted_iota(jnp.int32, lg.shape, 0)
    neg = -jnp.inf
    gl = jnp.where(ridx < MOE_GROUPS, lg, neg)
    gmax = jnp.max(gl, axis=0, keepdims=True)
    g_idx = jnp.min(jnp.where(gl == gmax, ridx, nr), axis=0, keepdims=True)
    p_group = 1.0 / jnp.sum(jnp.exp(gl - gmax), axis=0, keepdims=True)
    lo = MOE_GROUPS + MOE_EPG * g_idx
    el = jnp.where((ridx >= lo) & (ridx < lo + MOE_EPG), lg, neg)
    m1 = jnp.max(el, axis=0, keepdims=True)
    i1 = jnp.min(jnp.where(el == m1, ridx, nr), axis=0, keepdims=True)
    el2 = jnp.where(ridx == i1, neg, el)
    m2 = jnp.max(el2, axis=0, keepdims=True)
    i2 = jnp.min(jnp.where(el2 == m2, ridx, nr), axis=0, keepdims=True)
    r21 = jnp.exp(m2 - m1)
    w1 = p_group / (1.0 + r21)
    w2 = w1 * r21

    hot1 = ridx == i1
    hot2 = ridx == i2
    f1 = jnp.where(hot1, 1.0, 0.0)
    f2 = jnp.where(hot2, 1.0, 0.0)
    r = lax.broadcasted_iota(jnp.int32, (tm, tm), 0)
    c = lax.broadcasted_iota(jnp.int32, (tm, tm), 1)
    earlier = jnp.where(r < c, 1.0, 0.0).astype(BF16)
    pre = _dot(jnp.concatenate([f1, f2], axis=0).astype(BF16), earlier)
    run = run_ref[:, 0:1]
    n1 = jnp.sum(f1, axis=1, keepdims=True)
    n2 = jnp.sum(f2, axis=1, keepdims=True)
    rank1 = jnp.sum(jnp.where(hot1, pre[:nr] + run, 0.0), axis=0, keepdims=True)
    rank2 = jnp.sum(jnp.where(hot2, pre[nr:] + (run + n1), 0.0), axis=0, keepdims=True)
    run = jnp.broadcast_to(run + n1 + n2, run_ref.shape)
    run_ref[...] = run
    cnt_ref[...] = run

    eid_ref[...] = jnp.concatenate([i1, i2], axis=0) - MOE_GROUPS
    rank_ref[...] = jnp.concatenate([rank1, rank2], axis=0).astype(jnp.int32)
    gw_ref[...] = jnp.concatenate([w1, w2, jnp.zeros((GATE_ROWS - MOE_TOP_K, tm), F32)], axis=0)


def _outproj_call(a, w, res_lat, lat_off, res_ctx, gt, sh, sc, g, wr, n_ctx, tm):
    b_, tn, d = a.shape
    ncb = n_ctx // tm
    nt = tn // tm
    lob = lat_off // tm
    split = res_ctx is not None
    tok = pl.BlockSpec((None, tm, d), lambda b, j: (b, j, 0))
    pair = pl.BlockSpec((None, None, MOE_TOP_K, tm), lambda b, j: (b, j, 0, 0))
    res_specs, res_args = [], []
    if split:
        res_specs.append(pl.BlockSpec((None, tm, d), lambda b, j: (b, jnp.minimum(j, ncb - 1), 0)))
        res_args.append(res_ctx)
        res_specs.append(pl.BlockSpec((None, tm, d), lambda b, j: (b, jnp.maximum(j - ncb, 0) + lob, 0)))
    else:
        res_specs.append(pl.BlockSpec((None, tm, d), lambda b, j: (b, j + lob, 0)))
    res_args.append(res_lat)
    return pl.pallas_call(
        functools.partial(_outproj_kernel, n_ctx_blocks=ncb, split_residual=split),
        grid=(b_, nt),
        in_specs=[tok, _resident(w.shape)] + res_specs + [
            _mod_spec(d, ncb, b_), _mod_spec(d, ncb, b_), _mod_spec(d, ncb, b_),
            _resident((1, d)), _resident(wr.shape)],
        out_specs=[tok, tok, pair,
                   pl.BlockSpec((None, None, GATE_ROWS, tm), lambda b, j: (b, j, 0, 0)), pair,
                   pl.BlockSpec((ROUTER_ROWS, LANES), lambda b, j: (0, 0))],
        out_shape=[jax.ShapeDtypeStruct((b_, tn, d), F32),
                   jax.ShapeDtypeStruct((b_, tn, d), BF16),
                   jax.ShapeDtypeStruct((b_, nt, MOE_TOP_K, tm), jnp.int32),
                   jax.ShapeDtypeStruct((b_, nt, GATE_ROWS, tm), F32),
                   jax.ShapeDtypeStruct((b_, nt, MOE_TOP_K, tm), jnp.int32),
                   jax.ShapeDtypeStruct((ROUTER_ROWS, LANES), F32)],
        scratch_shapes=[pltpu.VMEM((ROUTER_ROWS, LANES), F32)],
        compiler_params=_cparams(("arbitrary", "arbitrary")),
        name="outproj_ffnmod",
    )(a, w, *res_args, gt, sh, sc, g, wr)


def _expert_kernel(*refs, blk_off, aliased):
    if aliased:
        be_ref, nu_ref, x_ref, wg_ref, wu_ref, wd_ref, _, y_ref, wgb_ref, wub_ref, wdb_ref = refs
    else:
        be_ref, nu_ref, x_ref, wg_ref, wu_ref, wd_ref, y_ref, wgb_ref, wub_ref, wdb_ref = refs
    step = pl.program_id(0)
    i = step + blk_off
    new_expert = (step == 0) | (be_ref[i] != be_ref[jnp.maximum(i - 1, 0)])

    @pl.when(new_expert)
    def _():
        wgb_ref[...] = wg_ref[...].astype(BF16)
        wub_ref[...] = wu_ref[...].astype(BF16)
        wdb_ref[...] = wd_ref[...].astype(BF16)

    @pl.when(i < nu_ref[0])
    def _():
        rows = x_ref.shape[0] // 2
        halves = [slice(j * rows, (j + 1) * rows) for j in range(2)]
        gates = [_dot(x_ref[r, :], wgb_ref[...]) for r in halves]
        ups = [_dot(x_ref[r, :], wub_ref[...]) for r in halves]
        for j, r in enumerate(halves):
            act = (gates[j] * _sigmoid(gates[j]) * ups[j]).astype(BF16)
            y_ref[r, :] = _dot(act, wdb_ref[...]).astype(y_ref.dtype)

    @pl.when(i >= nu_ref[0])
    def _():
        y_ref[...] = jnp.zeros_like(y_ref)


def _expert_call(buf, block_expert, n_used, wg, wu, wd, layer, blk_off, r_pad, ybuf=None):
    rows, d = buf.shape
    n_part = rows // MOE_BLOCK
    ff = wg.shape[-1]
    aliased = ybuf is not None

    def x_idx(s, be, nu):
        return (jnp.clip(s, 0, jnp.maximum(nu[0] - blk_off - 1, 0)), 0)

    in_specs = [
        pl.BlockSpec((MOE_BLOCK, d), x_idx),
        pl.BlockSpec((None, None, d, ff), lambda s, be, nu: (layer, be[s + blk_off], 0, 0)),
        pl.BlockSpec((None, None, d, ff), lambda s, be, nu: (layer, be[s + blk_off], 0, 0)),
        pl.BlockSpec((None, None, ff, d), lambda s, be, nu: (layer, be[s + blk_off], 0, 0)),
    ]
    args = [block_expert, n_used, buf, wg, wu, wd]
    if aliased:
        in_specs.append(pl.BlockSpec(memory_space=pl.ANY))
        args.append(ybuf)
    grid_spec = pltpu.PrefetchScalarGridSpec(
        num_scalar_prefetch=2,
        grid=(n_part,),
        in_specs=in_specs,
        out_specs=pl.BlockSpec((MOE_BLOCK, d), lambda s, be, nu: (s + blk_off, 0)),
        scratch_shapes=[pltpu.VMEM((d, ff), BF16), pltpu.VMEM((d, ff), BF16),
                        pltpu.VMEM((ff, d), BF16)],
    )
    return pl.pallas_call(
        functools.partial(_expert_kernel, blk_off=blk_off, aliased=aliased),
        grid_spec=grid_spec,
        out_shape=jax.ShapeDtypeStruct((r_pad, d), BF16),
        input_output_aliases={len(args) - 1: 0} if aliased else {},
        compiler_params=_cparams(("arbitrary",)),
        name="moe_experts",
    )(*args)


def _combine_kernel(*refs, with_next, n_prev):
    if with_next:
        y0_ref, y1_ref, gw_ref, x_ref, gt_ref, sh_ref, sc_ref, g_ref = refs[:8]
        xo_ref, hn_ref = refs[8 + n_prev:]
    else:
        y0_ref, y1_ref, gw_ref, x_ref, gt_ref = refs[:5]
        (xo_ref,) = refs[5 + n_prev:]
    gw = gw_ref[...].T
    f = gw[:, 0:1] * y0_ref[...].astype(F32) + gw[:, 1:2] * y1_ref[...].astype(F32)
    xo = x_ref[...] + gt_ref[...] * f
    xo_ref[...] = xo
    if with_next:
        h = _rms(xo) * g_ref[...]
        hn_ref[...] = (h * (1.0 + sc_ref[...]) + sh_ref[...]).astype(BF16)


def _combine_call(y0, y1, gw, x, gt, n_ctx, tm, next_mod, b_off, prev):
    b_, tn, d = x.shape
    bp = y0.shape[0]
    ncb = n_ctx // tm
    part_tok = pl.BlockSpec((None, tm, d), lambda b, j: (b, j, 0))
    tok = pl.BlockSpec((None, tm, d), lambda b, j: (b + b_off, j, 0))
    in_specs = [part_tok, part_tok,
                pl.BlockSpec((None, None, GATE_ROWS, tm), lambda b, j: (b + b_off, j, 0, 0)),
                tok, _mod_spec(d, ncb, b_, b_off)]
    args = [y0, y1, gw, x, gt]
    out_specs = [tok]
    out_shape = [jax.ShapeDtypeStruct((b_, tn, d), F32)]
    with_next = next_mod is not None
    if with_next:
        sh, sc, g = next_mod
        in_specs += [_mod_spec(d, ncb, b_, b_off), _mod_spec(d, ncb, b_, b_off), _resident((1, d))]
        args += [sh, sc, g]
        out_specs.append(tok)
        out_shape.append(jax.ShapeDtypeStruct((b_, tn, d), BF16))
    aliases = {}
    n_prev = 0
    if prev is not None:
        n_prev = len(prev)
        for k, arr in enumerate(prev):
            aliases[len(args)] = k
            in_specs.append(pl.BlockSpec(memory_space=pl.ANY))
            args.append(arr)
    return pl.pallas_call(
        functools.partial(_combine_kernel, with_next=with_next, n_prev=n_prev),
        grid=(bp, tn // tm),
        in_specs=in_specs,
        out_specs=out_specs,
        out_shape=out_shape,
        input_output_aliases=aliases,
        compiler_params=_cparams(("parallel", "arbitrary")),
        name="moe_combine",
    )(*args)


def _mla_in_kernel(h_ref, win_ref, qg_ref, wq_ref, wqr_ref, tqc_ref, tqs_ref,
                   ktab_ref, wk_ref, wv_ref, kgain_ref, q_ref, k_ref, v_ref):
    tm = h_ref.shape[0]
    proj = _dot(h_ref[...], win_ref[...])

    cq = proj[:, :MLA_Q_LORA]
    qn = (_rms(cq) * qg_ref[...]).astype(BF16)
    p2 = proj[:, MLA_Q_LORA:]
    lane = lax.broadcasted_iota(jnp.int32, (tm, p2.shape[1]), 1)
    is_kv = lane < MLA_KV_LORA
    sq = p2 * p2
    ms_kv = jnp.sum(jnp.where(is_kv, sq, 0.0), axis=-1, keepdims=True) * (1.0 / MLA_KV_LORA)
    is_rope = (lane >= MLA_KV_LORA) & (lane < MLA_KV_LORA + MLA_ROPE)
    ss_rope = jnp.sum(jnp.where(is_rope, sq, 0.0), axis=-1, keepdims=True)
    mult = ktab_ref[...] * jnp.where(is_kv, lax.rsqrt(ms_kv + NORM_EPS), 1.0)
    lhs = (p2 * mult).astype(BF16)

    qa = _dot(qn, wq_ref[...])
    qr = _dot(qn, wqr_ref[...])
    ka = _dot(lhs, wk_ref[...])
    va = _dot(lhs[:, :MLA_KV_LORA], wv_ref[...])
    tqc = tqc_ref[...]
    tqs = tqs_ref[...]
    kgain = kgain_ref[...]
    lane1 = lax.broadcasted_iota(jnp.int32, (tm, LANES), 1)
    for h in range(MLA_HEADS):
        hs = slice(h * LANES, (h + 1) * LANES)
        a = qa[:, hs]
        rq = lax.rsqrt(jnp.sum(a * a, axis=-1, keepdims=True) * (1.0 / MLA_QK) + NORM_EPS)
        q_ref[h] = ((a * tqc + qr[:, hs] * tqs) * rq).astype(BF16)
        a = ka[:, hs]
        ssn = jnp.sum(jnp.where(lane1 < MLA_NOPE, a * a, 0.0), axis=-1, keepdims=True)
        rk = lax.rsqrt((ssn + ss_rope) * (1.0 / MLA_QK) + NORM_EPS)
        k_ref[h] = (a * kgain * rk).astype(BF16)
        ones_lane = MLA_V if h % 2 == 0 else 0
        v_ref[h] = jnp.where(lane1 == ones_lane, 1.0, va[:, hs]).astype(BF16)


def _mla_in_call(hm, win, qg, wq, wqr, tqc, tqs, ktab, wk, wv, kgain, n_ctx, tm):
    b_, t_, d = hm.shape
    ncb = n_ctx // tm
    tok = pl.BlockSpec((None, tm, d), lambda b, j: (b, j, 0))
    head = pl.BlockSpec((None, MLA_HEADS, tm, LANES), lambda b, j: (b, 0, j, 0))
    hshape = jax.ShapeDtypeStruct((b_, MLA_HEADS, t_, LANES), BF16)
    qhead = pl.BlockSpec((None, MLA_HEADS, tm, LANES), lambda b, j: (b, 0, jnp.maximum(j - ncb, 0), 0))
    qshape = jax.ShapeDtypeStruct((b_, MLA_HEADS, t_ - n_ctx, LANES), BF16)
    return pl.pallas_call(
        _mla_in_kernel,
        grid=(b_, t_ // tm),
        in_specs=[tok, _resident(win.shape), _resident(qg.shape), _resident(wq.shape),
                  _resident(wqr.shape),
                  pl.BlockSpec((tm, LANES), lambda b, j: (j, 0)),
                  pl.BlockSpec((tm, LANES), lambda b, j: (j, 0)),
                  pl.BlockSpec((tm, ktab.shape[1]), lambda b, j: (j, 0)),
                  _resident(wk.shape), _resident(wv.shape), _resident(kgain.shape)],
        out_specs=[qhead, head, head],
        out_shape=[qshape, hshape, hshape],
        compiler_params=_cparams(("parallel", "arbitrary")),
        name="mla_in",
    )(hm, win, qg, wq, wqr, tqc, tqs, ktab, wk, wv, kgain)


def _attn_kernel(q_ref, k_ref, v_ref, o_ref):
    n_heads, tq, _ = q_ref.shape
    t_ = k_ref.shape[1]
    ck = next(c for c in ATT_KEY_CHUNKS if t_ % c == 0)
    outs = []
    for hh in range(n_heads):
        q = q_ref[hh]
        m = jnp.full((tq, 1), -jnp.inf, F32)
        acc = jnp.zeros((tq, LANES), F32)
        for c in range(t_ // ck):
            ks = slice(c * ck, (c + 1) * ck)
            s = _dot_nt(q, k_ref[hh, ks, :])
            m_new = jnp.maximum(m, jnp.max(s, axis=-1, keepdims=True))
            p = jnp.exp2((s - m_new).astype(BF16))
            acc = acc * jnp.exp2(m - m_new) + _dot(p, v_ref[hh, ks, :])
            m = m_new
        ones_lane = MLA_V if hh % 2 == 0 else 0
        outs.append(acc / acc[:, ones_lane:ones_lane + 1])
    lane = lax.broadcasted_iota(jnp.int32, outs[0].shape, 1)
    for pr in range(len(outs) // 2):
        o_ref[:, pr * LANES:(pr + 1) * LANES] = jnp.where(
            lane < MLA_V, outs[2 * pr], outs[2 * pr + 1]).astype(o_ref.dtype)


def _attn_call(q, k, v, tq):
    b_, nh, l_, _ = q.shape
    t_ = k.shape[2]
    hps = ATT_HEADS_PER_STEP
    return pl.pallas_call(
        _attn_kernel,
        grid=(b_, nh // hps, l_ // tq),
        in_specs=[
            pl.BlockSpec((None, hps, tq, LANES), lambda b, h, i: (b, h, i, 0)),
            pl.BlockSpec((None, hps, t_, LANES), lambda b, h, i: (b, h, 0, 0)),
            pl.BlockSpec((None, hps, t_, LANES), lambda b, h, i: (b, h, 0, 0)),
        ],
        out_specs=pl.BlockSpec((None, tq, (hps // 2) * LANES), lambda b, h, i: (b, i, h)),
        out_shape=jax.ShapeDtypeStruct((b_, l_, (nh // 2) * LANES), BF16),
        compiler_params=_cparams(("parallel", "parallel", "arbitrary")),
        name="mla_attention",
    )(q, k, v)


def _dispatch_layout(eid, rank, counts):
    tm = eid.shape[-1]
    n_slot = eid.size
    starts = jnp.cumsum(counts) - counts
    padded = (counts + MOE_BLOCK - 1) // MOE_BLOCK * MOE_BLOCK
    pad_ends = jnp.cumsum(padded)
    pad_starts = pad_ends - padded
    n_blocks = (n_slot + MOE_BLOCK - 1) // MOE_BLOCK + MOE_EXPERTS
    block_start = jnp.arange(n_blocks, dtype=jnp.int32) * MOE_BLOCK
    block_expert = jnp.minimum(
        jnp.sum((block_start[:, None] >= pad_ends[None, :]).astype(jnp.int32), axis=1),
        MOE_EXPERTS - 1)
    n_used = pad_ends[-1:] // MOE_BLOCK
    slot_row = rank
    for e in range(MOE_EXPERTS):
        slot_row = slot_row + jnp.where(eid == e, pad_starts[e], 0)
    order = jnp.argsort(eid.reshape(n_slot)).astype(jnp.int32)
    offs = (block_start - pad_starts[block_expert])[:, None] + jnp.arange(MOE_BLOCK, dtype=jnp.int32)
    valid = offs < counts[block_expert][:, None]
    pos = jnp.minimum(starts[block_expert][:, None] + offs, n_slot - 1)
    slot = jnp.take(order, pos, mode='clip')
    tok = (slot // (MOE_TOP_K * tm)) * tm + slot % tm
    n_tok = n_slot // MOE_TOP_K
    filler = (block_start[:, None] + jnp.arange(MOE_BLOCK, dtype=jnp.int32)) % n_tok
    row_tok = jnp.where(valid, tok, filler)
    return row_tok.reshape(-1), slot_row, block_expert, n_used


def _moe(h2, eid, rank, counts, gw, x, gt, wg, wu, wd, layer, n_ctx, tm, next_mod=None):
    b_, tn, d = h2.shape
    n_tok = b_ * tn
    cnt = counts[MOE_GROUPS:MOE_GROUPS + MOE_EXPERTS, 0].astype(jnp.int32)
    row_tok, slot_row, block_expert, n_used = _dispatch_layout(eid, rank, cnt)
    r_pad = row_tok.shape[0]
    n_blocks = r_pad // MOE_BLOCK
    bounds = [n_blocks * p // EXPERT_PARTS for p in range(EXPERT_PARTS + 1)]
    h2_rows = h2.reshape(n_tok, d)
    ybuf = None
    for lo, hi in zip(bounds[:-1], bounds[1:]):
        part = jnp.take(h2_rows, row_tok[lo * MOE_BLOCK:hi * MOE_BLOCK], axis=0, mode='clip')
        ybuf = _expert_call(part, block_expert, n_used, wg, wu, wd, layer, lo, r_pad, ybuf)
    bp = b_ // COMBINE_PARTS
    outs = None
    for p in range(COMBINE_PARTS):
        rows = slot_row[p * bp:(p + 1) * bp]
        y0 = jnp.take(ybuf, rows[:, :, 0, :].reshape(bp * tn), axis=0, mode='clip').reshape(bp, tn, d)
        y1 = jnp.take(ybuf, rows[:, :, 1, :].reshape(bp * tn), axis=0, mode='clip').reshape(bp, tn, d)
        outs = _combine_call(y0, y1, gw, x, gt, n_ctx, tm, next_mod, p * bp, outs)
    return outs


def _rot_half_perm():
    half = MLA_ROPE // 2
    j = jnp.arange(MLA_ROPE)
    within = j % half
    base = j - within
    src = jnp.where(within < half // 2, base + within + half // 2, base + within - half // 2)
    sign = jnp.where(within < half // 2, -1.0, 1.0).astype(F32)
    return src, sign


def _rope_tables(n_ctx, seq):
    rows = seq // GRID_W
    row = jnp.repeat(jnp.arange(rows), GRID_W)
    col = jnp.tile(jnp.arange(GRID_W), rows)
    half = MLA_ROPE // 2
    inv_freq = ROPE_THETA ** (-jnp.arange(0, half, 2, dtype=F32) / half)
    ang = jnp.stack([row, col], axis=-1).astype(F32)[..., None] * inv_freq
    ang = jnp.concatenate([ang, ang], axis=-1).reshape(seq, MLA_ROPE)
    cos = jnp.concatenate([jnp.ones((n_ctx, MLA_ROPE), F32), jnp.cos(ang)], axis=0)
    sin = jnp.concatenate([jnp.zeros((n_ctx, MLA_ROPE), F32), jnp.sin(ang)], axis=0)
    return cos, sin


def _mla_prepare(w_in, w_qb, w_kvb, q_qk_g, k_qk_g, kv_norm_g, n_ctx, seq):
    d = w_in.shape[0]
    src, sign = _rot_half_perm()
    cos, sin = _rope_tables(n_ctx, seq)
    t_ = n_ctx + seq

    rope0 = MLA_Q_LORA + MLA_KV_LORA
    w_rope = w_in[:, rope0:rope0 + MLA_ROPE]
    win = jnp.concatenate(
        [w_in, w_rope[:, src] * sign, jnp.zeros((d, 512 - rope0 - 2 * MLA_ROPE), F32)], axis=1)

    wq3 = w_qb.reshape(MLA_Q_LORA, MLA_HEADS, MLA_QK)
    pad = jnp.zeros((MLA_Q_LORA, MLA_HEADS, LANES - MLA_QK), F32)
    wq = jnp.concatenate([wq3, pad], axis=-1).reshape(MLA_Q_LORA, MLA_HEADS * LANES)
    wq_rot = wq3[:, :, MLA_NOPE:][:, :, src] * sign
    wqr = jnp.concatenate([jnp.zeros((MLA_Q_LORA, MLA_HEADS, MLA_NOPE), F32), wq_rot, pad],
                          axis=-1).reshape(MLA_Q_LORA, MLA_HEADS * LANES)

    scale = MLA_QK ** -0.5 * 1.4426950408889634
    gq_n, gq_r = q_qk_g[:MLA_NOPE], q_qk_g[MLA_NOPE:]
    zpad = jnp.zeros((t_, LANES - MLA_QK), F32)
    tqc = jnp.concatenate([jnp.broadcast_to(gq_n, (t_, MLA_NOPE)), gq_r * cos, zpad], axis=1) * scale
    tqs = jnp.concatenate([jnp.zeros((t_, MLA_NOPE), F32), gq_r[src] * sin, zpad], axis=1) * scale

    wkv3 = w_kvb.reshape(MLA_KV_LORA, MLA_HEADS, MLA_NOPE + MLA_V)
    wk_lat = jnp.concatenate(
        [wkv3[:, :, :MLA_NOPE], jnp.zeros((MLA_KV_LORA, MLA_HEADS, LANES - MLA_NOPE), F32)], axis=-1)
    place = jnp.concatenate([jnp.zeros((MLA_ROPE, MLA_NOPE), F32), jnp.eye(MLA_ROPE, dtype=F32),
                             jnp.zeros((MLA_ROPE, LANES - MLA_QK), F32)], axis=1)
    place = jnp.broadcast_to(place[:, None, :], (MLA_ROPE, MLA_HEADS, LANES))
    wk = jnp.concatenate([wk_lat, place, place,
                          jnp.zeros((256 - MLA_KV_LORA - 2 * MLA_ROPE, MLA_HEADS, LANES), F32)],
                         axis=0).reshape(256, MLA_HEADS * LANES)
    gk_n, gk_r = k_qk_g[:MLA_NOPE], k_qk_g[MLA_NOPE:]
    ktab = jnp.concatenate([jnp.broadcast_to(kv_norm_g, (t_, MLA_KV_LORA)), gk_r * cos,
                            gk_r[src] * sin, jnp.zeros((t_, 256 - MLA_KV_LORA - 2 * MLA_ROPE), F32)],
                           axis=1)
    kgain = jnp.concatenate([gk_n, jnp.ones((MLA_ROPE,), F32),
                             jnp.zeros((LANES - MLA_QK,), F32)]).reshape(1, LANES)

    wv_h = wkv3[:, :, MLA_NOPE:]
    zv = jnp.zeros_like(wv_h)
    odd = (jnp.arange(MLA_HEADS) % 2 == 1)[None, :, None]
    wv = jnp.concatenate([jnp.where(odd, zv, wv_h), jnp.where(odd, wv_h, zv)],
                         axis=-1).reshape(MLA_KV_LORA, MLA_HEADS * LANES)
    return (win.astype(BF16), wq.astype(BF16), wqr.astype(BF16), tqc, tqs, ktab,
            wk.astype(BF16), wv.astype(BF16), kgain)


def _router_weights(w_group, w_expert):
    d = w_group.shape[0]
    wt = jnp.concatenate([w_group, w_expert], axis=1).T
    wt = jnp.concatenate([wt, jnp.zeros((ROUTER_ROWS - wt.shape[0], d), F32)], axis=0)
    head = wt.astype(BF16)
    rest = (wt - head.astype(F32)).astype(BF16)
    return jnp.concatenate([head, rest], axis=0)


def kernel(x, c, ctx, c_ctx, ada_w, ada_b, norm_mix_g, norm_ffn_g, hg_w_in, hg_lower_bounds, hg_out_norm_g, hg_w_out, mla_w_in, mla_q_norm_g, mla_kv_norm_g, mla_w_qb, mla_w_kvb, mla_q_qknorm_g, mla_k_qknorm_g, mla_w_out, moe_w_group, moe_w_expert, moe_w_gate, moe_w_up, moe_w_down):
    b_, seq, d = x.shape
    n_ctx = ctx.shape[1]
    tm = min(ROW_TILE, n_ctx)
    tq = min(ATT_Q_TILE, seq)
    assert n_ctx % GLA_BLOCK == 0 and seq % GLA_BLOCK == 0 and n_ctx % tm == 0 and seq % tm == 0
    assert b_ + 1 <= ADA_ROWS and seq % GRID_W == 0
    assert b_ % min(GLA_BATCH, b_) == 0 and b_ % COMBINE_PARTS == 0

    cc = jnp.concatenate([c, c_ctx[None, :], jnp.zeros((ADA_ROWS - b_ - 1, d), F32)], axis=0)
    mods = _ada_call(cc, ada_w, ada_b).reshape(DEPTH, ADA_CHUNKS, ADA_ROWS, 1, d)

    def mod(i, chunk):
        return mods[i, chunk]

    row = lambda v: v.reshape(1, -1)

    lower = jnp.cumsum(jax.nn.softmax(hg_lower_bounds.astype(F32), axis=1), axis=1)[:, 0]
    qs, kf, gf, kb, gb, v, gs = _hg_in_call(
        ctx, x, mod(0, 0), mod(0, 1), row(norm_mix_g[0]), hg_w_in[0].astype(BF16), lower, tm)
    o_b = _gla_call(qs, kb, gb, v, n_ctx, reverse=True)
    og = _gla_call(qs, kf, gf, v, n_ctx, reverse=False,
                   final_args=(o_b, gs, row(hg_out_norm_g[0])))
    x1, h2, eid, gw, rank, cnt = _outproj_call(
        og, hg_w_out[0].astype(BF16), x, 0, ctx, mod(0, 2), mod(0, 3), mod(0, 4),
        row(norm_ffn_g[0]), _router_weights(moe_w_group[0], moe_w_expert[0]), n_ctx, tm)
    x2, hm = _moe(h2, eid, rank, cnt, gw, x1, mod(0, 5), moe_w_gate, moe_w_up, moe_w_down, 0,
                  n_ctx, tm, next_mod=(mod(1, 0), mod(1, 1), row(norm_mix_g[1])))

    win, wq, wqr, tqc, tqs, ktab, wk, wv, kgain = _mla_prepare(
        mla_w_in[0], mla_w_qb[0], mla_w_kvb[0], mla_q_qknorm_g[0], mla_k_qknorm_g[0],
        mla_kv_norm_g[0], n_ctx, seq)
    q, k, vv = _mla_in_call(hm, win, row(mla_q_norm_g[0]), wq, wqr, tqc, tqs, ktab, wk, wv, kgain,
                            n_ctx, tm)
    o_att = _attn_call(q, k, vv, tq)
    x3, h2, eid, gw, rank, cnt = _outproj_call(
        o_att, mla_w_out[0].astype(BF16), x2, n_ctx, None, mod(1, 2), mod(1, 3), mod(1, 4),
        row(norm_ffn_g[1]), _router_weights(moe_w_group[1], moe_w_expert[1]), 0, tm)
    (x4,) = _moe(h2, eid, rank, cnt, gw, x3, mod(1, 5), moe_w_gate, moe_w_up, moe_w_down, 1, 0, tm)
    return x4
```

```python
import functools

import jax
import jax.numpy as jnp
from jax import lax
from jax.experimental import pallas as pl
from jax.experimental.pallas import tpu as pltpu

F32 = jnp.float32
BF16 = jnp.bfloat16

DEPTH = 2
ADA_CHUNKS = 6
NORM_EPS = 1e-6
GRID_W = 64
HG_HEADS = 8
HG_CHUNK = 32
MLA_HEADS = 16
MLA_Q_LORA = 256
MLA_KV_LORA = 128
MLA_NOPE = 64
MLA_ROPE = 32
MLA_V = 64
MLA_QK = MLA_NOPE + MLA_ROPE
ROPE_THETA = 10000.0
MOE_GROUPS = 4
MOE_EPG = 8
MOE_EXPERTS = MOE_GROUPS * MOE_EPG
MOE_TOP_K = 2
MOE_FF = 512

LANES = 128
ROUTER_ROWS = 48
GATE_ROWS = 8
VMEM_LIMIT = 56 * 1024 * 1024

ROW_TILE = 256
GLA_BLOCK = 128
GLA_BATCH = 8
MOE_BLOCK = 512
ATT_Q_TILE = 512
ATT_HEADS_PER_STEP = 8
ATT_KEY_CHUNKS = (256, 128)
EXPERT_PARTS = 4
COMBINE_PARTS = 1
ADA_ROWS = 40


def _cparams(sem):
    return pltpu.CompilerParams(dimension_semantics=sem, vmem_limit_bytes=VMEM_LIMIT)


def _resident(shape):
    nd = len(shape)
    return pl.BlockSpec(shape, lambda *_: (0,) * nd, pipeline_mode=pl.Buffered(1))


def _dot(a, b):
    return jnp.dot(a, b, preferred_element_type=F32)


def _dot_nt(a, b):
    return lax.dot_general(a, b, (((1,), (1,)), ((), ())), preferred_element_type=F32)


def _dot_tn(a, b):
    return lax.dot_general(a, b, (((0,), (0,)), ((), ())), preferred_element_type=F32)


def _sigmoid(x):
    return 1.0 / (1.0 + jnp.exp(-x))


def _rms(x):
    return x * lax.rsqrt(jnp.mean(x * x, axis=-1, keepdims=True) + NORM_EPS)


def _ada_kernel(c_ref, w_ref, b_ref, o_ref):
    c = c_ref[...]
    a = (c * _sigmoid(c)).astype(BF16)
    o_ref[...] = _dot(a, w_ref[...].astype(BF16)) + b_ref[...]


def _ada_call(cc, ada_w, ada_b):
    depth, d, _ = ada_w.shape
    rows = cc.shape[0]
    return pl.pallas_call(
        _ada_kernel,
        grid=(depth, ADA_CHUNKS),
        in_specs=[
            pl.BlockSpec((rows, d), lambda i, j: (0, 0)),
            pl.BlockSpec((None, d, d), lambda i, j: (i, 0, j)),
            pl.BlockSpec((None, 1, d), lambda i, j: (i, 0, j)),
        ],
        out_specs=pl.BlockSpec((None, None, rows, d), lambda i, j: (i, j, 0, 0)),
        out_shape=jax.ShapeDtypeStruct((depth, ADA_CHUNKS, rows, d), F32),
        compiler_params=_cparams(("arbitrary", "arbitrary")),
        name="ada_mod",
    )(cc, ada_w, ada_b.reshape(depth, 1, ADA_CHUNKS * d))


def _mod_spec(d, n_ctx_blocks, ctx_row, b_off=0):
    def idx(b, j):
        return (jnp.where(j < n_ctx_blocks, ctx_row, b + b_off), 0, 0)
    return pl.BlockSpec((None, 1, d), idx)


def _hg_in_kernel(xc_ref, x_ref, sh_ref, sc_ref, g_ref, w_ref, lb_ref,
                  q_ref, kf_ref, gf_ref, kb_ref, gb_ref, v_ref, gs_ref, *, n_ctx_blocks):
    d = x_ref.shape[-1]
    x = jnp.where(pl.program_id(1) < n_ctx_blocks, xc_ref[...], x_ref[...])
    h = _rms(x) * g_ref[...]
    h = (h * (1.0 + sc_ref[...]) + sh_ref[...]).astype(BF16)

    def proj(c):
        return _dot(h, w_ref[:, c * d:(c + 1) * d])

    p = proj(0)
    q_ref[...] = (p * _sigmoid(p)).astype(BF16)
    for c, k_ref, lg_ref in ((1, kf_ref, gf_ref), (2, kb_ref, gb_ref)):
        s = _sigmoid(proj(c))
        lb = lb_ref[c - 1:c, :]
        k_ref[...] = ((1.0 - lb) * (1.0 - s)).astype(BF16)
        lg_ref[...] = jnp.log(lb + (1.0 - lb) * s).astype(BF16)
    v_ref[...] = proj(3).astype(BF16)
    p = proj(4)
    gs_ref[...] = (p * _sigmoid(p)).astype(BF16)


def _hg_in_call(ctx, x, sh, sc, g, w_in, lb, tm):
    b_, n_ctx, d = ctx.shape
    t_ = n_ctx + x.shape[1]
    ncb = n_ctx // tm
    tok = pl.BlockSpec((None, tm, d), lambda b, j: (b, j, 0))
    out = jax.ShapeDtypeStruct((b_, t_, d), BF16)
    return pl.pallas_call(
        functools.partial(_hg_in_kernel, n_ctx_blocks=ncb),
        grid=(b_, t_ // tm),
        in_specs=[pl.BlockSpec((None, tm, d), lambda b, j: (b, jnp.minimum(j, ncb - 1), 0)),
                  pl.BlockSpec((None, tm, d), lambda b, j: (b, jnp.maximum(j - ncb, 0), 0)),
                  _mod_spec(d, ncb, b_), _mod_spec(d, ncb, b_),
                  _resident((1, d)), _resident(w_in.shape), _resident((2, d))],
        out_specs=[tok] * 7,
        out_shape=[out] * 7,
        compiler_params=_cparams(("parallel", "arbitrary")),
        name="hg_in",
    )(ctx, x, sh, sc, g, w_in, lb)


def _gla_kernel(*refs, reverse, final, n_heads):
    if final:
        q_ref, k_ref, g_ref, v_ref, ob_ref, gs_ref, gn_ref, o_ref, st_ref = refs
    else:
        q_ref, k_ref, g_ref, v_ref, o_ref, st_ref = refs
    nb, tb, d = q_ref.shape
    dh = d // n_heads
    n_chunks = tb // HG_CHUNK

    @pl.when(pl.program_id(1) == 0)
    def _():
        st_ref[...] = jnp.zeros_like(st_ref)

    row = lax.broadcasted_iota(jnp.int32, (tb, tb), 0)
    col = lax.broadcasted_iota(jnp.int32, (tb, tb), 1)
    same_chunk = (row // HG_CHUNK) == (col // HG_CHUNK)
    tri = same_chunk & ((col >= row) if reverse else (col <= row))
    cum = jnp.where(tri, 1.0, 0.0).astype(BF16)
    chunk_order = range(n_chunks - 1, -1, -1) if reverse else range(n_chunks)

    qd, ki_b, e_end, k_end = [], [], [], []
    for n in range(nb):
        b = _dot(cum, g_ref[n])
        qd.append((q_ref[n].astype(F32) * jnp.exp(b)).astype(BF16))
        ki = k_ref[n].astype(F32) * jnp.exp(-b)
        ki_b.append(ki.astype(BF16))
        ee, ke = {}, {}
        for c in range(n_chunks):
            r0 = c * HG_CHUNK
            last = r0 if reverse else r0 + HG_CHUNK - 1
            ee[c] = jnp.exp(b[last:last + 1, :])
            ke[c] = (ki[r0:r0 + HG_CHUNK, :] * ee[c]).astype(BF16)
        e_end.append(ee)
        k_end.append(ke)

    chains = [(n, h) for n in range(nb) for h in range(n_heads)]
    hsl = [slice(h * dh, (h + 1) * dh) for h in range(n_heads)]
    scores = {(n, h): jnp.where(tri, _dot_nt(qd[n][:, hsl[h]], ki_b[n][:, hsl[h]]), 0.0).astype(BF16)
              for n, h in chains}
    o_intra = {(n, h): _dot(scores[n, h], v_ref[n, :, hsl[h]]) for n, h in chains}
    kv = {(n, h, c): _dot_tn(v_ref[n, c * HG_CHUNK:(c + 1) * HG_CHUNK, hsl[h]], k_end[n][c][:, hsl[h]])
          for n, h in chains for c in range(n_chunks)}
    st = {(n, h): st_ref[n, h] for n, h in chains}
    for c in chunk_order:
        rs = slice(c * HG_CHUNK, (c + 1) * HG_CHUNK)
        for n, h in chains:
            hs = hsl[h]
            o_c = o_intra[n, h][rs, :] + _dot_nt(qd[n][rs, hs], st[n, h].astype(BF16))
            st[n, h] = st[n, h] * e_end[n][c][:, hs] + kv[n, h, c]
            if final:
                o_c = o_c + ob_ref[n, rs, hs].astype(F32)
                o_c = _rms(o_c) * gn_ref[...]
                o_c = o_c * gs_ref[n, rs, hs].astype(F32)
            o_ref[n, rs, hs] = o_c.astype(o_ref.dtype)
    for n, h in chains:
        st_ref[n, h] = st[n, h]


def _gla_call(q, k, g, v, n_ctx, reverse, final_args=None):
    b_, t_, d = q.shape
    tb = GLA_BLOCK
    gb = min(GLA_BATCH, b_)
    nb = t_ // tb
    ncb = n_ctx // tb
    dh = d // HG_HEADS

    def blk(b, j):
        if reverse:
            jj = jnp.where(j < ncb, ncb - 1 - j, nb - 1 - (j - ncb))
        else:
            jj = j
        return (b, jj, 0)

    tok = pl.BlockSpec((gb, tb, d), blk)
    in_specs = [tok] * 4
    args = [q, k, g, v]
    final = final_args is not None
    if final:
        o_b, gs, gn = final_args
        in_specs += [tok, tok, _resident((1, dh))]
        args += [o_b, gs, gn]
    return pl.pallas_call(
        functools.partial(_gla_kernel, reverse=reverse, final=final, n_heads=HG_HEADS),
        grid=(b_ // gb, nb),
        in_specs=in_specs,
        out_specs=tok,
        out_shape=jax.ShapeDtypeStruct((b_, t_, d), BF16),
        scratch_shapes=[pltpu.VMEM((gb, HG_HEADS, dh, dh), F32)],
        compiler_params=_cparams(("parallel", "arbitrary")),
        name="gla_bwd" if reverse else "gla_fwd",
    )(*args)


def _outproj_kernel(*refs, n_ctx_blocks, split_residual):
    if split_residual:
        (a_ref, w_ref, xc_ref, x_ref, gt_ref, sh_ref, sc_ref, g_ref, wr_ref,
         xn_ref, h2_ref, eid_ref, gw_ref, rank_ref, cnt_ref, run_ref) = refs
    else:
        (a_ref, w_ref, x_ref, gt_ref, sh_ref, sc_ref, g_ref, wr_ref,
         xn_ref, h2_ref, eid_ref, gw_ref, rank_ref, cnt_ref, run_ref) = refs
    tm = a_ref.shape[0]
    nr = ROUTER_ROWS
    first = (pl.program_id(0) == 0) & (pl.program_id(1) == 0)

    @pl.when(first)
    def _():
        run_ref[...] = jnp.zeros_like(run_ref)

    x = x_ref[...]
    if split_residual:
        x = jnp.where(pl.program_id(1) < n_ctx_blocks, xc_ref[...], x)
    y = _dot(a_ref[...], w_ref[...])
    xn = x + gt_ref[...] * y
    xn_ref[...] = xn
    h2 = _rms(xn) * g_ref[...]
    h2 = h2 * (1.0 + sc_ref[...]) + sh_ref[...]
    h2b = h2.astype(BF16)
    h2_ref[...] = h2b
    h2r = (h2 - h2b.astype(F32)).astype(BF16)
    lg2 = _dot_nt(wr_ref[...], h2b) + _dot_nt(wr_ref[...], h2r)
    lg = lg2[:nr] + lg2[nr:]

    ridx = lax.broadcasted_iota(jnp.int32, lg.shape, 0)
    neg = -jnp.inf
    gl = jnp.where(ridx < MOE_GROUPS, lg, neg)
    gmax = jnp.max(gl, axis=0, keepdims=True)
    g_idx = jnp.min(jnp.where(gl == gmax, ridx, nr), axis=0, keepdims=True)
    p_group = 1.0 / jnp.sum(jnp.exp(gl - gmax), axis=0, keepdims=True)
    lo = MOE_GROUPS + MOE_EPG * g_idx
    el = jnp.where((ridx >= lo) & (ridx < lo + MOE_EPG), lg, neg)
    m1 = jnp.max(el, axis=0, keepdims=True)
    i1 = jnp.min(jnp.where(el == m1, ridx, nr), axis=0, keepdims=True)
    el2 = jnp.where(ridx == i1, neg, el)
    m2 = jnp.max(el2, axis=0, keepdims=True)
    i2 = jnp.min(jnp.where(el2 == m2, ridx, nr), axis=0, keepdims=True)
    r21 = jnp.exp(m2 - m1)
    w1 = p_group / (1.0 + r21)
    w2 = w1 * r21

    hot1 = ridx == i1
    hot2 = ridx == i2
    f1 = jnp.where(hot1, 1.0, 0.0)
    f2 = jnp.where(hot2, 1.0, 0.0)
    r = lax.broadcasted_iota(jnp.int32, (tm, tm), 0)
    c = lax.broadcasted_iota(jnp.int32, (tm, tm), 1)
    earlier = jnp.where(r < c, 1.0, 0.0).astype(BF16)
    pre = _dot(jnp.concatenate([f1, f2], axis=0).astype(BF16), earlier)
    run = run_ref[:, 0:1]
    n1 = jnp.sum(f1, axis=1, keepdims=True)
    n2 = jnp.sum(f2, axis=1, keepdims=True)
    rank1 = jnp.sum(jnp.where(hot1, pre[:nr] + run, 0.0), axis=0, keepdims=True)
    rank2 = jnp.sum(jnp.where(hot2, pre[nr:] + (run + n1), 0.0), axis=0, keepdims=True)
    run = jnp.broadcast_to(run + n1 + n2, run_ref.shape)
    run_ref[...] = run
    cnt_ref[...] = run

    eid_ref[...] = jnp.concatenate([i1, i2], axis=0) - MOE_GROUPS
    rank_ref[...] = jnp.concatenate([rank1, rank2], axis=0).astype(jnp.int32)
    gw_ref[...] = jnp.concatenate([w1, w2, jnp.zeros((GATE_ROWS - MOE_TOP_K, tm), F32)], axis=0)


def _outproj_call(a, w, res_lat, lat_off, res_ctx, gt, sh, sc, g, wr, n_ctx, tm):
    b_, tn, d = a.shape
    ncb = n_ctx // tm
    nt = tn // tm
    lob = lat_off // tm
    split = res_ctx is not None
    tok = pl.BlockSpec((None, tm, d), lambda b, j: (b, j, 0))
    pair = pl.BlockSpec((None, None, MOE_TOP_K, tm), lambda b, j: (b, j, 0, 0))
    res_specs, res_args = [], []
    if split:
        res_specs.append(pl.BlockSpec((None, tm, d), lambda b, j: (b, jnp.minimum(j, ncb - 1), 0)))
        res_args.append(res_ctx)
        res_specs.append(pl.BlockSpec((None, tm, d), lambda b, j: (b, jnp.maximum(j - ncb, 0) + lob, 0)))
    else:
        res_specs.append(pl.BlockSpec((None, tm, d), lambda b, j: (b, j + lob, 0)))
    res_args.append(res_lat)
    return pl.pallas_call(
        functools.partial(_outproj_kernel, n_ctx_blocks=ncb, split_residual=split),
        grid=(b_, nt),
        in_specs=[tok, _resident(w.shape)] + res_specs + [
            _mod_spec(d, ncb, b_), _mod_spec(d, ncb, b_), _mod_spec(d, ncb, b_),
            _resident((1, d)), _resident(wr.shape)],
        out_specs=[tok, tok, pair,
                   pl.BlockSpec((None, None, GATE_ROWS, tm), lambda b, j: (b, j, 0, 0)), pair,
                   pl.BlockSpec((ROUTER_ROWS, LANES), lambda b, j: (0, 0))],
        out_shape=[jax.ShapeDtypeStruct((b_, tn, d), F32),
                   jax.ShapeDtypeStruct((b_, tn, d), BF16),
                   jax.ShapeDtypeStruct((b_, nt, MOE_TOP_K, tm), jnp.int32),
                   jax.ShapeDtypeStruct((b_, nt, GATE_ROWS, tm), F32),
                   jax.ShapeDtypeStruct((b_, nt, MOE_TOP_K, tm), jnp.int32),
                   jax.ShapeDtypeStruct((ROUTER_ROWS, LANES), F32)],
        scratch_shapes=[pltpu.VMEM((ROUTER_ROWS, LANES), F32)],
        compiler_params=_cparams(("arbitrary", "arbitrary")),
        name="outproj_ffnmod",
    )(a, w, *res_args, gt, sh, sc, g, wr)


def _expert_kernel(*refs, blk_off, aliased):
    if aliased:
        be_ref, nu_ref, x_ref, wg_ref, wu_ref, wd_ref, _, y_ref, wgb_ref, wub_ref, wdb_ref = refs
    else:
        be_ref, nu_ref, x_ref, wg_ref, wu_ref, wd_ref, y_ref, wgb_ref, wub_ref, wdb_ref = refs
    step = pl.program_id(0)
    i = step + blk_off
    new_expert = (step == 0) | (be_ref[i] != be_ref[jnp.maximum(i - 1, 0)])

    @pl.when(new_expert)
    def _():
        wgb_ref[...] = wg_ref[...].astype(BF16)
        wub_ref[...] = wu_ref[...].astype(BF16)
        wdb_ref[...] = wd_ref[...].astype(BF16)

    @pl.when(i < nu_ref[0])
    def _():
        rows = x_ref.shape[0] // 2
        halves = [slice(j * rows, (j + 1) * rows) for j in range(2)]
        gates = [_dot(x_ref[r, :], wgb_ref[...]) for r in halves]
        ups = [_dot(x_ref[r, :], wub_ref[...]) for r in halves]
        for j, r in enumerate(halves):
            act = (gates[j] * _sigmoid(gates[j]) * ups[j]).astype(BF16)
            y_ref[r, :] = _dot(act, wdb_ref[...]).astype(y_ref.dtype)

    @pl.when(i >= nu_ref[0])
    def _():
        y_ref[...] = jnp.zeros_like(y_ref)


def _expert_call(buf, block_expert, n_used, wg, wu, wd, layer, blk_off, r_pad, ybuf=None):
    rows, d = buf.shape
    n_part = rows // MOE_BLOCK
    ff = wg.shape[-1]
    aliased = ybuf is not None

    def x_idx(s, be, nu):
        return (jnp.clip(s, 0, jnp.maximum(nu[0] - blk_off - 1, 0)), 0)

    in_specs = [
        pl.BlockSpec((MOE_BLOCK, d), x_idx),
        pl.BlockSpec((None, None, d, ff), lambda s, be, nu: (layer, be[s + blk_off], 0, 0)),
        pl.BlockSpec((None, None, d, ff), lambda s, be, nu: (layer, be[s + blk_off], 0, 0)),
        pl.BlockSpec((None, None, ff, d), lambda s, be, nu: (layer, be[s + blk_off], 0, 0)),
    ]
    args = [block_expert, n_used, buf, wg, wu, wd]
    if aliased:
        in_specs.append(pl.BlockSpec(memory_space=pl.ANY))
        args.append(ybuf)
    grid_spec = pltpu.PrefetchScalarGridSpec(
        num_scalar_prefetch=2,
        grid=(n_part,),
        in_specs=in_specs,
        out_specs=pl.BlockSpec((MOE_BLOCK, d), lambda s, be, nu: (s + blk_off, 0)),
        scratch_shapes=[pltpu.VMEM((d, ff), BF16), pltpu.VMEM((d, ff), BF16),
                        pltpu.VMEM((ff, d), BF16)],
    )
    return pl.pallas_call(
        functools.partial(_expert_kernel, blk_off=blk_off, aliased=aliased),
        grid_spec=grid_spec,
        out_shape=jax.ShapeDtypeStruct((r_pad, d), BF16),
        input_output_aliases={len(args) - 1: 0} if aliased else {},
        compiler_params=_cparams(("arbitrary",)),
        name="moe_experts",
    )(*args)


def _combine_kernel(*refs, with_next, n_prev):
    if with_next:
        y0_ref, y1_ref, gw_ref, x_ref, gt_ref, sh_ref, sc_ref, g_ref = refs[:8]
        xo_ref, hn_ref = refs[8 + n_prev:]
    else:
        y0_ref, y1_ref, gw_ref, x_ref, gt_ref = refs[:5]
        (xo_ref,) = refs[5 + n_prev:]
    gw = gw_ref[...].T
    f = gw[:, 0:1] * y0_ref[...].astype(F32) + gw[:, 1:2] * y1_ref[...].astype(F32)
    xo = x_ref[...] + gt_ref[...] * f
    xo_ref[...] = xo
    if with_next:
        h = _rms(xo) * g_ref[...]
        hn_ref[...] = (h * (1.0 + sc_ref[...]) + sh_ref[...]).astype(BF16)


def _combine_call(y0, y1, gw, x, gt, n_ctx, tm, next_mod, b_off, prev):
    b_, tn, d = x.shape
    bp = y0.shape[0]
    ncb = n_ctx // tm
    part_tok = pl.BlockSpec((None, tm, d), lambda b, j: (b, j, 0))
    tok = pl.BlockSpec((None, tm, d), lambda b, j: (b + b_off, j, 0))
    in_specs = [part_tok, part_tok,
                pl.BlockSpec((None, None, GATE_ROWS, tm), lambda b, j: (b + b_off, j, 0, 0)),
                tok, _mod_spec(d, ncb, b_, b_off)]
    args = [y0, y1, gw, x, gt]
    out_specs = [tok]
    out_shape = [jax.ShapeDtypeStruct((b_, tn, d), F32)]
    with_next = next_mod is not None
    if with_next:
        sh, sc, g = next_mod
        in_specs += [_mod_spec(d, ncb, b_, b_off), _mod_spec(d, ncb, b_, b_off), _resident((1, d))]
        args += [sh, sc, g]
        out_specs.append(tok)
        out_shape.append(jax.ShapeDtypeStruct((b_, tn, d), BF16))
    aliases = {}
    n_prev = 0
    if prev is not None:
        n_prev = len(prev)
        for k, arr in enumerate(prev):
            aliases[len(args)] = k
            in_specs.append(pl.BlockSpec(memory_space=pl.ANY))
            args.append(arr)
    return pl.pallas_call(
        functools.partial(_combine_kernel, with_next=with_next, n_prev=n_prev),
        grid=(bp, tn // tm),
        in_specs=in_specs,
        out_specs=out_specs,
        out_shape=out_shape,
        input_output_aliases=aliases,
        compiler_params=_cparams(("parallel", "arbitrary")),
        name="moe_combine",
    )(*args)


def _mla_in_kernel(h_ref, win_ref, qg_ref, wq_ref, wqr_ref, tqc_ref, tqs_ref,
                   ktab_ref, wk_ref, wv_ref, kgain_ref, q_ref, k_ref, v_ref):
    tm = h_ref.shape[0]
    proj = _dot(h_ref[...], win_ref[...])

    cq = proj[:, :MLA_Q_LORA]
    qn = (_rms(cq) * qg_ref[...]).astype(BF16)
    p2 = proj[:, MLA_Q_LORA:]
    lane = lax.broadcasted_iota(jnp.int32, (tm, p2.shape[1]), 1)
    is_kv = lane < MLA_KV_LORA
    sq = p2 * p2
    ms_kv = jnp.sum(jnp.where(is_kv, sq, 0.0), axis=-1, keepdims=True) * (1.0 / MLA_KV_LORA)
    is_rope = (lane >= MLA_KV_LORA) & (lane < MLA_KV_LORA + MLA_ROPE)
    ss_rope = jnp.sum(jnp.where(is_rope, sq, 0.0), axis=-1, keepdims=True)
    mult = ktab_ref[...] * jnp.where(is_kv, lax.rsqrt(ms_kv + NORM_EPS), 1.0)
    lhs = (p2 * mult).astype(BF16)

    qa = _dot(qn, wq_ref[...])
    qr = _dot(qn, wqr_ref[...])
    ka = _dot(lhs, wk_ref[...])
    va = _dot(lhs[:, :MLA_KV_LORA], wv_ref[...])
    tqc = tqc_ref[...]
    tqs = tqs_ref[...]
    kgain = kgain_ref[...]
    lane1 = lax.broadcasted_iota(jnp.int32, (tm, LANES), 1)
    for h in range(MLA_HEADS):
        hs = slice(h * LANES, (h + 1) * LANES)
        a = qa[:, hs]
        rq = lax.rsqrt(jnp.sum(a * a, axis=-1, keepdims=True) * (1.0 / MLA_QK) + NORM_EPS)
        q_ref[h] = ((a * tqc + qr[:, hs] * tqs) * rq).astype(BF16)
        a = ka[:, hs]
        ssn = jnp.sum(jnp.where(lane1 < MLA_NOPE, a * a, 0.0), axis=-1, keepdims=True)
        rk = lax.rsqrt((ssn + ss_rope) * (1.0 / MLA_QK) + NORM_EPS)
        k_ref[h] = (a * kgain * rk).astype(BF16)
        ones_lane = MLA_V if h % 2 == 0 else 0
        v_ref[h] = jnp.where(lane1 == ones_lane, 1.0, va[:, hs]).astype(BF16)


def _mla_in_call(hm, win, qg, wq, wqr, tqc, tqs, ktab, wk, wv, kgain, n_ctx, tm):
    b_, t_, d = hm.shape
    ncb = n_ctx // tm
    tok = pl.BlockSpec((None, tm, d), lambda b, j: (b, j, 0))
    head = pl.BlockSpec((None, MLA_HEADS, tm, LANES), lambda b, j: (b, 0, j, 0))
    hshape = jax.ShapeDtypeStruct((b_, MLA_HEADS, t_, LANES), BF16)
    qhead = pl.BlockSpec((None, MLA_HEADS, tm, LANES), lambda b, j: (b, 0, jnp.maximum(j - ncb, 0), 0))
    qshape = jax.ShapeDtypeStruct((b_, MLA_HEADS, t_ - n_ctx, LANES), BF16)
    return pl.pallas_call(
        _mla_in_kernel,
        grid=(b_, t_ // tm),
        in_specs=[tok, _resident(win.shape), _resident(qg.shape), _resident(wq.shape),
                  _resident(wqr.shape),
                  pl.BlockSpec((tm, LANES), lambda b, j: (j, 0)),
                  pl.BlockSpec((tm, LANES), lambda b, j: (j, 0)),
                  pl.BlockSpec((tm, ktab.shape[1]), lambda b, j: (j, 0)),
                  _resident(wk.shape), _resident(wv.shape), _resident(kgain.shape)],
        out_specs=[qhead, head, head],
        out_shape=[qshape, hshape, hshape],
        compiler_params=_cparams(("parallel", "arbitrary")),
        name="mla_in",
    )(hm, win, qg, wq, wqr, tqc, tqs, ktab, wk, wv, kgain)


def _attn_kernel(q_ref, k_ref, v_ref, o_ref):
    n_heads, tq, _ = q_ref.shape
    t_ = k_ref.shape[1]
    ck = next(c for c in ATT_KEY_CHUNKS if t_ % c == 0)
    outs = []
    for hh in range(n_heads):
        q = q_ref[hh]
        m = jnp.full((tq, 1), -jnp.inf, F32)
        acc = jnp.zeros((tq, LANES), F32)
        for c in range(t_ // ck):
            ks = slice(c * ck, (c + 1) * ck)
            s = _dot_nt(q, k_ref[hh, ks, :])
            m_new = jnp.maximum(m, jnp.max(s, axis=-1, keepdims=True))
            p = jnp.exp2((s - m_new).astype(BF16))
            acc = acc * jnp.exp2(m - m_new) + _dot(p, v_ref[hh, ks, :])
            m = m_new
        ones_lane = MLA_V if hh % 2 == 0 else 0
        outs.append(acc / acc[:, ones_lane:ones_lane + 1])
    lane = lax.broadcasted_iota(jnp.int32, outs[0].shape, 1)
    for pr in range(len(outs) // 2):
        o_ref[:, pr * LANES:(pr + 1) * LANES] = jnp.where(
            lane < MLA_V, outs[2 * pr], outs[2 * pr + 1]).astype(o_ref.dtype)


def _attn_call(q, k, v, tq):
    b_, nh, l_, _ = q.shape
    t_ = k.shape[2]
    hps = ATT_HEADS_PER_STEP
    return pl.pallas_call(
        _attn_kernel,
        grid=(b_, nh // hps, l_ // tq),
        in_specs=[
            pl.BlockSpec((None, hps, tq, LANES), lambda b, h, i: (b, h, i, 0)),
            pl.BlockSpec((None, hps, t_, LANES), lambda b, h, i: (b, h, 0, 0)),
            pl.BlockSpec((None, hps, t_, LANES), lambda b, h, i: (b, h, 0, 0)),
        ],
        out_specs=pl.BlockSpec((None, tq, (hps // 2) * LANES), lambda b, h, i: (b, i, h)),
        out_shape=jax.ShapeDtypeStruct((b_, l_, (nh // 2) * LANES), BF16),
        compiler_params=_cparams(("parallel", "parallel", "arbitrary")),
        name="mla_attention",
    )(q, k, v)


def _dispatch_layout(eid, rank, counts):
    tm = eid.shape[-1]
    n_slot = eid.size
    padded = (counts + MOE_BLOCK - 1) // MOE_BLOCK * MOE_BLOCK
    pad_ends = jnp.cumsum(padded)
    pad_starts = pad_ends - padded
    n_blocks = (n_slot + MOE_BLOCK - 1) // MOE_BLOCK + MOE_EXPERTS
    block_start = jnp.arange(n_blocks, dtype=jnp.int32) * MOE_BLOCK
    block_expert = jnp.minimum(
        jnp.sum((block_start[:, None] >= pad_ends[None, :]).astype(jnp.int32), axis=1),
        MOE_EXPERTS - 1)
    n_used = pad_ends[-1:] // MOE_BLOCK
    slot_row = rank
    for e in range(MOE_EXPERTS):
        slot_row = slot_row + jnp.where(eid == e, pad_starts[e], 0)
    offs = (block_start - pad_starts[block_expert])[:, None] + jnp.arange(MOE_BLOCK, dtype=jnp.int32)
    valid = offs < counts[block_expert][:, None]
    slot = jnp.zeros((n_blocks * MOE_BLOCK,), jnp.int32).at[slot_row.reshape(n_slot)].set(
        jnp.arange(n_slot, dtype=jnp.int32), unique_indices=True,
        mode='promise_in_bounds').reshape(n_blocks, MOE_BLOCK)
    tok = (slot // (MOE_TOP_K * tm)) * tm + slot % tm
    n_tok = n_slot // MOE_TOP_K
    filler = (block_start[:, None] + jnp.arange(MOE_BLOCK, dtype=jnp.int32)) % n_tok
    row_tok = jnp.where(valid, tok, filler)
    return row_tok.reshape(-1), slot_row, block_expert, n_used


def _moe(h2, eid, rank, counts, gw, x, gt, wg, wu, wd, layer, n_ctx, tm, next_mod=None):
    b_, tn, d = h2.shape
    n_tok = b_ * tn
    cnt = counts[MOE_GROUPS:MOE_GROUPS + MOE_EXPERTS, 0].astype(jnp.int32)
    row_tok, slot_row, block_expert, n_used = _dispatch_layout(eid, rank, cnt)
    r_pad = row_tok.shape[0]
    n_blocks = r_pad // MOE_BLOCK
    bounds = [n_blocks * p // EXPERT_PARTS for p in range(EXPERT_PARTS + 1)]
    h2_rows = h2.reshape(n_tok, d)
    ybuf = None
    for lo, hi in zip(bounds[:-1], bounds[1:]):
        part = jnp.take(h2_rows, row_tok[lo * MOE_BLOCK:hi * MOE_BLOCK], axis=0, mode='clip')
        ybuf = _expert_call(part, block_expert, n_used, wg, wu, wd, layer, lo, r_pad, ybuf)
    bp = b_ // COMBINE_PARTS
    outs = None
    for p in range(COMBINE_PARTS):
        rows = slot_row[p * bp:(p + 1) * bp]
        y0 = jnp.take(ybuf, rows[:, :, 0, :].reshape(bp * tn), axis=0, mode='clip').reshape(bp, tn, d)
        y1 = jnp.take(ybuf, rows[:, :, 1, :].reshape(bp * tn), axis=0, mode='clip').reshape(bp, tn, d)
        outs = _combine_call(y0, y1, gw, x, gt, n_ctx, tm, next_mod, p * bp, outs)
    return outs


def _rot_half_perm():
    half = MLA_ROPE // 2
    j = jnp.arange(MLA_ROPE)
    within = j % half
    base = j - within
    src = jnp.where(within < half // 2, base + within + half // 2, base + within - half // 2)
    sign = jnp.where(within < half // 2, -1.0, 1.0).astype(F32)
    return src, sign


def _rope_tables(n_ctx, seq):
    rows = seq // GRID_W
    row = jnp.repeat(jnp.arange(rows), GRID_W)
    col = jnp.tile(jnp.arange(GRID_W), rows)
    half = MLA_ROPE // 2
    inv_freq = ROPE_THETA ** (-jnp.arange(0, half, 2, dtype=F32) / half)
    ang = jnp.stack([row, col], axis=-1).astype(F32)[..., None] * inv_freq
    ang = jnp.concatenate([ang, ang], axis=-1).reshape(seq, MLA_ROPE)
    cos = jnp.concatenate([jnp.ones((n_ctx, MLA_ROPE), F32), jnp.cos(ang)], axis=0)
    sin = jnp.concatenate([jnp.zeros((n_ctx, MLA_ROPE), F32), jnp.sin(ang)], axis=0)
    return cos, sin


def _mla_prepare(w_in, w_qb, w_kvb, q_qk_g, k_qk_g, kv_norm_g, n_ctx, seq):
    d = w_in.shape[0]
    src, sign = _rot_half_perm()
    cos, sin = _rope_tables(n_ctx, seq)
    t_ = n_ctx + seq

    rope0 = MLA_Q_LORA + MLA_KV_LORA
    w_rope = w_in[:, rope0:rope0 + MLA_ROPE]
    win = jnp.concatenate(
        [w_in, w_rope[:, src] * sign, jnp.zeros((d, 512 - rope0 - 2 * MLA_ROPE), F32)], axis=1)

    wq3 = w_qb.reshape(MLA_Q_LORA, MLA_HEADS, MLA_QK)
    pad = jnp.zeros((MLA_Q_LORA, MLA_HEADS, LANES - MLA_QK), F32)
    wq = jnp.concatenate([wq3, pad], axis=-1).reshape(MLA_Q_LORA, MLA_HEADS * LANES)
    wq_rot = wq3[:, :, MLA_NOPE:][:, :, src] * sign
    wqr = jnp.concatenate([jnp.zeros((MLA_Q_LORA, MLA_HEADS, MLA_NOPE), F32), wq_rot, pad],
                          axis=-1).reshape(MLA_Q_LORA, MLA_HEADS * LANES)

    scale = MLA_QK ** -0.5 * 1.4426950408889634
    gq_n, gq_r = q_qk_g[:MLA_NOPE], q_qk_g[MLA_NOPE:]
    zpad = jnp.zeros((t_, LANES - MLA_QK), F32)
    tqc = jnp.concatenate([jnp.broadcast_to(gq_n, (t_, MLA_NOPE)), gq_r * cos, zpad], axis=1) * scale
    tqs = jnp.concatenate([jnp.zeros((t_, MLA_NOPE), F32), gq_r[src] * sin, zpad], axis=1) * scale

    wkv3 = w_kvb.reshape(MLA_KV_LORA, MLA_HEADS, MLA_NOPE + MLA_V)
    wk_lat = jnp.concatenate(
        [wkv3[:, :, :MLA_NOPE], jnp.zeros((MLA_KV_LORA, MLA_HEADS, LANES - MLA_NOPE), F32)], axis=-1)
    place = jnp.concatenate([jnp.zeros((MLA_ROPE, MLA_NOPE), F32), jnp.eye(MLA_ROPE, dtype=F32),
                             jnp.zeros((MLA_ROPE, LANES - MLA_QK), F32)], axis=1)
    place = jnp.broadcast_to(place[:, None, :], (MLA_ROPE, MLA_HEADS, LANES))
    wk = jnp.concatenate([wk_lat, place, place,
                          jnp.zeros((256 - MLA_KV_LORA - 2 * MLA_ROPE, MLA_HEADS, LANES), F32)],
                         axis=0).reshape(256, MLA_HEADS * LANES)
    gk_n, gk_r = k_qk_g[:MLA_NOPE], k_qk_g[MLA_NOPE:]
    ktab = jnp.concatenate([jnp.broadcast_to(kv_norm_g, (t_, MLA_KV_LORA)), gk_r * cos,
                            gk_r[src] * sin, jnp.zeros((t_, 256 - MLA_KV_LORA - 2 * MLA_ROPE), F32)],
                           axis=1)
    kgain = jnp.concatenate([gk_n, jnp.ones((MLA_ROPE,), F32),
                             jnp.zeros((LANES - MLA_QK,), F32)]).reshape(1, LANES)

    wv_h = wkv3[:, :, MLA_NOPE:]
    zv = jnp.zeros_like(wv_h)
    odd = (jnp.arange(MLA_HEADS) % 2 == 1)[None, :, None]
    wv = jnp.concatenate([jnp.where(odd, zv, wv_h), jnp.where(odd, wv_h, zv)],
                         axis=-1).reshape(MLA_KV_LORA, MLA_HEADS * LANES)
    return (win.astype(BF16), wq.astype(BF16), wqr.astype(BF16), tqc, tqs, ktab,
            wk.astype(BF16), wv.astype(BF16), kgain)


def _router_weights(w_group, w_expert):
    d = w_group.shape[0]
    wt = jnp.concatenate([w_group, w_expert], axis=1).T
    wt = jnp.concatenate([wt, jnp.zeros((ROUTER_ROWS - wt.shape[0], d), F32)], axis=0)
    head = wt.astype(BF16)
    rest = (wt - head.astype(F32)).astype(BF16)
    return jnp.concatenate([head, rest], axis=0)


def kernel(x, c, ctx, c_ctx, ada_w, ada_b, norm_mix_g, norm_ffn_g, hg_w_in, hg_lower_bounds, hg_out_norm_g, hg_w_out, mla_w_in, mla_q_norm_g, mla_kv_norm_g, mla_w_qb, mla_w_kvb, mla_q_qknorm_g, mla_k_qknorm_g, mla_w_out, moe_w_group, moe_w_expert, moe_w_gate, moe_w_up, moe_w_down):
    b_, seq, d = x.shape
    n_ctx = ctx.shape[1]
    tm = min(ROW_TILE, n_ctx)
    tq = min(ATT_Q_TILE, seq)
    assert n_ctx % GLA_BLOCK == 0 and seq % GLA_BLOCK == 0 and n_ctx % tm == 0 and seq % tm == 0
    assert b_ + 1 <= ADA_ROWS and seq % GRID_W == 0
    assert b_ % min(GLA_BATCH, b_) == 0 and b_ % COMBINE_PARTS == 0

    cc = jnp.concatenate([c, c_ctx[None, :], jnp.zeros((ADA_ROWS - b_ - 1, d), F32)], axis=0)
    mods = _ada_call(cc, ada_w, ada_b).reshape(DEPTH, ADA_CHUNKS, ADA_ROWS, 1, d)

    def mod(i, chunk):
        return mods[i, chunk]

    row = lambda v: v.reshape(1, -1)

    lower = jnp.cumsum(jax.nn.softmax(hg_lower_bounds.astype(F32), axis=1), axis=1)[:, 0]
    qs, kf, gf, kb, gb, v, gs = _hg_in_call(
        ctx, x, mod(0, 0), mod(0, 1), row(norm_mix_g[0]), hg_w_in[0].astype(BF16), lower, tm)
    o_b = _gla_call(qs, kb, gb, v, n_ctx, reverse=True)
    og = _gla_call(qs, kf, gf, v, n_ctx, reverse=False,
                   final_args=(o_b, gs, row(hg_out_norm_g[0])))
    x1, h2, eid, gw, rank, cnt = _outproj_call(
        og, hg_w_out[0].astype(BF16), x, 0, ctx, mod(0, 2), mod(0, 3), mod(0, 4),
        row(norm_ffn_g[0]), _router_weights(moe_w_group[0], moe_w_expert[0]), n_ctx, tm)
    x2, hm = _moe(h2, eid, rank, cnt, gw, x1, mod(0, 5), moe_w_gate, moe_w_up, moe_w_down, 0,
                  n_ctx, tm, next_mod=(mod(1, 0), mod(1, 1), row(norm_mix_g[1])))

    win, wq, wqr, tqc, tqs, ktab, wk, wv, kgain = _mla_prepare(
        mla_w_in[0], mla_w_qb[0], mla_w_kvb[0], mla_q_qknorm_g[0], mla_k_qknorm_g[0],
        mla_kv_norm_g[0], n_ctx, seq)
    q, k, vv = _mla_in_call(hm, win, row(mla_q_norm_g[0]), wq, wqr, tqc, tqs, ktab, wk, wv, kgain,
                            n_ctx, tm)
    o_att = _attn_call(q, k, vv, tq)
    x3, h2, eid, gw, rank, cnt = _outproj_call(
        o_att, mla_w_out[0].astype(BF16), x2, n_ctx, None, mod(1, 2), mod(1, 3), mod(1, 4),
        row(norm_ffn_g[1]), _router_weights(moe_w_group[1], moe_w_expert[1]), 0, tm)
    (x4,) = _moe(h2, eid, rank, cnt, gw, x3, mod(1, 5), moe_w_gate, moe_w_up, moe_w_down, 1, 0, tm)
    return x4
```

```python
import functools

import jax
import jax.numpy as jnp
from jax import lax
from jax.experimental import pallas as pl
from jax.experimental.pallas import tpu as pltpu

F32 = jnp.float32
BF16 = jnp.bfloat16

DEPTH = 2
ADA_CHUNKS = 6
NORM_EPS = 1e-6
GRID_W = 64
HG_HEADS = 8
HG_CHUNK = 32
MLA_HEADS = 16
MLA_Q_LORA = 256
MLA_KV_LORA = 128
MLA_NOPE = 64
MLA_ROPE = 32
MLA_V = 64
MLA_QK = MLA_NOPE + MLA_ROPE
ROPE_THETA = 10000.0
MOE_GROUPS = 4
MOE_EPG = 8
MOE_EXPERTS = MOE_GROUPS * MOE_EPG
MOE_TOP_K = 2
MOE_FF = 512

LANES = 128
ROUTER_ROWS = 48
GATE_ROWS = 8
VMEM_LIMIT = 56 * 1024 * 1024

ROW_TILE = 256
GLA_BLOCK = 128
GLA_BATCH = 8
MOE_BLOCK = 512
ATT_Q_TILE = 512
ATT_HEADS_PER_STEP = 8
ATT_KEY_CHUNKS = (256, 128)
EXPERT_PARTS = 4
COMBINE_PARTS = 2
ADA_ROWS = 40


def _cparams(sem):
    return pltpu.CompilerParams(dimension_semantics=sem, vmem_limit_bytes=VMEM_LIMIT)


def _resident(shape):
    nd = len(shape)
    return pl.BlockSpec(shape, lambda *_: (0,) * nd, pipeline_mode=pl.Buffered(1))


def _dot(a, b):
    return jnp.dot(a, b, preferred_element_type=F32)


def _dot_nt(a, b):
    return lax.dot_general(a, b, (((1,), (1,)), ((), ())), preferred_element_type=F32)


def _dot_tn(a, b):
    return lax.dot_general(a, b, (((0,), (0,)), ((), ())), preferred_element_type=F32)


def _sigmoid(x):
    return 1.0 / (1.0 + jnp.exp(-x))


def _rms(x):
    return x * lax.rsqrt(jnp.mean(x * x, axis=-1, keepdims=True) + NORM_EPS)


def _ada_kernel(c_ref, w_ref, b_ref, o_ref):
    c = c_ref[...]
    a = (c * _sigmoid(c)).astype(BF16)
    o_ref[...] = _dot(a, w_ref[...].astype(BF16)) + b_ref[...]


def _ada_call(cc, ada_w, ada_b):
    depth, d, _ = ada_w.shape
    rows = cc.shape[0]
    return pl.pallas_call(
        _ada_kernel,
        grid=(depth, ADA_CHUNKS),
        in_specs=[
            pl.BlockSpec((rows, d), lambda i, j: (0, 0)),
            pl.BlockSpec((None, d, d), lambda i, j: (i, 0, j)),
            pl.BlockSpec((None, 1, d), lambda i, j: (i, 0, j)),
        ],
        out_specs=pl.BlockSpec((None, None, rows, d), lambda i, j: (i, j, 0, 0)),
        out_shape=jax.ShapeDtypeStruct((depth, ADA_CHUNKS, rows, d), F32),
        compiler_params=_cparams(("arbitrary", "arbitrary")),
        name="ada_mod",
    )(cc, ada_w, ada_b.reshape(depth, 1, ADA_CHUNKS * d))


def _mod_spec(d, n_ctx_blocks, ctx_row, b_off=0):
    def idx(b, j):
        return (jnp.where(j < n_ctx_blocks, ctx_row, b + b_off), 0, 0)
    return pl.BlockSpec((None, 1, d), idx)


def _hg_in_kernel(xc_ref, x_ref, sh_ref, sc_ref, g_ref, w_ref, lb_ref,
                  q_ref, kf_ref, gf_ref, kb_ref, gb_ref, v_ref, gs_ref, *, n_ctx_blocks):
    d = x_ref.shape[-1]
    x = jnp.where(pl.program_id(1) < n_ctx_blocks, xc_ref[...], x_ref[...])
    h = _rms(x) * g_ref[...]
    h = (h * (1.0 + sc_ref[...]) + sh_ref[...]).astype(BF16)

    def proj(c):
        return _dot(h, w_ref[:, c * d:(c + 1) * d])

    p = proj(0)
    q_ref[...] = (p * _sigmoid(p)).astype(BF16)
    for c, k_ref, lg_ref in ((1, kf_ref, gf_ref), (2, kb_ref, gb_ref)):
        s = _sigmoid(proj(c))
        lb = lb_ref[c - 1:c, :]
        k_ref[...] = ((1.0 - lb) * (1.0 - s)).astype(BF16)
        lg_ref[...] = jnp.log(lb + (1.0 - lb) * s).astype(BF16)
    v_ref[...] = proj(3).astype(BF16)
    p = proj(4)
    gs_ref[...] = (p * _sigmoid(p)).astype(BF16)


def _hg_in_call(ctx, x, sh, sc, g, w_in, lb, tm):
    b_, n_ctx, d = ctx.shape
    t_ = n_ctx + x.shape[1]
    ncb = n_ctx // tm
    tok = pl.BlockSpec((None, tm, d), lambda b, j: (b, j, 0))
    out = jax.ShapeDtypeStruct((b_, t_, d), BF16)
    return pl.pallas_call(
        functools.partial(_hg_in_kernel, n_ctx_blocks=ncb),
        grid=(b_, t_ // tm),
        in_specs=[pl.BlockSpec((None, tm, d), lambda b, j: (b, jnp.minimum(j, ncb - 1), 0)),
                  pl.BlockSpec((None, tm, d), lambda b, j: (b, jnp.maximum(j - ncb, 0), 0)),
                  _mod_spec(d, ncb, b_), _mod_spec(d, ncb, b_),
                  _resident((1, d)), _resident(w_in.shape), _resident((2, d))],
        out_specs=[tok] * 7,
        out_shape=[out] * 7,
        compiler_params=_cparams(("parallel", "arbitrary")),
        name="hg_in",
    )(ctx, x, sh, sc, g, w_in, lb)


def _gla_kernel(*refs, reverse, final, n_heads):
    if final:
        q_ref, k_ref, g_ref, v_ref, ob_ref, gs_ref, gn_ref, o_ref, st_ref = refs
    else:
        q_ref, k_ref, g_ref, v_ref, o_ref, st_ref = refs
    nb, tb, d = q_ref.shape
    dh = d // n_heads
    n_chunks = tb // HG_CHUNK

    @pl.when(pl.program_id(1) == 0)
    def _():
        st_ref[...] = jnp.zeros_like(st_ref)

    row = lax.broadcasted_iota(jnp.int32, (tb, tb), 0)
    col = lax.broadcasted_iota(jnp.int32, (tb, tb), 1)
    same_chunk = (row // HG_CHUNK) == (col // HG_CHUNK)
    tri = same_chunk & ((col >= row) if reverse else (col <= row))
    cum = jnp.where(tri, 1.0, 0.0).astype(BF16)
    chunk_order = range(n_chunks - 1, -1, -1) if reverse else range(n_chunks)

    qd, ki_b, e_end, k_end = [], [], [], []
    for n in range(nb):
        b = _dot(cum, g_ref[n])
        qd.append((q_ref[n].astype(F32) * jnp.exp(b)).astype(BF16))
        ki = k_ref[n].astype(F32) * jnp.exp(-b)
        ki_b.append(ki.astype(BF16))
        ee, ke = {}, {}
        for c in range(n_chunks):
            r0 = c * HG_CHUNK
            last = r0 if reverse else r0 + HG_CHUNK - 1
            ee[c] = jnp.exp(b[last:last + 1, :])
            ke[c] = (ki[r0:r0 + HG_CHUNK, :] * ee[c]).astype(BF16)
        e_end.append(ee)
        k_end.append(ke)

    chains = [(n, h) for n in range(nb) for h in range(n_heads)]
    hsl = [slice(h * dh, (h + 1) * dh) for h in range(n_heads)]
    scores = {(n, h): jnp.where(tri, _dot_nt(qd[n][:, hsl[h]], ki_b[n][:, hsl[h]]), 0.0).astype(BF16)
              for n, h in chains}
    o_intra = {(n, h): _dot(scores[n, h], v_ref[n, :, hsl[h]]) for n, h in chains}
    kv = {(n, h, c): _dot_tn(v_ref[n, c * HG_CHUNK:(c + 1) * HG_CHUNK, hsl[h]], k_end[n][c][:, hsl[h]])
          for n, h in chains for c in range(n_chunks)}
    st = {(n, h): st_ref[n, h] for n, h in chains}
    for c in chunk_order:
        rs = slice(c * HG_CHUNK, (c + 1) * HG_CHUNK)
        for n, h in chains:
            hs = hsl[h]
            o_c = o_intra[n, h][rs, :] + _dot_nt(qd[n][rs, hs], st[n, h].astype(BF16))
            st[n, h] = st[n, h] * e_end[n][c][:, hs] + kv[n, h, c]
            if final:
                o_c = o_c + ob_ref[n, rs, hs].astype(F32)
                o_c = _rms(o_c) * gn_ref[...]
                o_c = o_c * gs_ref[n, rs, hs].astype(F32)
            o_ref[n, rs, hs] = o_c.astype(o_ref.dtype)
    for n, h in chains:
        st_ref[n, h] = st[n, h]


def _gla_call(q, k, g, v, n_ctx, reverse, final_args=None):
    b_, t_, d = q.shape
    tb = GLA_BLOCK
    gb = min(GLA_BATCH, b_)
    nb = t_ // tb
    ncb = n_ctx // tb
    dh = d // HG_HEADS

    def blk(b, j):
        if reverse:
            jj = jnp.where(j < ncb, ncb - 1 - j, nb - 1 - (j - ncb))
        else:
            jj = j
        return (b, jj, 0)

    tok = pl.BlockSpec((gb, tb, d), blk)
    in_specs = [tok] * 4
    args = [q, k, g, v]
    final = final_args is not None
    if final:
        o_b, gs, gn = final_args
        in_specs += [tok, tok, _resident((1, dh))]
        args += [o_b, gs, gn]
    return pl.pallas_call(
        functools.partial(_gla_kernel, reverse=reverse, final=final, n_heads=HG_HEADS),
        grid=(b_ // gb, nb),
        in_specs=in_specs,
        out_specs=tok,
        out_shape=jax.ShapeDtypeStruct((b_, t_, d), BF16),
        scratch_shapes=[pltpu.VMEM((gb, HG_HEADS, dh, dh), F32)],
        compiler_params=_cparams(("parallel", "arbitrary")),
        name="gla_bwd" if reverse else "gla_fwd",
    )(*args)


def _outproj_kernel(*refs, n_ctx_blocks, split_residual):
    if split_residual:
        (a_ref, w_ref, xc_ref, x_ref, gt_ref, sh_ref, sc_ref, g_ref, wr_ref,
         xn_ref, h2_ref, eid_ref, gw_ref, rank_ref, cnt_ref, run_ref) = refs
    else:
        (a_ref, w_ref, x_ref, gt_ref, sh_ref, sc_ref, g_ref, wr_ref,
         xn_ref, h2_ref, eid_ref, gw_ref, rank_ref, cnt_ref, run_ref) = refs
    tm = a_ref.shape[0]
    nr = ROUTER_ROWS
    first = (pl.program_id(0) == 0) & (pl.program_id(1) == 0)

    @pl.when(first)
    def _():
        run_ref[...] = jnp.zeros_like(run_ref)

    x = x_ref[...]
    if split_residual:
        x = jnp.where(pl.program_id(1) < n_ctx_blocks, xc_ref[...], x)
    y = _dot(a_ref[...], w_ref[...])
    xn = x + gt_ref[...] * y
    xn_ref[...] = xn
    h2 = _rms(xn) * g_ref[...]
    h2 = h2 * (1.0 + sc_ref[...]) + sh_ref[...]
    h2b = h2.astype(BF16)
    h2_ref[...] = h2b
    lg2 = _dot_nt(wr_ref[...], h2b)
    lg = lg2[:nr] + lg2[nr:]

    ridx = lax.broadcasted_iota(jnp.int32, lg.shape, 0)
    neg = -jnp.inf
    gl = jnp.where(ridx < MOE_GROUPS, lg, neg)
    gmax = jnp.max(gl, axis=0, keepdims=True)
    g_idx = jnp.min(jnp.where(gl == gmax, ridx, nr), axis=0, keepdims=True)
    p_group = 1.0 / jnp.sum(jnp.exp(gl - gmax), axis=0, keepdims=True)
    lo = MOE_GROUPS + MOE_EPG * g_idx
    el = jnp.where((ridx >= lo) & (ridx < lo + MOE_EPG), lg, neg)
    m1 = jnp.max(el, axis=0, keepdims=True)
    i1 = jnp.min(jnp.where(el == m1, ridx, nr), axis=0, keepdims=True)
    el2 = jnp.where(ridx == i1, neg, el)
    m2 = jnp.max(el2, axis=0, keepdims=True)
    i2 = jnp.min(jnp.where(el2 == m2, ridx, nr), axis=0, keepdims=True)
    r21 = jnp.exp(m2 - m1)
    w1 = p_group / (1.0 + r21)
    w2 = w1 * r21

    hot1 = ridx == i1
    hot2 = ridx == i2
    f1 = jnp.where(hot1, 1.0, 0.0)
    f2 = jnp.where(hot2, 1.0, 0.0)
    r = lax.broadcasted_iota(jnp.int32, (tm, tm), 0)
    c = lax.broadcasted_iota(jnp.int32, (tm, tm), 1)
    earlier = jnp.where(r < c, 1.0, 0.0).astype(BF16)
    pre = _dot(jnp.concatenate([f1, f2], axis=0).astype(BF16), earlier)
    run = run_ref[:, 0:1]
    n1 = jnp.sum(f1, axis=1, keepdims=True)
    n2 = jnp.sum(f2, axis=1, keepdims=True)
    rank1 = jnp.sum(jnp.where(hot1, pre[:nr] + run, 0.0), axis=0, keepdims=True)
    rank2 = jnp.sum(jnp.where(hot2, pre[nr:] + (run + n1), 0.0), axis=0, keepdims=True)
    run = jnp.broadcast_to(run + n1 + n2, run_ref.shape)
    run_ref[...] = run
    cnt_ref[...] = run

    eid_ref[...] = jnp.concatenate([i1, i2], axis=0) - MOE_GROUPS
    rank_ref[...] = jnp.concatenate([rank1, rank2], axis=0).astype(jnp.int32)
    gw_ref[...] = jnp.concatenate([w1, w2, jnp.zeros((GATE_ROWS - MOE_TOP_K, tm), F32)], axis=0)


def _outproj_call(a, w, res_lat, lat_off, res_ctx, gt, sh, sc, g, wr, n_ctx, tm):
    b_, tn, d = a.shape
    ncb = n_ctx // tm
    nt = tn // tm
    lob = lat_off // tm
    split = res_ctx is not None
    tok = pl.BlockSpec((None, tm, d), lambda b, j: (b, j, 0))
    pair = pl.BlockSpec((None, None, MOE_TOP_K, tm), lambda b, j: (b, j, 0, 0))
    res_specs, res_args = [], []
    if split:
        res_specs.append(pl.BlockSpec((None, tm, d), lambda b, j: (b, jnp.minimum(j, ncb - 1), 0)))
        res_args.append(res_ctx)
        res_specs.append(pl.BlockSpec((None, tm, d), lambda b, j: (b, jnp.maximum(j - ncb, 0) + lob, 0)))
    else:
        res_specs.append(pl.BlockSpec((None, tm, d), lambda b, j: (b, j + lob, 0)))
    res_args.append(res_lat)
    return pl.pallas_call(
        functools.partial(_outproj_kernel, n_ctx_blocks=ncb, split_residual=split),
        grid=(b_, nt),
        in_specs=[tok, _resident(w.shape)] + res_specs + [
            _mod_spec(d, ncb, b_), _mod_spec(d, ncb, b_), _mod_spec(d, ncb, b_),
            _resident((1, d)), _resident(wr.shape)],
        out_specs=[tok, tok, pair,
                   pl.BlockSpec((None, None, GATE_ROWS, tm), lambda b, j: (b, j, 0, 0)), pair,
                   pl.BlockSpec((ROUTER_ROWS, LANES), lambda b, j: (0, 0))],
        out_shape=[jax.ShapeDtypeStruct((b_, tn, d), F32),
                   jax.ShapeDtypeStruct((b_, tn, d), BF16),
                   jax.ShapeDtypeStruct((b_, nt, MOE_TOP_K, tm), jnp.int32),
                   jax.ShapeDtypeStruct((b_, nt, GATE_ROWS, tm), F32),
                   jax.ShapeDtypeStruct((b_, nt, MOE_TOP_K, tm), jnp.int32),
                   jax.ShapeDtypeStruct((ROUTER_ROWS, LANES), F32)],
        scratch_shapes=[pltpu.VMEM((ROUTER_ROWS, LANES), F32)],
        compiler_params=_cparams(("arbitrary", "arbitrary")),
        name="outproj_ffnmod",
    )(a, w, *res_args, gt, sh, sc, g, wr)


def _expert_kernel(*refs, blk_off, aliased):
    if aliased:
        be_ref, nu_ref, x_ref, wg_ref, wu_ref, wd_ref, _, y_ref, wgb_ref, wub_ref, wdb_ref = refs
    else:
        be_ref, nu_ref, x_ref, wg_ref, wu_ref, wd_ref, y_ref, wgb_ref, wub_ref, wdb_ref = refs
    step = pl.program_id(0)
    i = step + blk_off
    new_expert = (step == 0) | (be_ref[i] != be_ref[jnp.maximum(i - 1, 0)])

    @pl.when(new_expert)
    def _():
        wgb_ref[...] = wg_ref[...].astype(BF16)
        wub_ref[...] = wu_ref[...].astype(BF16)
        wdb_ref[...] = wd_ref[...].astype(BF16)

    @pl.when(i < nu_ref[0])
    def _():
        rows = x_ref.shape[0] // 2
        halves = [slice(j * rows, (j + 1) * rows) for j in range(2)]
        gates = [_dot(x_ref[r, :], wgb_ref[...]) for r in halves]
        ups = [_dot(x_ref[r, :], wub_ref[...]) for r in halves]
        for j, r in enumerate(halves):
            act = (gates[j] * _sigmoid(gates[j]) * ups[j]).astype(BF16)
            y_ref[r, :] = _dot(act, wdb_ref[...]).astype(y_ref.dtype)

    @pl.when(i >= nu_ref[0])
    def _():
        y_ref[...] = jnp.zeros_like(y_ref)


def _expert_call(buf, block_expert, n_used, wg, wu, wd, layer, blk_off, r_pad, ybuf=None):
    rows, d = buf.shape
    n_part = rows // MOE_BLOCK
    ff = wg.shape[-1]
    aliased = ybuf is not None

    def x_idx(s, be, nu):
        return (jnp.clip(s, 0, jnp.maximum(nu[0] - blk_off - 1, 0)), 0)

    in_specs = [
        pl.BlockSpec((MOE_BLOCK, d), x_idx),
        pl.BlockSpec((None, None, d, ff), lambda s, be, nu: (layer, be[s + blk_off], 0, 0)),
        pl.BlockSpec((None, None, d, ff), lambda s, be, nu: (layer, be[s + blk_off], 0, 0)),
        pl.BlockSpec((None, None, ff, d), lambda s, be, nu: (layer, be[s + blk_off], 0, 0)),
    ]
    args = [block_expert, n_used, buf, wg, wu, wd]
    if aliased:
        in_specs.append(pl.BlockSpec(memory_space=pl.ANY))
        args.append(ybuf)
    grid_spec = pltpu.PrefetchScalarGridSpec(
        num_scalar_prefetch=2,
        grid=(n_part,),
        in_specs=in_specs,
        out_specs=pl.BlockSpec((MOE_BLOCK, d), lambda s, be, nu: (s + blk_off, 0)),
        scratch_shapes=[pltpu.VMEM((d, ff), BF16), pltpu.VMEM((d, ff), BF16),
                        pltpu.VMEM((ff, d), BF16)],
    )
    return pl.pallas_call(
        functools.partial(_expert_kernel, blk_off=blk_off, aliased=aliased),
        grid_spec=grid_spec,
        out_shape=jax.ShapeDtypeStruct((r_pad, d), BF16),
        input_output_aliases={len(args) - 1: 0} if aliased else {},
        compiler_params=_cparams(("arbitrary",)),
        name="moe_experts",
    )(*args)


def _combine_kernel(*refs, with_next, n_prev):
    if with_next:
        y0_ref, y1_ref, gw_ref, x_ref, gt_ref, sh_ref, sc_ref, g_ref = refs[:8]
        xo_ref, hn_ref = refs[8 + n_prev:]
    else:
        y0_ref, y1_ref, gw_ref, x_ref, gt_ref = refs[:5]
        (xo_ref,) = refs[5 + n_prev:]
    gw = gw_ref[...].T
    f = gw[:, 0:1] * y0_ref[...].astype(F32) + gw[:, 1:2] * y1_ref[...].astype(F32)
    xo = x_ref[...] + gt_ref[...] * f
    xo_ref[...] = xo
    if with_next:
        h = _rms(xo) * g_ref[...]
        hn_ref[...] = (h * (1.0 + sc_ref[...]) + sh_ref[...]).astype(BF16)


def _combine_call(y0, y1, gw, x, gt, n_ctx, tm, next_mod, b_off, prev):
    b_, tn, d = x.shape
    bp = y0.shape[0]
    ncb = n_ctx // tm
    part_tok = pl.BlockSpec((None, tm, d), lambda b, j: (b, j, 0))
    tok = pl.BlockSpec((None, tm, d), lambda b, j: (b + b_off, j, 0))
    in_specs = [part_tok, part_tok,
                pl.BlockSpec((None, None, GATE_ROWS, tm), lambda b, j: (b + b_off, j, 0, 0)),
                tok, _mod_spec(d, ncb, b_, b_off)]
    args = [y0, y1, gw, x, gt]
    out_specs = [tok]
    out_shape = [jax.ShapeDtypeStruct((b_, tn, d), F32)]
    with_next = next_mod is not None
    if with_next:
        sh, sc, g = next_mod
        in_specs += [_mod_spec(d, ncb, b_, b_off), _mod_spec(d, ncb, b_, b_off), _resident((1, d))]
        args += [sh, sc, g]
        out_specs.append(tok)
        out_shape.append(jax.ShapeDtypeStruct((b_, tn, d), BF16))
    aliases = {}
    n_prev = 0
    if prev is not None:
        n_prev = len(prev)
        for k, arr in enumerate(prev):
            aliases[len(args)] = k
            in_specs.append(pl.BlockSpec(memory_space=pl.ANY))
            args.append(arr)
    return pl.pallas_call(
        functools.partial(_combine_kernel, with_next=with_next, n_prev=n_prev),
        grid=(bp, tn // tm),
        in_specs=in_specs,
        out_specs=out_specs,
        out_shape=out_shape,
        input_output_aliases=aliases,
        compiler_params=_cparams(("parallel", "arbitrary")),
        name="moe_combine",
    )(*args)


def _mla_in_kernel(h_ref, win_ref, qg_ref, wq_ref, wqr_ref, tqc_ref, tqs_ref,
                   ktab_ref, wk_ref, wv_ref, kgain_ref, q_ref, k_ref, v_ref):
    tm = h_ref.shape[0]
    proj = _dot(h_ref[...], win_ref[...])

    cq = proj[:, :MLA_Q_LORA]
    qn = (_rms(cq) * qg_ref[...]).astype(BF16)
    p2 = proj[:, MLA_Q_LORA:]
    lane = lax.broadcasted_iota(jnp.int32, (tm, p2.shape[1]), 1)
    is_kv = lane < MLA_KV_LORA
    sq = p2 * p2
    ms_kv = jnp.sum(jnp.where(is_kv, sq, 0.0), axis=-1, keepdims=True) * (1.0 / MLA_KV_LORA)
    is_rope = (lane >= MLA_KV_LORA) & (lane < MLA_KV_LORA + MLA_ROPE)
    ss_rope = jnp.sum(jnp.where(is_rope, sq, 0.0), axis=-1, keepdims=True)
    mult = ktab_ref[...] * jnp.where(is_kv, lax.rsqrt(ms_kv + NORM_EPS), 1.0)
    lhs = (p2 * mult).astype(BF16)

    qa = _dot(qn, wq_ref[...])
    qr = _dot(qn, wqr_ref[...])
    ka = _dot(lhs, wk_ref[...])
    va = _dot(lhs[:, :MLA_KV_LORA], wv_ref[...])
    tqc = tqc_ref[...]
    tqs = tqs_ref[...]
    kgain = kgain_ref[...]
    lane1 = lax.broadcasted_iota(jnp.int32, (tm, LANES), 1)
    for h in range(MLA_HEADS):
        hs = slice(h * LANES, (h + 1) * LANES)
        a = qa[:, hs]
        rq = lax.rsqrt(jnp.sum(a * a, axis=-1, keepdims=True) * (1.0 / MLA_QK) + NORM_EPS)
        q_ref[h] = ((a * tqc + qr[:, hs] * tqs) * rq).astype(BF16)
        a = ka[:, hs]
        ssn = jnp.sum(jnp.where(lane1 < MLA_NOPE, a * a, 0.0), axis=-1, keepdims=True)
        rk = lax.rsqrt((ssn + ss_rope) * (1.0 / MLA_QK) + NORM_EPS)
        k_ref[h] = (a * kgain * rk).astype(BF16)
        ones_lane = MLA_V if h % 2 == 0 else 0
        v_ref[h] = jnp.where(lane1 == ones_lane, 1.0, va[:, hs]).astype(BF16)


def _mla_in_call(hm, win, qg, wq, wqr, tqc, tqs, ktab, wk, wv, kgain, n_ctx, tm):
    b_, t_, d = hm.shape
    ncb = n_ctx // tm
    tok = pl.BlockSpec((None, tm, d), lambda b, j: (b, j, 0))
    head = pl.BlockSpec((None, MLA_HEADS, tm, LANES), lambda b, j: (b, 0, j, 0))
    hshape = jax.ShapeDtypeStruct((b_, MLA_HEADS, t_, LANES), BF16)
    qhead = pl.BlockSpec((None, MLA_HEADS, tm, LANES), lambda b, j: (b, 0, jnp.maximum(j - ncb, 0), 0))
    qshape = jax.ShapeDtypeStruct((b_, MLA_HEADS, t_ - n_ctx, LANES), BF16)
    return pl.pallas_call(
        _mla_in_kernel,
        grid=(b_, t_ // tm),
        in_specs=[tok, _resident(win.shape), _resident(qg.shape), _resident(wq.shape),
                  _resident(wqr.shape),
                  pl.BlockSpec((tm, LANES), lambda b, j: (j, 0)),
                  pl.BlockSpec((tm, LANES), lambda b, j: (j, 0)),
                  pl.BlockSpec((tm, ktab.shape[1]), lambda b, j: (j, 0)),
                  _resident(wk.shape), _resident(wv.shape), _resident(kgain.shape)],
        out_specs=[qhead, head, head],
        out_shape=[qshape, hshape, hshape],
        compiler_params=_cparams(("parallel", "arbitrary")),
        name="mla_in",
    )(hm, win, qg, wq, wqr, tqc, tqs, ktab, wk, wv, kgain)


def _attn_kernel(q_ref, k_ref, v_ref, o_ref):
    n_heads, tq, _ = q_ref.shape
    t_ = k_ref.shape[1]
    ck = next(c for c in ATT_KEY_CHUNKS if t_ % c == 0)
    outs = []
    for hh in range(n_heads):
        q = q_ref[hh]
        m = jnp.full((tq, 1), -jnp.inf, F32)
        acc = jnp.zeros((tq, LANES), F32)
        for c in range(t_ // ck):
            ks = slice(c * ck, (c + 1) * ck)
            s = _dot_nt(q, k_ref[hh, ks, :])
            m_new = jnp.maximum(m, jnp.max(s, axis=-1, keepdims=True))
            p = jnp.exp2((s - m_new).astype(BF16))
            acc = acc * jnp.exp2(m - m_new) + _dot(p, v_ref[hh, ks, :])
            m = m_new
        ones_lane = MLA_V if hh % 2 == 0 else 0
        outs.append(acc / acc[:, ones_lane:ones_lane + 1])
    lane = lax.broadcasted_iota(jnp.int32, outs[0].shape, 1)
    for pr in range(len(outs) // 2):
        o_ref[:, pr * LANES:(pr + 1) * LANES] = jnp.where(
            lane < MLA_V, outs[2 * pr], outs[2 * pr + 1]).astype(o_ref.dtype)


def _attn_call(q, k, v, tq):
    b_, nh, l_, _ = q.shape
    t_ = k.shape[2]
    hps = ATT_HEADS_PER_STEP
    return pl.pallas_call(
        _attn_kernel,
        grid=(b_, nh // hps, l_ // tq),
        in_specs=[
            pl.BlockSpec((None, hps, tq, LANES), lambda b, h, i: (b, h, i, 0)),
            pl.BlockSpec((None, hps, t_, LANES), lambda b, h, i: (b, h, 0, 0)),
            pl.BlockSpec((None, hps, t_, LANES), lambda b, h, i: (b, h, 0, 0)),
        ],
        out_specs=pl.BlockSpec((None, tq, (hps // 2) * LANES), lambda b, h, i: (b, i, h)),
        out_shape=jax.ShapeDtypeStruct((b_, l_, (nh // 2) * LANES), BF16),
        compiler_params=_cparams(("parallel", "parallel", "arbitrary")),
        name="mla_attention",
    )(q, k, v)


def _dispatch_layout(eid, rank, counts):
    tm = eid.shape[-1]
    n_slot = eid.size
    starts = jnp.cumsum(counts) - counts
    padded = (counts + MOE_BLOCK - 1) // MOE_BLOCK * MOE_BLOCK
    pad_ends = jnp.cumsum(padded)
    pad_starts = pad_ends - padded
    n_blocks = (n_slot + MOE_BLOCK - 1) // MOE_BLOCK + MOE_EXPERTS
    block_start = jnp.arange(n_blocks, dtype=jnp.int32) * MOE_BLOCK
    block_expert = jnp.minimum(
        jnp.sum((block_start[:, None] >= pad_ends[None, :]).astype(jnp.int32), axis=1),
        MOE_EXPERTS - 1)
    n_used = pad_ends[-1:] // MOE_BLOCK
    slot_row = rank
    for e in range(MOE_EXPERTS):
        slot_row = slot_row + jnp.where(eid == e, pad_starts[e], 0)
    order = jnp.argsort(eid.reshape(n_slot)).astype(jnp.int32)
    offs = (block_start - pad_starts[block_expert])[:, None] + jnp.arange(MOE_BLOCK, dtype=jnp.int32)
    valid = offs < counts[block_expert][:, None]
    pos = jnp.minimum(starts[block_expert][:, None] + offs, n_slot - 1)
    slot = jnp.take(order, pos, mode='clip')
    tok = (slot // (MOE_TOP_K * tm)) * tm + slot % tm
    n_tok = n_slot // MOE_TOP_K
    filler = (block_start[:, None] + jnp.arange(MOE_BLOCK, dtype=jnp.int32)) % n_tok
    row_tok = jnp.where(valid, tok, filler)
    return row_tok.reshape(-1), slot_row, block_expert, n_used


def _moe(h2, eid, rank, counts, gw, x, gt, wg, wu, wd, layer, n_ctx, tm, next_mod=None):
    b_, tn, d = h2.shape
    n_tok = b_ * tn
    cnt = counts[MOE_GROUPS:MOE_GROUPS + MOE_EXPERTS, 0].astype(jnp.int32)
    row_tok, slot_row, block_expert, n_used = _dispatch_layout(eid, rank, cnt)
    r_pad = row_tok.shape[0]
    n_blocks = r_pad // MOE_BLOCK
    bounds = [n_blocks * p // EXPERT_PARTS for p in range(EXPERT_PARTS + 1)]
    h2_rows = h2.reshape(n_tok, d)
    ybuf = None
    for lo, hi in zip(bounds[:-1], bounds[1:]):
        part = jnp.take(h2_rows, row_tok[lo * MOE_BLOCK:hi * MOE_BLOCK], axis=0, mode='clip')
        ybuf = _expert_call(part, block_expert, n_used, wg, wu, wd, layer, lo, r_pad, ybuf)
    bp = b_ // COMBINE_PARTS
    outs = None
    for p in range(COMBINE_PARTS):
        rows = slot_row[p * bp:(p + 1) * bp]
        y0 = jnp.take(ybuf, rows[:, :, 0, :].reshape(bp * tn), axis=0, mode='clip').reshape(bp, tn, d)
        y1 = jnp.take(ybuf, rows[:, :, 1, :].reshape(bp * tn), axis=0, mode='clip').reshape(bp, tn, d)
        outs = _combine_call(y0, y1, gw, x, gt, n_ctx, tm, next_mod, p * bp, outs)
    return outs


def _rot_half_perm():
    half = MLA_ROPE // 2
    j = jnp.arange(MLA_ROPE)
    within = j % half
    base = j - within
    src = jnp.where(within < half // 2, base + within + half // 2, base + within - half // 2)
    sign = jnp.where(within < half // 2, -1.0, 1.0).astype(F32)
    return src, sign


def _rope_tables(n_ctx, seq):
    rows = seq // GRID_W
    row = jnp.repeat(jnp.arange(rows), GRID_W)
    col = jnp.tile(jnp.arange(GRID_W), rows)
    half = MLA_ROPE // 2
    inv_freq = ROPE_THETA ** (-jnp.arange(0, half, 2, dtype=F32) / half)
    ang = jnp.stack([row, col], axis=-1).astype(F32)[..., None] * inv_freq
    ang = jnp.concatenate([ang, ang], axis=-1).reshape(seq, MLA_ROPE)
    cos = jnp.concatenate([jnp.ones((n_ctx, MLA_ROPE), F32), jnp.cos(ang)], axis=0)
    sin = jnp.concatenate([jnp.zeros((n_ctx, MLA_ROPE), F32), jnp.sin(ang)], axis=0)
    return cos, sin


def _mla_prepare(w_in, w_qb, w_kvb, q_qk_g, k_qk_g, kv_norm_g, n_ctx, seq):
    d = w_in.shape[0]
    src, sign = _rot_half_perm()
    cos, sin = _rope_tables(n_ctx, seq)
    t_ = n_ctx + seq

    rope0 = MLA_Q_LORA + MLA_KV_LORA
    w_rope = w_in[:, rope0:rope0 + MLA_ROPE]
    win = jnp.concatenate(
        [w_in, w_rope[:, src] * sign, jnp.zeros((d, 512 - rope0 - 2 * MLA_ROPE), F32)], axis=1)

    wq3 = w_qb.reshape(MLA_Q_LORA, MLA_HEADS, MLA_QK)
    pad = jnp.zeros((MLA_Q_LORA, MLA_HEADS, LANES - MLA_QK), F32)
    wq = jnp.concatenate([wq3, pad], axis=-1).reshape(MLA_Q_LORA, MLA_HEADS * LANES)
    wq_rot = wq3[:, :, MLA_NOPE:][:, :, src] * sign
    wqr = jnp.concatenate([jnp.zeros((MLA_Q_LORA, MLA_HEADS, MLA_NOPE), F32), wq_rot, pad],
                          axis=-1).reshape(MLA_Q_LORA, MLA_HEADS * LANES)

    scale = MLA_QK ** -0.5 * 1.4426950408889634
    gq_n, gq_r = q_qk_g[:MLA_NOPE], q_qk_g[MLA_NOPE:]
    zpad = jnp.zeros((t_, LANES - MLA_QK), F32)
    tqc = jnp.concatenate([jnp.broadcast_to(gq_n, (t_, MLA_NOPE)), gq_r * cos, zpad], axis=1) * scale
    tqs = jnp.concatenate([jnp.zeros((t_, MLA_NOPE), F32), gq_r[src] * sin, zpad], axis=1) * scale

    wkv3 = w_kvb.reshape(MLA_KV_LORA, MLA_HEADS, MLA_NOPE + MLA_V)
    wk_lat = jnp.concatenate(
        [wkv3[:, :, :MLA_NOPE], jnp.zeros((MLA_KV_LORA, MLA_HEADS, LANES - MLA_NOPE), F32)], axis=-1)
    place = jnp.concatenate([jnp.zeros((MLA_ROPE, MLA_NOPE), F32), jnp.eye(MLA_ROPE, dtype=F32),
                             jnp.zeros((MLA_ROPE, LANES - MLA_QK), F32)], axis=1)
    place = jnp.broadcast_to(place[:, None, :], (MLA_ROPE, MLA_HEADS, LANES))
    wk = jnp.concatenate([wk_lat, place, place,
                          jnp.zeros((256 - MLA_KV_LORA - 2 * MLA_ROPE, MLA_HEADS, LANES), F32)],
                         axis=0).reshape(256, MLA_HEADS * LANES)
    gk_n, gk_r = k_qk_g[:MLA_NOPE], k_qk_g[MLA_NOPE:]
    ktab = jnp.concatenate([jnp.broadcast_to(kv_norm_g, (t_, MLA_KV_LORA)), gk_r * cos,
                            gk_r[src] * sin, jnp.zeros((t_, 256 - MLA_KV_LORA - 2 * MLA_ROPE), F32)],
                           axis=1)
    kgain = jnp.concatenate([gk_n, jnp.ones((MLA_ROPE,), F32),
                             jnp.zeros((LANES - MLA_QK,), F32)]).reshape(1, LANES)

    wv_h = wkv3[:, :, MLA_NOPE:]
    zv = jnp.zeros_like(wv_h)
    odd = (jnp.arange(MLA_HEADS) % 2 == 1)[None, :, None]
    wv = jnp.concatenate([jnp.where(odd, zv, wv_h), jnp.where(odd, wv_h, zv)],
                         axis=-1).reshape(MLA_KV_LORA, MLA_HEADS * LANES)
    return (win.astype(BF16), wq.astype(BF16), wqr.astype(BF16), tqc, tqs, ktab,
            wk.astype(BF16), wv.astype(BF16), kgain)


def _router_weights(w_group, w_expert):
    d = w_group.shape[0]
    wt = jnp.concatenate([w_group, w_expert], axis=1).T
    wt = jnp.concatenate([wt, jnp.zeros((ROUTER_ROWS - wt.shape[0], d), F32)], axis=0)
    head = wt.astype(BF16)
    rest = (wt - head.astype(F32)).astype(BF16)
    return jnp.concatenate([head, rest], axis=0)


def kernel(x, c, ctx, c_ctx, ada_w, ada_b, norm_mix_g, norm_ffn_g, hg_w_in, hg_lower_bounds, hg_out_norm_g, hg_w_out, mla_w_in, mla_q_norm_g, mla_kv_norm_g, mla_w_qb, mla_w_kvb, mla_q_qknorm_g, mla_k_qknorm_g, mla_w_out, moe_w_group, moe_w_expert, moe_w_gate, moe_w_up, moe_w_down):
    b_, seq, d = x.shape
    n_ctx = ctx.shape[1]
    tm = min(ROW_TILE, n_ctx)
    tq = min(ATT_Q_TILE, seq)
    assert n_ctx % GLA_BLOCK == 0 and seq % GLA_BLOCK == 0 and n_ctx % tm == 0 and seq % tm == 0
    assert b_ + 1 <= ADA_ROWS and seq % GRID_W == 0
    assert b_ % min(GLA_BATCH, b_) == 0 and b_ % COMBINE_PARTS == 0

    cc = jnp.concatenate([c, c_ctx[None, :], jnp.zeros((ADA_ROWS - b_ - 1, d), F32)], axis=0)
    mods = _ada_call(cc, ada_w, ada_b).reshape(DEPTH, ADA_CHUNKS, ADA_ROWS, 1, d)

    def mod(i, chunk):
        return mods[i, chunk]

    row = lambda v: v.reshape(1, -1)

    lower = jnp.cumsum(jax.nn.softmax(hg_lower_bounds.astype(F32), axis=1), axis=1)[:, 0]
    qs, kf, gf, kb, gb, v, gs = _hg_in_call(
        ctx, x, mod(0, 0), mod(0, 1), row(norm_mix_g[0]), hg_w_in[0].astype(BF16), lower, tm)
    o_b = _gla_call(qs, kb, gb, v, n_ctx, reverse=True)
    og = _gla_call(qs, kf, gf, v, n_ctx, reverse=False,
                   final_args=(o_b, gs, row(hg_out_norm_g[0])))
    x1, h2, eid, gw, rank, cnt = _outproj_call(
        og, hg_w_out[0].astype(BF16), x, 0, ctx, mod(0, 2), mod(0, 3), mod(0, 4),
        row(norm_ffn_g[0]), _router_weights(moe_w_group[0], moe_w_expert[0]), n_ctx, tm)
    x2, hm = _moe(h2, eid, rank, cnt, gw, x1, mod(0, 5), moe_w_gate, moe_w_up, moe_w_down, 0,
                  n_ctx, tm, next_mod=(mod(1, 0), mod(1, 1), row(norm_mix_g[1])))

    win, wq, wqr, tqc, tqs, ktab, wk, wv, kgain = _mla_prepare(
        mla_w_in[0], mla_w_qb[0], mla_w_kvb[0], mla_q_qknorm_g[0], mla_k_qknorm_g[0],
        mla_kv_norm_g[0], n_ctx, seq)
    q, k, vv = _mla_in_call(hm, win, row(mla_q_norm_g[0]), wq, wqr, tqc, tqs, ktab, wk, wv, kgain,
                            n_ctx, tm)
    o_att = _attn_call(q, k, vv, tq)
    x3, h2, eid, gw, rank, cnt = _outproj_call(
        o_att, mla_w_out[0].astype(BF16), x2, n_ctx, None, mod(1, 2), mod(1, 3), mod(1, 4),
        row(norm_ffn_g[1]), _router_weights(moe_w_group[1], moe_w_expert[1]), 0, tm)
    (x4,) = _moe(h2, eid, rank, cnt, gw, x3, mod(1, 5), moe_w_gate, moe_w_up, moe_w_down, 1, 0, tm)
    return x4
```

```python
import functools

import jax
import jax.numpy as jnp
from jax import lax
from jax.experimental import pallas as pl
from jax.experimental.pallas import tpu as pltpu

F32 = jnp.float32
BF16 = jnp.bfloat16

DEPTH = 2
ADA_CHUNKS = 6
NORM_EPS = 1e-6
GRID_W = 64
HG_HEADS = 8
HG_CHUNK = 32
MLA_HEADS = 16
MLA_Q_LORA = 256
MLA_KV_LORA = 128
MLA_NOPE = 64
MLA_ROPE = 32
MLA_V = 64
MLA_QK = MLA_NOPE + MLA_ROPE
ROPE_THETA = 10000.0
MOE_GROUPS = 4
MOE_EPG = 8
MOE_EXPERTS = MOE_GROUPS * MOE_EPG
MOE_TOP_K = 2
MOE_FF = 512

LANES = 128
ROUTER_ROWS = 48
GATE_ROWS = 8
VMEM_LIMIT = 56 * 1024 * 1024

ROW_TILE = 256
GLA_BLOCK = 128
GLA_BATCH = 8
MOE_BLOCK = 1024
ATT_Q_TILE = 512
ATT_HEADS_PER_STEP = 8
ATT_KEY_CHUNKS = (256, 128)
EXPERT_PARTS = 4
COMBINE_PARTS = 2
ADA_ROWS = 40


def _cparams(sem):
    return pltpu.CompilerParams(dimension_semantics=sem, vmem_limit_bytes=VMEM_LIMIT)


def _resident(shape):
    nd = len(shape)
    return pl.BlockSpec(shape, lambda *_: (0,) * nd, pipeline_mode=pl.Buffered(1))


def _dot(a, b):
    return jnp.dot(a, b, preferred_element_type=F32)


def _dot_nt(a, b):
    return lax.dot_general(a, b, (((1,), (1,)), ((), ())), preferred_element_type=F32)


def _dot_tn(a, b):
    return lax.dot_general(a, b, (((0,), (0,)), ((), ())), preferred_element_type=F32)


def _sigmoid(x):
    return 1.0 / (1.0 + jnp.exp(-x))


def _rms(x):
    return x * lax.rsqrt(jnp.mean(x * x, axis=-1, keepdims=True) + NORM_EPS)


def _ada_kernel(c_ref, w_ref, b_ref, o_ref):
    c = c_ref[...]
    a = (c * _sigmoid(c)).astype(BF16)
    o_ref[...] = _dot(a, w_ref[...].astype(BF16)) + b_ref[...]


def _ada_call(cc, ada_w, ada_b):
    depth, d, _ = ada_w.shape
    rows = cc.shape[0]
    return pl.pallas_call(
        _ada_kernel,
        grid=(depth, ADA_CHUNKS),
        in_specs=[
            pl.BlockSpec((rows, d), lambda i, j: (0, 0)),
            pl.BlockSpec((None, d, d), lambda i, j: (i, 0, j)),
            pl.BlockSpec((None, 1, d), lambda i, j: (i, 0, j)),
        ],
        out_specs=pl.BlockSpec((None, None, rows, d), lambda i, j: (i, j, 0, 0)),
        out_shape=jax.ShapeDtypeStruct((depth, ADA_CHUNKS, rows, d), F32),
        compiler_params=_cparams(("arbitrary", "arbitrary")),
        name="ada_mod",
    )(cc, ada_w, ada_b.reshape(depth, 1, ADA_CHUNKS * d))


def _mod_spec(d, n_ctx_blocks, ctx_row, b_off=0):
    def idx(b, j):
        return (jnp.where(j < n_ctx_blocks, ctx_row, b + b_off), 0, 0)
    return pl.BlockSpec((None, 1, d), idx)


def _hg_in_kernel(xc_ref, x_ref, sh_ref, sc_ref, g_ref, w_ref, lb_ref,
                  q_ref, kf_ref, gf_ref, kb_ref, gb_ref, v_ref, gs_ref, *, n_ctx_blocks):
    d = x_ref.shape[-1]
    x = jnp.where(pl.program_id(1) < n_ctx_blocks, xc_ref[...], x_ref[...])
    h = _rms(x) * g_ref[...]
    h = (h * (1.0 + sc_ref[...]) + sh_ref[...]).astype(BF16)

    def proj(c):
        return _dot(h, w_ref[:, c * d:(c + 1) * d])

    p = proj(0)
    q_ref[...] = (p * _sigmoid(p)).astype(BF16)
    for c, k_ref, lg_ref in ((1, kf_ref, gf_ref), (2, kb_ref, gb_ref)):
        s = _sigmoid(proj(c))
        lb = lb_ref[c - 1:c, :]
        k_ref[...] = ((1.0 - lb) * (1.0 - s)).astype(BF16)
        lg_ref[...] = jnp.log(lb + (1.0 - lb) * s).astype(BF16)
    v_ref[...] = proj(3).astype(BF16)
    p = proj(4)
    gs_ref[...] = (p * _sigmoid(p)).astype(BF16)


def _hg_in_call(ctx, x, sh, sc, g, w_in, lb, tm):
    b_, n_ctx, d = ctx.shape
    t_ = n_ctx + x.shape[1]
    ncb = n_ctx // tm
    tok = pl.BlockSpec((None, tm, d), lambda b, j: (b, j, 0))
    out = jax.ShapeDtypeStruct((b_, t_, d), BF16)
    return pl.pallas_call(
        functools.partial(_hg_in_kernel, n_ctx_blocks=ncb),
        grid=(b_, t_ // tm),
        in_specs=[pl.BlockSpec((None, tm, d), lambda b, j: (b, jnp.minimum(j, ncb - 1), 0)),
                  pl.BlockSpec((None, tm, d), lambda b, j: (b, jnp.maximum(j - ncb, 0), 0)),
                  _mod_spec(d, ncb, b_), _mod_spec(d, ncb, b_),
                  _resident((1, d)), _resident(w_in.shape), _resident((2, d))],
        out_specs=[tok] * 7,
        out_shape=[out] * 7,
        compiler_params=_cparams(("parallel", "arbitrary")),
        name="hg_in",
    )(ctx, x, sh, sc, g, w_in, lb)


def _gla_kernel(*refs, reverse, final, n_heads):
    if final:
        q_ref, k_ref, g_ref, v_ref, ob_ref, gs_ref, gn_ref, o_ref, st_ref = refs
    else:
        q_ref, k_ref, g_ref, v_ref, o_ref, st_ref = refs
    nb, tb, d = q_ref.shape
    dh = d // n_heads
    n_chunks = tb // HG_CHUNK

    @pl.when(pl.program_id(1) == 0)
    def _():
        st_ref[...] = jnp.zeros_like(st_ref)

    row = lax.broadcasted_iota(jnp.int32, (tb, tb), 0)
    col = lax.broadcasted_iota(jnp.int32, (tb, tb), 1)
    same_chunk = (row // HG_CHUNK) == (col // HG_CHUNK)
    tri = same_chunk & ((col >= row) if reverse else (col <= row))
    cum = jnp.where(tri, 1.0, 0.0).astype(BF16)
    chunk_order = range(n_chunks - 1, -1, -1) if reverse else range(n_chunks)

    qd, ki_b, e_end, k_end = [], [], [], []
    for n in range(nb):
        b = _dot(cum, g_ref[n])
        qd.append((q_ref[n].astype(F32) * jnp.exp(b)).astype(BF16))
        ki = k_ref[n].astype(F32) * jnp.exp(-b)
        ki_b.append(ki.astype(BF16))
        ee, ke = {}, {}
        for c in range(n_chunks):
            r0 = c * HG_CHUNK
            last = r0 if reverse else r0 + HG_CHUNK - 1
            ee[c] = jnp.exp(b[last:last + 1, :])
            ke[c] = (ki[r0:r0 + HG_CHUNK, :] * ee[c]).astype(BF16)
        e_end.append(ee)
        k_end.append(ke)

    chains = [(n, h) for n in range(nb) for h in range(n_heads)]
    hsl = [slice(h * dh, (h + 1) * dh) for h in range(n_heads)]
    scores = {(n, h): jnp.where(tri, _dot_nt(qd[n][:, hsl[h]], ki_b[n][:, hsl[h]]), 0.0).astype(BF16)
              for n, h in chains}
    o_intra = {(n, h): _dot(scores[n, h], v_ref[n, :, hsl[h]]) for n, h in chains}
    kv = {(n, h, c): _dot_tn(v_ref[n, c * HG_CHUNK:(c + 1) * HG_CHUNK, hsl[h]], k_end[n][c][:, hsl[h]])
          for n, h in chains for c in range(n_chunks)}
    st = {(n, h): st_ref[n, h] for n, h in chains}
    for c in chunk_order:
        rs = slice(c * HG_CHUNK, (c + 1) * HG_CHUNK)
        for n, h in chains:
            hs = hsl[h]
            o_c = o_intra[n, h][rs, :] + _dot_nt(qd[n][rs, hs], st[n, h].astype(BF16))
            st[n, h] = st[n, h] * e_end[n][c][:, hs] + kv[n, h, c]
            if final:
                o_c = o_c + ob_ref[n, rs, hs].astype(F32)
                o_c = _rms(o_c) * gn_ref[...]
                o_c = o_c * gs_ref[n, rs, hs].astype(F32)
            o_ref[n, rs, hs] = o_c.astype(o_ref.dtype)
    for n, h in chains:
        st_ref[n, h] = st[n, h]


def _gla_call(q, k, g, v, n_ctx, reverse, final_args=None):
    b_, t_, d = q.shape
    tb = GLA_BLOCK
    gb = min(GLA_BATCH, b_)
    nb = t_ // tb
    ncb = n_ctx // tb
    dh = d // HG_HEADS

    def blk(b, j):
        if reverse:
            jj = jnp.where(j < ncb, ncb - 1 - j, nb - 1 - (j - ncb))
        else:
            jj = j
        return (b, jj, 0)

    tok = pl.BlockSpec((gb, tb, d), blk)
    in_specs = [tok] * 4
    args = [q, k, g, v]
    final = final_args is not None
    if final:
        o_b, gs, gn = final_args
        in_specs += [tok, tok, _resident((1, dh))]
        args += [o_b, gs, gn]
    return pl.pallas_call(
        functools.partial(_gla_kernel, reverse=reverse, final=final, n_heads=HG_HEADS),
        grid=(b_ // gb, nb),
        in_specs=in_specs,
        out_specs=tok,
        out_shape=jax.ShapeDtypeStruct((b_, t_, d), BF16),
        scratch_shapes=[pltpu.VMEM((gb, HG_HEADS, dh, dh), F32)],
        compiler_params=_cparams(("parallel", "arbitrary")),
        name="gla_bwd" if reverse else "gla_fwd",
    )(*args)


def _outproj_kernel(*refs, n_ctx_blocks, split_residual):
    if split_residual:
        (a_ref, w_ref, xc_ref, x_ref, gt_ref, sh_ref, sc_ref, g_ref, wr_ref,
         xn_ref, h2_ref, eid_ref, gw_ref, rank_ref, cnt_ref, run_ref) = refs
    else:
        (a_ref, w_ref, x_ref, gt_ref, sh_ref, sc_ref, g_ref, wr_ref,
         xn_ref, h2_ref, eid_ref, gw_ref, rank_ref, cnt_ref, run_ref) = refs
    tm = a_ref.shape[0]
    nr = ROUTER_ROWS
    first = (pl.program_id(0) == 0) & (pl.program_id(1) == 0)

    @pl.when(first)
    def _():
        run_ref[...] = jnp.zeros_like(run_ref)

    x = x_ref[...]
    if split_residual:
        x = jnp.where(pl.program_id(1) < n_ctx_blocks, xc_ref[...], x)
    y = _dot(a_ref[...], w_ref[...])
    xn = x + gt_ref[...] * y
    xn_ref[...] = xn
    h2 = _rms(xn) * g_ref[...]
    h2 = h2 * (1.0 + sc_ref[...]) + sh_ref[...]
    h2b = h2.astype(BF16)
    h2_ref[...] = h2b
    lg2 = _dot_nt(wr_ref[...], h2b)
    lg = lg2[:nr] + lg2[nr:]

    ridx = lax.broadcasted_iota(jnp.int32, lg.shape, 0)
    neg = -jnp.inf
    gl = jnp.where(ridx < MOE_GROUPS, lg, neg)
    gmax = jnp.max(gl, axis=0, keepdims=True)
    g_idx = jnp.min(jnp.where(gl == gmax, ridx, nr), axis=0, keepdims=True)
    p_group = 1.0 / jnp.sum(jnp.exp(gl - gmax), axis=0, keepdims=True)
    lo = MOE_GROUPS + MOE_EPG * g_idx
    el = jnp.where((ridx >= lo) & (ridx < lo + MOE_EPG), lg, neg)
    m1 = jnp.max(el, axis=0, keepdims=True)
    i1 = jnp.min(jnp.where(el == m1, ridx, nr), axis=0, keepdims=True)
    el2 = jnp.where(ridx == i1, neg, el)
    m2 = jnp.max(el2, axis=0, keepdims=True)
    i2 = jnp.min(jnp.where(el2 == m2, ridx, nr), axis=0, keepdims=True)
    r21 = jnp.exp(m2 - m1)
    w1 = p_group / (1.0 + r21)
    w2 = w1 * r21

    hot1 = ridx == i1
    hot2 = ridx == i2
    f1 = jnp.where(hot1, 1.0, 0.0)
    f2 = jnp.where(hot2, 1.0, 0.0)
    r = lax.broadcasted_iota(jnp.int32, (tm, tm), 0)
    c = lax.broadcasted_iota(jnp.int32, (tm, tm), 1)
    earlier = jnp.where(r < c, 1.0, 0.0).astype(BF16)
    pre = _dot(jnp.concatenate([f1, f2], axis=0).astype(BF16), earlier)
    run = run_ref[:, 0:1]
    n1 = jnp.sum(f1, axis=1, keepdims=True)
    n2 = jnp.sum(f2, axis=1, keepdims=True)
    rank1 = jnp.sum(jnp.where(hot1, pre[:nr] + run, 0.0), axis=0, keepdims=True)
    rank2 = jnp.sum(jnp.where(hot2, pre[nr:] + (run + n1), 0.0), axis=0, keepdims=True)
    run = jnp.broadcast_to(run + n1 + n2, run_ref.shape)
    run_ref[...] = run
    cnt_ref[...] = run

    eid_ref[...] = jnp.concatenate([i1, i2], axis=0) - MOE_GROUPS
    rank_ref[...] = jnp.concatenate([rank1, rank2], axis=0).astype(jnp.int32)
    gw_ref[...] = jnp.concatenate([w1, w2, jnp.zeros((GATE_ROWS - MOE_TOP_K, tm), F32)], axis=0)


def _outproj_call(a, w, res_lat, lat_off, res_ctx, gt, sh, sc, g, wr, n_ctx, tm):
    b_, tn, d = a.shape
    ncb = n_ctx // tm
    nt = tn // tm
    lob = lat_off // tm
    split = res_ctx is not None
    tok = pl.BlockSpec((None, tm, d), lambda b, j: (b, j, 0))
    pair = pl.BlockSpec((None, None, MOE_TOP_K, tm), lambda b, j: (b, j, 0, 0))
    res_specs, res_args = [], []
    if split:
        res_specs.append(pl.BlockSpec((None, tm, d), lambda b, j: (b, jnp.minimum(j, ncb - 1), 0)))
        res_args.append(res_ctx)
        res_specs.append(pl.BlockSpec((None, tm, d), lambda b, j: (b, jnp.maximum(j - ncb, 0) + lob, 0)))
    else:
        res_specs.append(pl.BlockSpec((None, tm, d), lambda b, j: (b, j + lob, 0)))
    res_args.append(res_lat)
    return pl.pallas_call(
        functools.partial(_outproj_kernel, n_ctx_blocks=ncb, split_residual=split),
        grid=(b_, nt),
        in_specs=[tok, _resident(w.shape)] + res_specs + [
            _mod_spec(d, ncb, b_), _mod_spec(d, ncb, b_), _mod_spec(d, ncb, b_),
            _resident((1, d)), _resident(wr.shape)],
        out_specs=[tok, tok, pair,
                   pl.BlockSpec((None, None, GATE_ROWS, tm), lambda b, j: (b, j, 0, 0)), pair,
                   pl.BlockSpec((ROUTER_ROWS, LANES), lambda b, j: (0, 0))],
        out_shape=[jax.ShapeDtypeStruct((b_, tn, d), F32),
                   jax.ShapeDtypeStruct((b_, tn, d), BF16),
                   jax.ShapeDtypeStruct((b_, nt, MOE_TOP_K, tm), jnp.int32),
                   jax.ShapeDtypeStruct((b_, nt, GATE_ROWS, tm), F32),
                   jax.ShapeDtypeStruct((b_, nt, MOE_TOP_K, tm), jnp.int32),
                   jax.ShapeDtypeStruct((ROUTER_ROWS, LANES), F32)],
        scratch_shapes=[pltpu.VMEM((ROUTER_ROWS, LANES), F32)],
        compiler_params=_cparams(("arbitrary", "arbitrary")),
        name="outproj_ffnmod",
    )(a, w, *res_args, gt, sh, sc, g, wr)


def _expert_kernel(*refs, blk_off, aliased):
    if aliased:
        be_ref, nu_ref, x_ref, wg_ref, wu_ref, wd_ref, _, y_ref, wgb_ref, wub_ref, wdb_ref = refs
    else:
        be_ref, nu_ref, x_ref, wg_ref, wu_ref, wd_ref, y_ref, wgb_ref, wub_ref, wdb_ref = refs
    step = pl.program_id(0)
    i = step + blk_off
    new_expert = (step == 0) | (be_ref[i] != be_ref[jnp.maximum(i - 1, 0)])

    @pl.when(new_expert)
    def _():
        wgb_ref[...] = wg_ref[...].astype(BF16)
        wub_ref[...] = wu_ref[...].astype(BF16)
        wdb_ref[...] = wd_ref[...].astype(BF16)

    @pl.when(i < nu_ref[0])
    def _():
        rows = x_ref.shape[0] // 2
        halves = [slice(j * rows, (j + 1) * rows) for j in range(2)]
        gates = [_dot(x_ref[r, :], wgb_ref[...]) for r in halves]
        ups = [_dot(x_ref[r, :], wub_ref[...]) for r in halves]
        for j, r in enumerate(halves):
            act = (gates[j] * _sigmoid(gates[j]) * ups[j]).astype(BF16)
            y_ref[r, :] = _dot(act, wdb_ref[...]).astype(y_ref.dtype)

    @pl.when(i >= nu_ref[0])
    def _():
        y_ref[...] = jnp.zeros_like(y_ref)


def _expert_call(buf, block_expert, n_used, wg, wu, wd, layer, blk_off, r_pad, ybuf=None):
    rows, d = buf.shape
    n_part = rows // MOE_BLOCK
    ff = wg.shape[-1]
    aliased = ybuf is not None

    def x_idx(s, be, nu):
        return (jnp.clip(s, 0, jnp.maximum(nu[0] - blk_off - 1, 0)), 0)

    in_specs = [
        pl.BlockSpec((MOE_BLOCK, d), x_idx),
        pl.BlockSpec((None, None, d, ff), lambda s, be, nu: (layer, be[s + blk_off], 0, 0)),
        pl.BlockSpec((None, None, d, ff), lambda s, be, nu: (layer, be[s + blk_off], 0, 0)),
        pl.BlockSpec((None, None, ff, d), lambda s, be, nu: (layer, be[s + blk_off], 0, 0)),
    ]
    args = [block_expert, n_used, buf, wg, wu, wd]
    if aliased:
        in_specs.append(pl.BlockSpec(memory_space=pl.ANY))
        args.append(ybuf)
    grid_spec = pltpu.PrefetchScalarGridSpec(
        num_scalar_prefetch=2,
        grid=(n_part,),
        in_specs=in_specs,
        out_specs=pl.BlockSpec((MOE_BLOCK, d), lambda s, be, nu: (s + blk_off, 0)),
        scratch_shapes=[pltpu.VMEM((d, ff), BF16), pltpu.VMEM((d, ff), BF16),
                        pltpu.VMEM((ff, d), BF16)],
    )
    return pl.pallas_call(
        functools.partial(_expert_kernel, blk_off=blk_off, aliased=aliased),
        grid_spec=grid_spec,
        out_shape=jax.ShapeDtypeStruct((r_pad, d), BF16),
        input_output_aliases={len(args) - 1: 0} if aliased else {},
        compiler_params=_cparams(("arbitrary",)),
        name="moe_experts",
    )(*args)


def _combine_kernel(*refs, with_next, n_prev):
    if with_next:
        y0_ref, y1_ref, gw_ref, x_ref, gt_ref, sh_ref, sc_ref, g_ref = refs[:8]
        xo_ref, hn_ref = refs[8 + n_prev:]
    else:
        y0_ref, y1_ref, gw_ref, x_ref, gt_ref = refs[:5]
        (xo_ref,) = refs[5 + n_prev:]
    gw = gw_ref[...].T
    f = gw[:, 0:1] * y0_ref[...].astype(F32) + gw[:, 1:2] * y1_ref[...].astype(F32)
    xo = x_ref[...] + gt_ref[...] * f
    xo_ref[...] = xo
    if with_next:
        h = _rms(xo) * g_ref[...]
        hn_ref[...] = (h * (1.0 + sc_ref[...]) + sh_ref[...]).astype(BF16)


def _combine_call(y0, y1, gw, x, gt, n_ctx, tm, next_mod, b_off, prev):
    b_, tn, d = x.shape
    bp = y0.shape[0]
    ncb = n_ctx // tm
    part_tok = pl.BlockSpec((None, tm, d), lambda b, j: (b, j, 0))
    tok = pl.BlockSpec((None, tm, d), lambda b, j: (b + b_off, j, 0))
    in_specs = [part_tok, part_tok,
                pl.BlockSpec((None, None, GATE_ROWS, tm), lambda b, j: (b + b_off, j, 0, 0)),
                tok, _mod_spec(d, ncb, b_, b_off)]
    args = [y0, y1, gw, x, gt]
    out_specs = [tok]
    out_shape = [jax.ShapeDtypeStruct((b_, tn, d), F32)]
    with_next = next_mod is not None
    if with_next:
        sh, sc, g = next_mod
        in_specs += [_mod_spec(d, ncb, b_, b_off), _mod_spec(d, ncb, b_, b_off), _resident((1, d))]
        args += [sh, sc, g]
        out_specs.append(tok)
        out_shape.append(jax.ShapeDtypeStruct((b_, tn, d), BF16))
    aliases = {}
    n_prev = 0
    if prev is not None:
        n_prev = len(prev)
        for k, arr in enumerate(prev):
            aliases[len(args)] = k
            in_specs.append(pl.BlockSpec(memory_space=pl.ANY))
            args.append(arr)
    return pl.pallas_call(
        functools.partial(_combine_kernel, with_next=with_next, n_prev=n_prev),
        grid=(bp, tn // tm),
        in_specs=in_specs,
        out_specs=out_specs,
        out_shape=out_shape,
        input_output_aliases=aliases,
        compiler_params=_cparams(("parallel", "arbitrary")),
        name="moe_combine",
    )(*args)


def _mla_in_kernel(h_ref, win_ref, qg_ref, wq_ref, wqr_ref, tqc_ref, tqs_ref,
                   ktab_ref, wk_ref, wv_ref, kgain_ref, q_ref, k_ref, v_ref):
    tm = h_ref.shape[0]
    proj = _dot(h_ref[...], win_ref[...])

    cq = proj[:, :MLA_Q_LORA]
    qn = (_rms(cq) * qg_ref[...]).astype(BF16)
    p2 = proj[:, MLA_Q_LORA:]
    lane = lax.broadcasted_iota(jnp.int32, (tm, p2.shape[1]), 1)
    is_kv = lane < MLA_KV_LORA
    sq = p2 * p2
    ms_kv = jnp.sum(jnp.where(is_kv, sq, 0.0), axis=-1, keepdims=True) * (1.0 / MLA_KV_LORA)
    is_rope = (lane >= MLA_KV_LORA) & (lane < MLA_KV_LORA + MLA_ROPE)
    ss_rope = jnp.sum(jnp.where(is_rope, sq, 0.0), axis=-1, keepdims=True)
    mult = ktab_ref[...] * jnp.where(is_kv, lax.rsqrt(ms_kv + NORM_EPS), 1.0)
    lhs = (p2 * mult).astype(BF16)

    qa = _dot(qn, wq_ref[...])
    qr = _dot(qn, wqr_ref[...])
    ka = _dot(lhs, wk_ref[...])
    va = _dot(lhs[:, :MLA_KV_LORA], wv_ref[...])
    tqc = tqc_ref[...]
    tqs = tqs_ref[...]
    kgain = kgain_ref[...]
    lane1 = lax.broadcasted_iota(jnp.int32, (tm, LANES), 1)
    for h in range(MLA_HEADS):
        hs = slice(h * LANES, (h + 1) * LANES)
        a = qa[:, hs]
        rq = lax.rsqrt(jnp.sum(a * a, axis=-1, keepdims=True) * (1.0 / MLA_QK) + NORM_EPS)
        q_ref[h] = ((a * tqc + qr[:, hs] * tqs) * rq).astype(BF16)
        a = ka[:, hs]
        ssn = jnp.sum(jnp.where(lane1 < MLA_NOPE, a * a, 0.0), axis=-1, keepdims=True)
        rk = lax.rsqrt((ssn + ss_rope) * (1.0 / MLA_QK) + NORM_EPS)
        k_ref[h] = (a * kgain * rk).astype(BF16)
        ones_lane = MLA_V if h % 2 == 0 else 0
        v_ref[h] = jnp.where(lane1 == ones_lane, 1.0, va[:, hs]).astype(BF16)


def _mla_in_call(hm, win, qg, wq, wqr, tqc, tqs, ktab, wk, wv, kgain, n_ctx, tm):
    b_, t_, d = hm.shape
    ncb = n_ctx // tm
    tok = pl.BlockSpec((None, tm, d), lambda b, j: (b, j, 0))
    head = pl.BlockSpec((None, MLA_HEADS, tm, LANES), lambda b, j: (b, 0, j, 0))
    hshape = jax.ShapeDtypeStruct((b_, MLA_HEADS, t_, LANES), BF16)
    qhead = pl.BlockSpec((None, MLA_HEADS, tm, LANES), lambda b, j: (b, 0, jnp.maximum(j - ncb, 0), 0))
    qshape = jax.ShapeDtypeStruct((b_, MLA_HEADS, t_ - n_ctx, LANES), BF16)
    return pl.pallas_call(
        _mla_in_kernel,
        grid=(b_, t_ // tm),
        in_specs=[tok, _resident(win.shape), _resident(qg.shape), _resident(wq.shape),
                  _resident(wqr.shape),
                  pl.BlockSpec((tm, LANES), lambda b, j: (j, 0)),
                  pl.BlockSpec((tm, LANES), lambda b, j: (j, 0)),
                  pl.BlockSpec((tm, ktab.shape[1]), lambda b, j: (j, 0)),
                  _resident(wk.shape), _resident(wv.shape), _resident(kgain.shape)],
        out_specs=[qhead, head, head],
        out_shape=[qshape, hshape, hshape],
        compiler_params=_cparams(("parallel", "arbitrary")),
        name="mla_in",
    )(hm, win, qg, wq, wqr, tqc, tqs, ktab, wk, wv, kgain)


def _attn_kernel(q_ref, k_ref, v_ref, o_ref):
    n_heads, tq, _ = q_ref.shape
    t_ = k_ref.shape[1]
    ck = next(c for c in ATT_KEY_CHUNKS if t_ % c == 0)
    outs = []
    for hh in range(n_heads):
        q = q_ref[hh]
        m = jnp.full((tq, 1), -jnp.inf, F32)
        acc = jnp.zeros((tq, LANES), F32)
        for c in range(t_ // ck):
            ks = slice(c * ck, (c + 1) * ck)
            s = _dot_nt(q, k_ref[hh, ks, :])
            m_new = jnp.maximum(m, jnp.max(s, axis=-1, keepdims=True))
            p = jnp.exp2((s - m_new).astype(BF16))
            acc = acc * jnp.exp2(m - m_new) + _dot(p, v_ref[hh, ks, :])
            m = m_new
        ones_lane = MLA_V if hh % 2 == 0 else 0
        outs.append(acc / acc[:, ones_lane:ones_lane + 1])
    lane = lax.broadcasted_iota(jnp.int32, outs[0].shape, 1)
    for pr in range(len(outs) // 2):
        o_ref[:, pr * LANES:(pr + 1) * LANES] = jnp.where(
            lane < MLA_V, outs[2 * pr], outs[2 * pr + 1]).astype(o_ref.dtype)


def _attn_call(q, k, v, tq):
    b_, nh, l_, _ = q.shape
    t_ = k.shape[2]
    hps = ATT_HEADS_PER_STEP
    return pl.pallas_call(
        _attn_kernel,
        grid=(b_, nh // hps, l_ // tq),
        in_specs=[
            pl.BlockSpec((None, hps, tq, LANES), lambda b, h, i: (b, h, i, 0)),
            pl.BlockSpec((None, hps, t_, LANES), lambda b, h, i: (b, h, 0, 0)),
            pl.BlockSpec((None, hps, t_, LANES), lambda b, h, i: (b, h, 0, 0)),
        ],
        out_specs=pl.BlockSpec((None, tq, (hps // 2) * LANES), lambda b, h, i: (b, i, h)),
        out_shape=jax.ShapeDtypeStruct((b_, l_, (nh // 2) * LANES), BF16),
        compiler_params=_cparams(("parallel", "parallel", "arbitrary")),
        name="mla_attention",
    )(q, k, v)


def _dispatch_layout(eid, rank, counts):
    tm = eid.shape[-1]
    n_slot = eid.size
    starts = jnp.cumsum(counts) - counts
    padded = (counts + MOE_BLOCK - 1) // MOE_BLOCK * MOE_BLOCK
    pad_ends = jnp.cumsum(padded)
    pad_starts = pad_ends - padded
    n_blocks = (n_slot + MOE_BLOCK - 1) // MOE_BLOCK + MOE_EXPERTS
    block_start = jnp.arange(n_blocks, dtype=jnp.int32) * MOE_BLOCK
    block_expert = jnp.minimum(
        jnp.sum((block_start[:, None] >= pad_ends[None, :]).astype(jnp.int32), axis=1),
        MOE_EXPERTS - 1)
    n_used = pad_ends[-1:] // MOE_BLOCK
    slot_row = rank
    for e in range(MOE_EXPERTS):
        slot_row = slot_row + jnp.where(eid == e, pad_starts[e], 0)
    order = jnp.argsort(eid.reshape(n_slot)).astype(jnp.int32)
    offs = (block_start - pad_starts[block_expert])[:, None] + jnp.arange(MOE_BLOCK, dtype=jnp.int32)
    valid = offs < counts[block_expert][:, None]
    pos = jnp.minimum(starts[block_expert][:, None] + offs, n_slot - 1)
    slot = jnp.take(order, pos, mode='clip')
    tok = (slot // (MOE_TOP_K * tm)) * tm + slot % tm
    n_tok = n_slot // MOE_TOP_K
    filler = (block_start[:, None] + jnp.arange(MOE_BLOCK, dtype=jnp.int32)) % n_tok
    row_tok = jnp.where(valid, tok, filler)
    return row_tok.reshape(-1), slot_row, block_expert, n_used


def _moe(h2, eid, rank, counts, gw, x, gt, wg, wu, wd, layer, n_ctx, tm, next_mod=None):
    b_, tn, d = h2.shape
    n_tok = b_ * tn
    cnt = counts[MOE_GROUPS:MOE_GROUPS + MOE_EXPERTS, 0].astype(jnp.int32)
    row_tok, slot_row, block_expert, n_used = _dispatch_layout(eid, rank, cnt)
    r_pad = row_tok.shape[0]
    n_blocks = r_pad // MOE_BLOCK
    bounds = [n_blocks * p // EXPERT_PARTS for p in range(EXPERT_PARTS + 1)]
    h2_rows = h2.reshape(n_tok, d)
    ybuf = None
    for lo, hi in zip(bounds[:-1], bounds[1:]):
        part = jnp.take(h2_rows, row_tok[lo * MOE_BLOCK:hi * MOE_BLOCK], axis=0, mode='clip')
        ybuf = _expert_call(part, block_expert, n_used, wg, wu, wd, layer, lo, r_pad, ybuf)
    bp = b_ // COMBINE_PARTS
    outs = None
    for p in range(COMBINE_PARTS):
        rows = slot_row[p * bp:(p + 1) * bp]
        y0 = jnp.take(ybuf, rows[:, :, 0, :].reshape(bp * tn), axis=0, mode='clip').reshape(bp, tn, d)
        y1 = jnp.take(ybuf, rows[:, :, 1, :].reshape(bp * tn), axis=0, mode='clip').reshape(bp, tn, d)
        outs = _combine_call(y0, y1, gw, x, gt, n_ctx, tm, next_mod, p * bp, outs)
    return outs


def _rot_half_perm():
    half = MLA_ROPE // 2
    j = jnp.arange(MLA_ROPE)
    within = j % half
    base = j - within
    src = jnp.where(within < half // 2, base + within + half // 2, base + within - half // 2)
    sign = jnp.where(within < half // 2, -1.0, 1.0).astype(F32)
    return src, sign


def _rope_tables(n_ctx, seq):
    rows = seq // GRID_W
    row = jnp.repeat(jnp.arange(rows), GRID_W)
    col = jnp.tile(jnp.arange(GRID_W), rows)
    half = MLA_ROPE // 2
    inv_freq = ROPE_THETA ** (-jnp.arange(0, half, 2, dtype=F32) / half)
    ang = jnp.stack([row, col], axis=-1).astype(F32)[..., None] * inv_freq
    ang = jnp.concatenate([ang, ang], axis=-1).reshape(seq, MLA_ROPE)
    cos = jnp.concatenate([jnp.ones((n_ctx, MLA_ROPE), F32), jnp.cos(ang)], axis=0)
    sin = jnp.concatenate([jnp.zeros((n_ctx, MLA_ROPE), F32), jnp.sin(ang)], axis=0)
    return cos, sin


def _mla_prepare(w_in, w_qb, w_kvb, q_qk_g, k_qk_g, kv_norm_g, n_ctx, seq):
    d = w_in.shape[0]
    src, sign = _rot_half_perm()
    cos, sin = _rope_tables(n_ctx, seq)
    t_ = n_ctx + seq

    rope0 = MLA_Q_LORA + MLA_KV_LORA
    w_rope = w_in[:, rope0:rope0 + MLA_ROPE]
    win = jnp.concatenate(
        [w_in, w_rope[:, src] * sign, jnp.zeros((d, 512 - rope0 - 2 * MLA_ROPE), F32)], axis=1)

    wq3 = w_qb.reshape(MLA_Q_LORA, MLA_HEADS, MLA_QK)
    pad = jnp.zeros((MLA_Q_LORA, MLA_HEADS, LANES - MLA_QK), F32)
    wq = jnp.concatenate([wq3, pad], axis=-1).reshape(MLA_Q_LORA, MLA_HEADS * LANES)
    wq_rot = wq3[:, :, MLA_NOPE:][:, :, src] * sign
    wqr = jnp.concatenate([jnp.zeros((MLA_Q_LORA, MLA_HEADS, MLA_NOPE), F32), wq_rot, pad],
                          axis=-1).reshape(MLA_Q_LORA, MLA_HEADS * LANES)

    scale = MLA_QK ** -0.5 * 1.4426950408889634
    gq_n, gq_r = q_qk_g[:MLA_NOPE], q_qk_g[MLA_NOPE:]
    zpad = jnp.zeros((t_, LANES - MLA_QK), F32)
    tqc = jnp.concatenate([jnp.broadcast_to(gq_n, (t_, MLA_NOPE)), gq_r * cos, zpad], axis=1) * scale
    tqs = jnp.concatenate([jnp.zeros((t_, MLA_NOPE), F32), gq_r[src] * sin, zpad], axis=1) * scale

    wkv3 = w_kvb.reshape(MLA_KV_LORA, MLA_HEADS, MLA_NOPE + MLA_V)
    wk_lat = jnp.concatenate(
        [wkv3[:, :, :MLA_NOPE], jnp.zeros((MLA_KV_LORA, MLA_HEADS, LANES - MLA_NOPE), F32)], axis=-1)
    place = jnp.concatenate([jnp.zeros((MLA_ROPE, MLA_NOPE), F32), jnp.eye(MLA_ROPE, dtype=F32),
                             jnp.zeros((MLA_ROPE, LANES - MLA_QK), F32)], axis=1)
    place = jnp.broadcast_to(place[:, None, :], (MLA_ROPE, MLA_HEADS, LANES))
    wk = jnp.concatenate([wk_lat, place, place,
                          jnp.zeros((256 - MLA_KV_LORA - 2 * MLA_ROPE, MLA_HEADS, LANES), F32)],
                         axis=0).reshape(256, MLA_HEADS * LANES)
    gk_n, gk_r = k_qk_g[:MLA_NOPE], k_qk_g[MLA_NOPE:]
    ktab = jnp.concatenate([jnp.broadcast_to(kv_norm_g, (t_, MLA_KV_LORA)), gk_r * cos,
                            gk_r[src] * sin, jnp.zeros((t_, 256 - MLA_KV_LORA - 2 * MLA_ROPE), F32)],
                           axis=1)
    kgain = jnp.concatenate([gk_n, jnp.ones((MLA_ROPE,), F32),
                             jnp.zeros((LANES - MLA_QK,), F32)]).reshape(1, LANES)

    wv_h = wkv3[:, :, MLA_NOPE:]
    zv = jnp.zeros_like(wv_h)
    odd = (jnp.arange(MLA_HEADS) % 2 == 1)[None, :, None]
    wv = jnp.concatenate([jnp.where(odd, zv, wv_h), jnp.where(odd, wv_h, zv)],
                         axis=-1).reshape(MLA_KV_LORA, MLA_HEADS * LANES)
    return (win.astype(BF16), wq.astype(BF16), wqr.astype(BF16), tqc, tqs, ktab,
            wk.astype(BF16), wv.astype(BF16), kgain)


def _router_weights(w_group, w_expert):
    d = w_group.shape[0]
    wt = jnp.concatenate([w_group, w_expert], axis=1).T
    wt = jnp.concatenate([wt, jnp.zeros((ROUTER_ROWS - wt.shape[0], d), F32)], axis=0)
    head = wt.astype(BF16)
    rest = (wt - head.astype(F32)).astype(BF16)
    return jnp.concatenate([head, rest], axis=0)


def kernel(x, c, ctx, c_ctx, ada_w, ada_b, norm_mix_g, norm_ffn_g, hg_w_in, hg_lower_bounds, hg_out_norm_g, hg_w_out, mla_w_in, mla_q_norm_g, mla_kv_norm_g, mla_w_qb, mla_w_kvb, mla_q_qknorm_g, mla_k_qknorm_g, mla_w_out, moe_w_group, moe_w_expert, moe_w_gate, moe_w_up, moe_w_down):
    b_, seq, d = x.shape
    n_ctx = ctx.shape[1]
    tm = min(ROW_TILE, n_ctx)
    tq = min(ATT_Q_TILE, seq)
    assert n_ctx % GLA_BLOCK == 0 and seq % GLA_BLOCK == 0 and n_ctx % tm == 0 and seq % tm == 0
    assert b_ + 1 <= ADA_ROWS and seq % GRID_W == 0
    assert b_ % min(GLA_BATCH, b_) == 0 and b_ % COMBINE_PARTS == 0

    cc = jnp.concatenate([c, c_ctx[None, :], jnp.zeros((ADA_ROWS - b_ - 1, d), F32)], axis=0)
    mods = _ada_call(cc, ada_w, ada_b).reshape(DEPTH, ADA_CHUNKS, ADA_ROWS, 1, d)

    def mod(i, chunk):
        return mods[i, chunk]

    row = lambda v: v.reshape(1, -1)

    lower = jnp.cumsum(jax.nn.softmax(hg_lower_bounds.astype(F32), axis=1), axis=1)[:, 0]
    qs, kf, gf, kb, gb, v, gs = _hg_in_call(
        ctx, x, mod(0, 0), mod(0, 1), row(norm_mix_g[0]), hg_w_in[0].astype(BF16), lower, tm)
    o_b = _gla_call(qs, kb, gb, v, n_ctx, reverse=True)
    og = _gla_call(qs, kf, gf, v, n_ctx, reverse=False,
                   final_args=(o_b, gs, row(hg_out_norm_g[0])))
    x1, h2, eid, gw, rank, cnt = _outproj_call(
        og, hg_w_out[0].astype(BF16), x, 0, ctx, mod(0, 2), mod(0, 3), mod(0, 4),
        row(norm_ffn_g[0]), _router_weights(moe_w_group[0], moe_w_expert[0]), n_ctx, tm)
    x2, hm = _moe(h2, eid, rank, cnt, gw, x1, mod(0, 5), moe_w_gate, moe_w_up, moe_w_down, 0,
                  n_ctx, tm, next_mod=(mod(1, 0), mod(1, 1), row(norm_mix_g[1])))

    win, wq, wqr, tqc, tqs, ktab, wk, wv, kgain = _mla_prepare(
        mla_w_in[0], mla_w_qb[0], mla_w_kvb[0], mla_q_qknorm_g[0], mla_k_qknorm_g[0],
        mla_kv_norm_g[0], n_ctx, seq)
    q, k, vv = _mla_in_call(hm, win, row(mla_q_norm_g[0]), wq, wqr, tqc, tqs, ktab, wk, wv, kgain,
                            n_ctx, tm)
    o_att = _attn_call(q, k, vv, tq)
    x3, h2, eid, gw, rank, cnt = _outproj_call(
        o_att, mla_w_out[0].astype(BF16), x2, n_ctx, None, mod(1, 2), mod(1, 3), mod(1, 4),
        row(norm_ffn_g[1]), _router_weights(moe_w_group[1], moe_w_expert[1]), 0, tm)
    (x4,) = _moe(h2, eid, rank, cnt, gw, x3, mod(1, 5), moe_w_gate, moe_w_up, moe_w_down, 1, 0, tm)
    return x4
```

```python
import functools

import jax
import jax.numpy as jnp
from jax import lax
from jax.experimental import pallas as pl
from jax.experimental.pallas import tpu as pltpu

F32 = jnp.float32
BF16 = jnp.bfloat16

DEPTH = 2
ADA_CHUNKS = 6
NORM_EPS = 1e-6
GRID_W = 64
HG_HEADS = 8
HG_CHUNK = 32
HG_Q, HG_KF, HG_GF, HG_KB, HG_GB, HG_V, HG_GS = range(7)
HG_PLANES = 7
MLA_HEADS = 16
MLA_Q_LORA = 256
MLA_KV_LORA = 128
MLA_NOPE = 64
MLA_ROPE = 32
MLA_V = 64
MLA_QK = MLA_NOPE + MLA_ROPE
ROPE_THETA = 10000.0
MOE_GROUPS = 4
MOE_EPG = 8
MOE_EXPERTS = MOE_GROUPS * MOE_EPG
MOE_TOP_K = 2
MOE_FF = 512

LANES = 128
ROUTER_ROWS = 48
GATE_ROWS = 8
VMEM_LIMIT = 56 * 1024 * 1024

ROW_TILE = 256
GLA_BLOCK = 128
GLA_BATCH = 8
MOE_BLOCK = 512
ATT_Q_TILE = 512
ATT_HEADS_PER_STEP = 8
ATT_KEY_CHUNKS = (256, 128)
EXPERT_PARTS = 4
COMBINE_PARTS = 2
ADA_ROWS = 40


def _cparams(sem):
    return pltpu.CompilerParams(dimension_semantics=sem, vmem_limit_bytes=VMEM_LIMIT)


def _resident(shape):
    nd = len(shape)
    return pl.BlockSpec(shape, lambda *_: (0,) * nd, pipeline_mode=pl.Buffered(1))


def _dot(a, b):
    return jnp.dot(a, b, preferred_element_type=F32)


def _dot_nt(a, b):
    return lax.dot_general(a, b, (((1,), (1,)), ((), ())), preferred_element_type=F32)


def _dot_tn(a, b):
    return lax.dot_general(a, b, (((0,), (0,)), ((), ())), preferred_element_type=F32)


def _sigmoid(x):
    return 1.0 / (1.0 + jnp.exp(-x))


def _rms(x):
    return x * lax.rsqrt(jnp.mean(x * x, axis=-1, keepdims=True) + NORM_EPS)


def _ada_kernel(c_ref, w_ref, b_ref, o_ref):
    c = c_ref[...]
    a = (c * _sigmoid(c)).astype(BF16)
    o_ref[...] = _dot(a, w_ref[...].astype(BF16)) + b_ref[...]


def _ada_call(cc, ada_w, ada_b):
    depth, d, _ = ada_w.shape
    rows = cc.shape[0]
    return pl.pallas_call(
        _ada_kernel,
        grid=(depth, ADA_CHUNKS),
        in_specs=[
            pl.BlockSpec((rows, d), lambda i, j: (0, 0)),
            pl.BlockSpec((None, d, d), lambda i, j: (i, 0, j)),
            pl.BlockSpec((None, 1, d), lambda i, j: (i, 0, j)),
        ],
        out_specs=pl.BlockSpec((None, None, rows, d), lambda i, j: (i, j, 0, 0)),
        out_shape=jax.ShapeDtypeStruct((depth, ADA_CHUNKS, rows, d), F32),
        compiler_params=_cparams(("arbitrary", "arbitrary")),
        name="ada_mod",
    )(cc, ada_w, ada_b.reshape(depth, 1, ADA_CHUNKS * d))


def _mod_spec(d, n_ctx_blocks, ctx_row, b_off=0):
    def idx(b, j):
        return (jnp.where(j < n_ctx_blocks, ctx_row, b + b_off), 0, 0)
    return pl.BlockSpec((None, 1, d), idx)


def _hg_in_kernel(xc_ref, x_ref, sh_ref, sc_ref, g_ref, w_ref, lb_ref, o_ref, *, n_ctx_blocks):
    d = x_ref.shape[-1]
    x = jnp.where(pl.program_id(1) < n_ctx_blocks, xc_ref[...], x_ref[...])
    h = _rms(x) * g_ref[...]
    h = (h * (1.0 + sc_ref[...]) + sh_ref[...]).astype(BF16)

    def proj(c):
        return _dot(h, w_ref[:, c * d:(c + 1) * d])

    p = proj(0)
    o_ref[HG_Q] = (p * _sigmoid(p)).astype(BF16)
    for c, k_plane, lg_plane in ((1, HG_KF, HG_GF), (2, HG_KB, HG_GB)):
        s = _sigmoid(proj(c))
        lb = lb_ref[c - 1:c, :]
        o_ref[k_plane] = ((1.0 - lb) * (1.0 - s)).astype(BF16)
        o_ref[lg_plane] = jnp.log(lb + (1.0 - lb) * s).astype(BF16)
    o_ref[HG_V] = proj(3).astype(BF16)
    p = proj(4)
    o_ref[HG_GS] = (p * _sigmoid(p)).astype(BF16)


def _hg_in_call(ctx, x, sh, sc, g, w_in, lb, tm):
    b_, n_ctx, d = ctx.shape
    t_ = n_ctx + x.shape[1]
    ncb = n_ctx // tm
    return pl.pallas_call(
        functools.partial(_hg_in_kernel, n_ctx_blocks=ncb),
        grid=(b_, t_ // tm),
        in_specs=[pl.BlockSpec((None, tm, d), lambda b, j: (b, jnp.minimum(j, ncb - 1), 0)),
                  pl.BlockSpec((None, tm, d), lambda b, j: (b, jnp.maximum(j - ncb, 0), 0)),
                  _mod_spec(d, ncb, b_), _mod_spec(d, ncb, b_),
                  _resident((1, d)), _resident(w_in.shape), _resident((2, d))],
        out_specs=pl.BlockSpec((HG_PLANES, None, tm, d), lambda b, j: (0, b, j, 0)),
        out_shape=jax.ShapeDtypeStruct((HG_PLANES, b_, t_, d), BF16),
        compiler_params=_cparams(("parallel", "arbitrary")),
        name="hg_in",
    )(ctx, x, sh, sc, g, w_in, lb)


def _gla_kernel(*refs, reverse, final, n_heads):
    if final:
        q_ref, k_ref, g_ref, v_ref, ob_ref, gs_ref, gn_ref, o_ref, st_ref = refs
    else:
        q_ref, k_ref, g_ref, v_ref, o_ref, st_ref = refs
    nb, tb, d = q_ref.shape
    dh = d // n_heads
    n_chunks = tb // HG_CHUNK

    @pl.when(pl.program_id(1) == 0)
    def _():
        st_ref[...] = jnp.zeros_like(st_ref)

    row = lax.broadcasted_iota(jnp.int32, (tb, tb), 0)
    col = lax.broadcasted_iota(jnp.int32, (tb, tb), 1)
    same_chunk = (row // HG_CHUNK) == (col // HG_CHUNK)
    tri = same_chunk & ((col >= row) if reverse else (col <= row))
    cum = jnp.where(tri, 1.0, 0.0).astype(BF16)
    chunk_order = range(n_chunks - 1, -1, -1) if reverse else range(n_chunks)

    qd, ki_b, e_end, k_end = [], [], [], []
    for n in range(nb):
        b = _dot(cum, g_ref[n])
        qd.append((q_ref[n].astype(F32) * jnp.exp(b)).astype(BF16))
        ki = k_ref[n].astype(F32) * jnp.exp(-b)
        ki_b.append(ki.astype(BF16))
        ee, ke = {}, {}
        for c in range(n_chunks):
            r0 = c * HG_CHUNK
            last = r0 if reverse else r0 + HG_CHUNK - 1
            ee[c] = jnp.exp(b[last:last + 1, :])
            ke[c] = (ki[r0:r0 + HG_CHUNK, :] * ee[c]).astype(BF16)
        e_end.append(ee)
        k_end.append(ke)

    chains = [(n, h) for n in range(nb) for h in range(n_heads)]
    hsl = [slice(h * dh, (h + 1) * dh) for h in range(n_heads)]
    scores = {(n, h): jnp.where(tri, _dot_nt(qd[n][:, hsl[h]], ki_b[n][:, hsl[h]]), 0.0).astype(BF16)
              for n, h in chains}
    o_intra = {(n, h): _dot(scores[n, h], v_ref[n, :, hsl[h]]) for n, h in chains}
    kv = {(n, h, c): _dot_tn(v_ref[n, c * HG_CHUNK:(c + 1) * HG_CHUNK, hsl[h]], k_end[n][c][:, hsl[h]])
          for n, h in chains for c in range(n_chunks)}
    st = {(n, h): st_ref[n, h] for n, h in chains}
    for c in chunk_order:
        rs = slice(c * HG_CHUNK, (c + 1) * HG_CHUNK)
        for n, h in chains:
            hs = hsl[h]
            o_c = o_intra[n, h][rs, :] + _dot_nt(qd[n][rs, hs], st[n, h].astype(BF16))
            st[n, h] = st[n, h] * e_end[n][c][:, hs] + kv[n, h, c]
            if final:
                o_c = o_c + ob_ref[n, rs, hs].astype(F32)
                o_c = _rms(o_c) * gn_ref[...]
                o_c = o_c * gs_ref[n, rs, hs].astype(F32)
            o_ref[n, rs, hs] = o_c.astype(o_ref.dtype)
    for n, h in chains:
        st_ref[n, h] = st[n, h]


def _gla_call(planes, n_ctx, reverse, final_args=None):
    _, b_, t_, d = planes.shape
    tb = GLA_BLOCK
    gb = min(GLA_BATCH, b_)
    nb = t_ // tb
    ncb = n_ctx // tb
    dh = d // HG_HEADS

    def blk(b, j):
        if reverse:
            jj = jnp.where(j < ncb, ncb - 1 - j, nb - 1 - (j - ncb))
        else:
            jj = j
        return (b, jj, 0)

    def plane(p):
        return pl.BlockSpec((None, gb, tb, d), lambda b, j: (p,) + blk(b, j))

    tok = pl.BlockSpec((gb, tb, d), blk)
    k_plane, g_plane = (HG_KB, HG_GB) if reverse else (HG_KF, HG_GF)
    in_specs = [plane(HG_Q), plane(k_plane), plane(g_plane), plane(HG_V)]
    args = [planes] * 4
    final = final_args is not None
    if final:
        o_b, gn = final_args
        in_specs += [tok, plane(HG_GS), _resident((1, dh))]
        args += [o_b, planes, gn]
    return pl.pallas_call(
        functools.partial(_gla_kernel, reverse=reverse, final=final, n_heads=HG_HEADS),
        grid=(b_ // gb, nb),
        in_specs=in_specs,
        out_specs=tok,
        out_shape=jax.ShapeDtypeStruct((b_, t_, d), BF16),
        scratch_shapes=[pltpu.VMEM((gb, HG_HEADS, dh, dh), F32)],
        compiler_params=_cparams(("parallel", "arbitrary")),
        name="gla_bwd" if reverse else "gla_fwd",
    )(*args)


def _outproj_kernel(*refs, n_ctx_blocks, split_residual):
    if split_residual:
        (a_ref, w_ref, xc_ref, x_ref, gt_ref, sh_ref, sc_ref, g_ref, wr_ref,
         xn_ref, h2_ref, eid_ref, gw_ref, rank_ref, cnt_ref, run_ref) = refs
    else:
        (a_ref, w_ref, x_ref, gt_ref, sh_ref, sc_ref, g_ref, wr_ref,
         xn_ref, h2_ref, eid_ref, gw_ref, rank_ref, cnt_ref, run_ref) = refs
    tm = a_ref.shape[0]
    nr = ROUTER_ROWS
    first = (pl.program_id(0) == 0) & (pl.program_id(1) == 0)

    @pl.when(first)
    def _():
        run_ref[...] = jnp.zeros_like(run_ref)

    x = x_ref[...]
    if split_residual:
        x = jnp.where(pl.program_id(1) < n_ctx_blocks, xc_ref[...], x)
    y = _dot(a_ref[...], w_ref[...])
    xn = x + gt_ref[...] * y
    xn_ref[...] = xn
    h2 = _rms(xn) * g_ref[...]
    h2 = h2 * (1.0 + sc_ref[...]) + sh_ref[...]
    h2b = h2.astype(BF16)
    h2_ref[...] = h2b
    lg2 = _dot_nt(wr_ref[...], h2b)
    lg = lg2[:nr] + lg2[nr:]

    ridx = lax.broadcasted_iota(jnp.int32, lg.shape, 0)
    neg = -jnp.inf
    gl = jnp.where(ridx < MOE_GROUPS, lg, neg)
    gmax = jnp.max(gl, axis=0, keepdims=True)
    g_idx = jnp.min(jnp.where(gl == gmax, ridx, nr), axis=0, keepdims=True)
    p_group = 1.0 / jnp.sum(jnp.exp(gl - gmax), axis=0, keepdims=True)
    lo = MOE_GROUPS + MOE_EPG * g_idx
    el = jnp.where((ridx >= lo) & (ridx < lo + MOE_EPG), lg, neg)
    m1 = jnp.max(el, axis=0, keepdims=True)
    i1 = jnp.min(jnp.where(el == m1, ridx, nr), axis=0, keepdims=True)
    el2 = jnp.where(ridx == i1, neg, el)
    m2 = jnp.max(el2, axis=0, keepdims=True)
    i2 = jnp.min(jnp.where(el2 == m2, ridx, nr), axis=0, keepdims=True)
    r21 = jnp.exp(m2 - m1)
    w1 = p_group / (1.0 + r21)
    w2 = w1 * r21

    hot1 = ridx == i1
    hot2 = ridx == i2
    f1 = jnp.where(hot1, 1.0, 0.0)
    f2 = jnp.where(hot2, 1.0, 0.0)
    r = lax.broadcasted_iota(jnp.int32, (tm, tm), 0)
    c = lax.broadcasted_iota(jnp.int32, (tm, tm), 1)
    earlier = jnp.where(r < c, 1.0, 0.0).astype(BF16)
    pre = _dot(jnp.concatenate([f1, f2], axis=0).astype(BF16), earlier)
    run = run_ref[:, 0:1]
    n1 = jnp.sum(f1, axis=1, keepdims=True)
    n2 = jnp.sum(f2, axis=1, keepdims=True)
    rank1 = jnp.sum(jnp.where(hot1, pre[:nr] + run, 0.0), axis=0, keepdims=True)
    rank2 = jnp.sum(jnp.where(hot2, pre[nr:] + (run + n1), 0.0), axis=0, keepdims=True)
    run = jnp.broadcast_to(run + n1 + n2, run_ref.shape)
    run_ref[...] = run
    cnt_ref[...] = run

    eid_ref[...] = jnp.concatenate([i1, i2], axis=0) - MOE_GROUPS
    rank_ref[...] = jnp.concatenate([rank1, rank2], axis=0).astype(jnp.int32)
    gw_ref[...] = jnp.concatenate([w1, w2, jnp.zeros((GATE_ROWS - MOE_TOP_K, tm), F32)], axis=0)


def _outproj_call(a, w, res_lat, lat_off, res_ctx, gt, sh, sc, g, wr, n_ctx, tm):
    b_, tn, d = a.shape
    ncb = n_ctx // tm
    nt = tn // tm
    lob = lat_off // tm
    split = res_ctx is not None
    tok = pl.BlockSpec((None, tm, d), lambda b, j: (b, j, 0))
    pair = pl.BlockSpec((None, None, MOE_TOP_K, tm), lambda b, j: (b, j, 0, 0))
    res_specs, res_args = [], []
    if split:
        res_specs.append(pl.BlockSpec((None, tm, d), lambda b, j: (b, jnp.minimum(j, ncb - 1), 0)))
        res_args.append(res_ctx)
        res_specs.append(pl.BlockSpec((None, tm, d), lambda b, j: (b, jnp.maximum(j - ncb, 0) + lob, 0)))
    else:
        res_specs.append(pl.BlockSpec((None, tm, d), lambda b, j: (b, j + lob, 0)))
    res_args.append(res_lat)
    return pl.pallas_call(
        functools.partial(_outproj_kernel, n_ctx_blocks=ncb, split_residual=split),
        grid=(b_, nt),
        in_specs=[tok, _resident(w.shape)] + res_specs + [
            _mod_spec(d, ncb, b_), _mod_spec(d, ncb, b_), _mod_spec(d, ncb, b_),
            _resident((1, d)), _resident(wr.shape)],
        out_specs=[tok, tok, pair,
                   pl.BlockSpec((None, None, GATE_ROWS, tm), lambda b, j: (b, j, 0, 0)), pair,
                   pl.BlockSpec((ROUTER_ROWS, LANES), lambda b, j: (0, 0))],
        out_shape=[jax.ShapeDtypeStruct((b_, tn, d), F32),
                   jax.ShapeDtypeStruct((b_, tn, d), BF16),
                   jax.ShapeDtypeStruct((b_, nt, MOE_TOP_K, tm), jnp.int32),
                   jax.ShapeDtypeStruct((b_, nt, GATE_ROWS, tm), F32),
                   jax.ShapeDtypeStruct((b_, nt, MOE_TOP_K, tm), jnp.int32),
                   jax.ShapeDtypeStruct((ROUTER_ROWS, LANES), F32)],
        scratch_shapes=[pltpu.VMEM((ROUTER_ROWS, LANES), F32)],
        compiler_params=_cparams(("arbitrary", "arbitrary")),
        name="outproj_ffnmod",
    )(a, w, *res_args, gt, sh, sc, g, wr)


def _expert_kernel(*refs, blk_off, aliased):
    if aliased:
        be_ref, nu_ref, x_ref, wg_ref, wu_ref, wd_ref, _, y_ref, wgb_ref, wub_ref, wdb_ref = refs
    else:
        be_ref, nu_ref, x_ref, wg_ref, wu_ref, wd_ref, y_ref, wgb_ref, wub_ref, wdb_ref = refs
    step = pl.program_id(0)
    i = step + blk_off
    new_expert = (step == 0) | (be_ref[i] != be_ref[jnp.maximum(i - 1, 0)])

    @pl.when(new_expert)
    def _():
        wgb_ref[...] = wg_ref[...].astype(BF16)
        wub_ref[...] = wu_ref[...].astype(BF16)
        wdb_ref[...] = wd_ref[...].astype(BF16)

    @pl.when(i < nu_ref[0])
    def _():
        rows = x_ref.shape[0] // 2
        halves = [slice(j * rows, (j + 1) * rows) for j in range(2)]
        gates = [_dot(x_ref[r, :], wgb_ref[...]) for r in halves]
        ups = [_dot(x_ref[r, :], wub_ref[...]) for r in halves]
        for j, r in enumerate(halves):
            act = (gates[j] * _sigmoid(gates[j]) * ups[j]).astype(BF16)
            y_ref[r, :] = _dot(act, wdb_ref[...]).astype(y_ref.dtype)

    @pl.when(i >= nu_ref[0])
    def _():
        y_ref[...] = jnp.zeros_like(y_ref)


def _expert_call(buf, block_expert, n_used, wg, wu, wd, layer, blk_off, r_pad, ybuf=None):
    rows, d = buf.shape
    n_part = rows // MOE_BLOCK
    ff = wg.shape[-1]
    aliased = ybuf is not None

    def x_idx(s, be, nu):
        return (jnp.clip(s, 0, jnp.maximum(nu[0] - blk_off - 1, 0)), 0)

    in_specs = [
        pl.BlockSpec((MOE_BLOCK, d), x_idx),
        pl.BlockSpec((None, None, d, ff), lambda s, be, nu: (layer, be[s + blk_off], 0, 0)),
        pl.BlockSpec((None, None, d, ff), lambda s, be, nu: (layer, be[s + blk_off], 0, 0)),
        pl.BlockSpec((None, None, ff, d), lambda s, be, nu: (layer, be[s + blk_off], 0, 0)),
    ]
    args = [block_expert, n_used, buf, wg, wu, wd]
    if aliased:
        in_specs.append(pl.BlockSpec(memory_space=pl.ANY))
        args.append(ybuf)
    grid_spec = pltpu.PrefetchScalarGridSpec(
        num_scalar_prefetch=2,
        grid=(n_part,),
        in_specs=in_specs,
        out_specs=pl.BlockSpec((MOE_BLOCK, d), lambda s, be, nu: (s + blk_off, 0)),
        scratch_shapes=[pltpu.VMEM((d, ff), BF16), pltpu.VMEM((d, ff), BF16),
                        pltpu.VMEM((ff, d), BF16)],
    )
    return pl.pallas_call(
        functools.partial(_expert_kernel, blk_off=blk_off, aliased=aliased),
        grid_spec=grid_spec,
        out_shape=jax.ShapeDtypeStruct((r_pad, d), BF16),
        input_output_aliases={len(args) - 1: 0} if aliased else {},
        compiler_params=_cparams(("arbitrary",)),
        name="moe_experts",
    )(*args)


def _combine_kernel(*refs, with_next, n_prev):
    if with_next:
        y0_ref, y1_ref, gw_ref, x_ref, gt_ref, sh_ref, sc_ref, g_ref = refs[:8]
        xo_ref, hn_ref = refs[8 + n_prev:]
    else:
        y0_ref, y1_ref, gw_ref, x_ref, gt_ref = refs[:5]
        (xo_ref,) = refs[5 + n_prev:]
    gw = gw_ref[...].T
    f = gw[:, 0:1] * y0_ref[...].astype(F32) + gw[:, 1:2] * y1_ref[...].astype(F32)
    xo = x_ref[...] + gt_ref[...] * f
    xo_ref[...] = xo
    if with_next:
        h = _rms(xo) * g_ref[...]
        hn_ref[...] = (h * (1.0 + sc_ref[...]) + sh_ref[...]).astype(BF16)


def _combine_call(y0, y1, gw, x, gt, n_ctx, tm, next_mod, b_off, prev):
    b_, tn, d = x.shape
    bp = y0.shape[0]
    ncb = n_ctx // tm
    part_tok = pl.BlockSpec((None, tm, d), lambda b, j: (b, j, 0))
    tok = pl.BlockSpec((None, tm, d), lambda b, j: (b + b_off, j, 0))
    in_specs = [part_tok, part_tok,
                pl.BlockSpec((None, None, GATE_ROWS, tm), lambda b, j: (b + b_off, j, 0, 0)),
                tok, _mod_spec(d, ncb, b_, b_off)]
    args = [y0, y1, gw, x, gt]
    out_specs = [tok]
    out_shape = [jax.ShapeDtypeStruct((b_, tn, d), F32)]
    with_next = next_mod is not None
    if with_next:
        sh, sc, g = next_mod
        in_specs += [_mod_spec(d, ncb, b_, b_off), _mod_spec(d, ncb, b_, b_off), _resident((1, d))]
        args += [sh, sc, g]
        out_specs.append(tok)
        out_shape.append(jax.ShapeDtypeStruct((b_, tn, d), BF16))
    aliases = {}
    n_prev = 0
    if prev is not None:
        n_prev = len(prev)
        for k, arr in enumerate(prev):
            aliases[len(args)] = k
            in_specs.append(pl.BlockSpec(memory_space=pl.ANY))
            args.append(arr)
    return pl.pallas_call(
        functools.partial(_combine_kernel, with_next=with_next, n_prev=n_prev),
        grid=(bp, tn // tm),
        in_specs=in_specs,
        out_specs=out_specs,
        out_shape=out_shape,
        input_output_aliases=aliases,
        compiler_params=_cparams(("parallel", "arbitrary")),
        name="moe_combine",
    )(*args)


def _mla_in_kernel(h_ref, win_ref, qg_ref, wq_ref, wqr_ref, tqc_ref, tqs_ref,
                   ktab_ref, wk_ref, wv_ref, kgain_ref, q_ref, k_ref, v_ref):
    tm = h_ref.shape[0]
    proj = _dot(h_ref[...], win_ref[...])

    cq = proj[:, :MLA_Q_LORA]
    qn = (_rms(cq) * qg_ref[...]).astype(BF16)
    p2 = proj[:, MLA_Q_LORA:]
    lane = lax.broadcasted_iota(jnp.int32, (tm, p2.shape[1]), 1)
    is_kv = lane < MLA_KV_LORA
    sq = p2 * p2
    ms_kv = jnp.sum(jnp.where(is_kv, sq, 0.0), axis=-1, keepdims=True) * (1.0 / MLA_KV_LORA)
    is_rope = (lane >= MLA_KV_LORA) & (lane < MLA_KV_LORA + MLA_ROPE)
    ss_rope = jnp.sum(jnp.where(is_rope, sq, 0.0), axis=-1, keepdims=True)
    mult = ktab_ref[...] * jnp.where(is_kv, lax.rsqrt(ms_kv + NORM_EPS), 1.0)
    lhs = (p2 * mult).astype(BF16)

    qa = _dot(qn, wq_ref[...])
    qr = _dot(qn, wqr_ref[...])
    ka = _dot(lhs, wk_ref[...])
    va = _dot(lhs[:, :MLA_KV_LORA], wv_ref[...])
    tqc = tqc_ref[...]
    tqs = tqs_ref[...]
    kgain = kgain_ref[...]
    lane1 = lax.broadcasted_iota(jnp.int32, (tm, LANES), 1)
    for h in range(MLA_HEADS):
        hs = slice(h * LANES, (h + 1) * LANES)
        a = qa[:, hs]
        rq = lax.rsqrt(jnp.sum(a * a, axis=-1, keepdims=True) * (1.0 / MLA_QK) + NORM_EPS)
        q_ref[h] = ((a * tqc + qr[:, hs] * tqs) * rq).astype(BF16)
        a = ka[:, hs]
        ssn = jnp.sum(jnp.where(lane1 < MLA_NOPE, a * a, 0.0), axis=-1, keepdims=True)
        rk = lax.rsqrt((ssn + ss_rope) * (1.0 / MLA_QK) + NORM_EPS)
        k_ref[h] = (a * kgain * rk).astype(BF16)
        ones_lane = MLA_V if h % 2 == 0 else 0
        v_ref[h] = jnp.where(lane1 == ones_lane, 1.0, va[:, hs]).astype(BF16)


def _mla_in_call(hm, win, qg, wq, wqr, tqc, tqs, ktab, wk, wv, kgain, n_ctx, tm):
    b_, t_, d = hm.shape
    ncb = n_ctx // tm
    tok = pl.BlockSpec((None, tm, d), lambda b, j: (b, j, 0))
    head = pl.BlockSpec((None, MLA_HEADS, tm, LANES), lambda b, j: (b, 0, j, 0))
    hshape = jax.ShapeDtypeStruct((b_, MLA_HEADS, t_, LANES), BF16)
    qhead = pl.BlockSpec((None, MLA_HEADS, tm, LANES), lambda b, j: (b, 0, jnp.maximum(j - ncb, 0), 0))
    qshape = jax.ShapeDtypeStruct((b_, MLA_HEADS, t_ - n_ctx, LANES), BF16)
    return pl.pallas_call(
        _mla_in_kernel,
        grid=(b_, t_ // tm),
        in_specs=[tok, _resident(win.shape), _resident(qg.shape), _resident(wq.shape),
                  _resident(wqr.shape),
                  pl.BlockSpec((tm, LANES), lambda b, j: (j, 0)),
                  pl.BlockSpec((tm, LANES), lambda b, j: (j, 0)),
                  pl.BlockSpec((tm, ktab.shape[1]), lambda b, j: (j, 0)),
                  _resident(wk.shape), _resident(wv.shape), _resident(kgain.shape)],
        out_specs=[qhead, head, head],
        out_shape=[qshape, hshape, hshape],
        compiler_params=_cparams(("parallel", "arbitrary")),
        name="mla_in",
    )(hm, win, qg, wq, wqr, tqc, tqs, ktab, wk, wv, kgain)


def _attn_kernel(q_ref, k_ref, v_ref, o_ref):
    n_heads, tq, _ = q_ref.shape
    t_ = k_ref.shape[1]
    ck = next(c for c in ATT_KEY_CHUNKS if t_ % c == 0)
    outs = []
    for hh in range(n_heads):
        q = q_ref[hh]
        m = jnp.full((tq, 1), -jnp.inf, F32)
        acc = jnp.zeros((tq, LANES), F32)
        for c in range(t_ // ck):
            ks = slice(c * ck, (c + 1) * ck)
            s = _dot_nt(q, k_ref[hh, ks, :])
            m_new = jnp.maximum(m, jnp.max(s, axis=-1, keepdims=True))
            p = jnp.exp2((s - m_new).astype(BF16))
            acc = acc * jnp.exp2(m - m_new) + _dot(p, v_ref[hh, ks, :])
            m = m_new
        ones_lane = MLA_V if hh % 2 == 0 else 0
        outs.append(acc / acc[:, ones_lane:ones_lane + 1])
    lane = lax.broadcasted_iota(jnp.int32, outs[0].shape, 1)
    for pr in range(len(outs) // 2):
        o_ref[:, pr * LANES:(pr + 1) * LANES] = jnp.where(
            lane < MLA_V, outs[2 * pr], outs[2 * pr + 1]).astype(o_ref.dtype)


def _attn_call(q, k, v, tq):
    b_, nh, l_, _ = q.shape
    t_ = k.shape[2]
    hps = ATT_HEADS_PER_STEP
    return pl.pallas_call(
        _attn_kernel,
        grid=(b_, nh // hps, l_ // tq),
        in_specs=[
            pl.BlockSpec((None, hps, tq, LANES), lambda b, h, i: (b, h, i, 0)),
            pl.BlockSpec((None, hps, t_, LANES), lambda b, h, i: (b, h, 0, 0)),
            pl.BlockSpec((None, hps, t_, LANES), lambda b, h, i: (b, h, 0, 0)),
        ],
        out_specs=pl.BlockSpec((None, tq, (hps // 2) * LANES), lambda b, h, i: (b, i, h)),
        out_shape=jax.ShapeDtypeStruct((b_, l_, (nh // 2) * LANES), BF16),
        compiler_params=_cparams(("parallel", "parallel", "arbitrary")),
        name="mla_attention",
    )(q, k, v)


def _dispatch_layout(eid, rank, counts):
    tm = eid.shape[-1]
    n_slot = eid.size
    starts = jnp.cumsum(counts) - counts
    padded = (counts + MOE_BLOCK - 1) // MOE_BLOCK * MOE_BLOCK
    pad_ends = jnp.cumsum(padded)
    pad_starts = pad_ends - padded
    n_blocks = (n_slot + MOE_BLOCK - 1) // MOE_BLOCK + MOE_EXPERTS
    block_start = jnp.arange(n_blocks, dtype=jnp.int32) * MOE_BLOCK
    block_expert = jnp.minimum(
        jnp.sum((block_start[:, None] >= pad_ends[None, :]).astype(jnp.int32), axis=1),
        MOE_EXPERTS - 1)
    n_used = pad_ends[-1:] // MOE_BLOCK
    slot_row = rank
    for e in range(MOE_EXPERTS):
        slot_row = slot_row + jnp.where(eid == e, pad_starts[e], 0)
    order = jnp.argsort(eid.reshape(n_slot)).astype(jnp.int32)
    offs = (block_start - pad_starts[block_expert])[:, None] + jnp.arange(MOE_BLOCK, dtype=jnp.int32)
    valid = offs < counts[block_expert][:, None]
    pos = jnp.minimum(starts[block_expert][:, None] + offs, n_slot - 1)
    slot = jnp.take(order, pos, mode='clip')
    tok = (slot // (MOE_TOP_K * tm)) * tm + slot % tm
    n_tok = n_slot // MOE_TOP_K
    filler = (block_start[:, None] + jnp.arange(MOE_BLOCK, dtype=jnp.int32)) % n_tok
    row_tok = jnp.where(valid, tok, filler)
    return row_tok.reshape(-1), slot_row, block_expert, n_used


def _moe(h2, eid, rank, counts, gw, x, gt, wg, wu, wd, layer, n_ctx, tm, next_mod=None):
    b_, tn, d = h2.shape
    n_tok = b_ * tn
    cnt = counts[MOE_GROUPS:MOE_GROUPS + MOE_EXPERTS, 0].astype(jnp.int32)
    row_tok, slot_row, block_expert, n_used = _dispatch_layout(eid, rank, cnt)
    r_pad = row_tok.shape[0]
    n_blocks = r_pad // MOE_BLOCK
    bounds = [n_blocks * p // EXPERT_PARTS for p in range(EXPERT_PARTS + 1)]
    h2_rows = h2.reshape(n_tok, d)
    ybuf = None
    for lo, hi in zip(bounds[:-1], bounds[1:]):
        part = jnp.take(h2_rows, row_tok[lo * MOE_BLOCK:hi * MOE_BLOCK], axis=0, mode='clip')
        ybuf = _expert_call(part, block_expert, n_used, wg, wu, wd, layer, lo, r_pad, ybuf)
    bp = b_ // COMBINE_PARTS
    outs = None
    for p in range(COMBINE_PARTS):
        rows = slot_row[p * bp:(p + 1) * bp]
        y0 = jnp.take(ybuf, rows[:, :, 0, :].reshape(bp * tn), axis=0, mode='clip').reshape(bp, tn, d)
        y1 = jnp.take(ybuf, rows[:, :, 1, :].reshape(bp * tn), axis=0, mode='clip').reshape(bp, tn, d)
        outs = _combine_call(y0, y1, gw, x, gt, n_ctx, tm, next_mod, p * bp, outs)
    return outs


def _rot_half_perm():
    half = MLA_ROPE // 2
    j = jnp.arange(MLA_ROPE)
    within = j % half
    base = j - within
    src = jnp.where(within < half // 2, base + within + half // 2, base + within - half // 2)
    sign = jnp.where(within < half // 2, -1.0, 1.0).astype(F32)
    return src, sign


def _rope_tables(n_ctx, seq):
    rows = seq // GRID_W
    row = jnp.repeat(jnp.arange(rows), GRID_W)
    col = jnp.tile(jnp.arange(GRID_W), rows)
    half = MLA_ROPE // 2
    inv_freq = ROPE_THETA ** (-jnp.arange(0, half, 2, dtype=F32) / half)
    ang = jnp.stack([row, col], axis=-1).astype(F32)[..., None] * inv_freq
    ang = jnp.concatenate([ang, ang], axis=-1).reshape(seq, MLA_ROPE)
    cos = jnp.concatenate([jnp.ones((n_ctx, MLA_ROPE), F32), jnp.cos(ang)], axis=0)
    sin = jnp.concatenate([jnp.zeros((n_ctx, MLA_ROPE), F32), jnp.sin(ang)], axis=0)
    return cos, sin


def _mla_prepare(w_in, w_qb, w_kvb, q_qk_g, k_qk_g, kv_norm_g, n_ctx, seq):
    d = w_in.shape[0]
    src, sign = _rot_half_perm()
    cos, sin = _rope_tables(n_ctx, seq)
    t_ = n_ctx + seq

    rope0 = MLA_Q_LORA + MLA_KV_LORA
    w_rope = w_in[:, rope0:rope0 + MLA_ROPE]
    win = jnp.concatenate(
        [w_in, w_rope[:, src] * sign, jnp.zeros((d, 512 - rope0 - 2 * MLA_ROPE), F32)], axis=1)

    wq3 = w_qb.reshape(MLA_Q_LORA, MLA_HEADS, MLA_QK)
    pad = jnp.zeros((MLA_Q_LORA, MLA_HEADS, LANES - MLA_QK), F32)
    wq = jnp.concatenate([wq3, pad], axis=-1).reshape(MLA_Q_LORA, MLA_HEADS * LANES)
    wq_rot = wq3[:, :, MLA_NOPE:][:, :, src] * sign
    wqr = jnp.concatenate([jnp.zeros((MLA_Q_LORA, MLA_HEADS, MLA_NOPE), F32), wq_rot, pad],
                          axis=-1).reshape(MLA_Q_LORA, MLA_HEADS * LANES)

    scale = MLA_QK ** -0.5 * 1.4426950408889634
    gq_n, gq_r = q_qk_g[:MLA_NOPE], q_qk_g[MLA_NOPE:]
    zpad = jnp.zeros((t_, LANES - MLA_QK), F32)
    tqc = jnp.concatenate([jnp.broadcast_to(gq_n, (t_, MLA_NOPE)), gq_r * cos, zpad], axis=1) * scale
    tqs = jnp.concatenate([jnp.zeros((t_, MLA_NOPE), F32), gq_r[src] * sin, zpad], axis=1) * scale

    wkv3 = w_kvb.reshape(MLA_KV_LORA, MLA_HEADS, MLA_NOPE + MLA_V)
    wk_lat = jnp.concatenate(
        [wkv3[:, :, :MLA_NOPE], jnp.zeros((MLA_KV_LORA, MLA_HEADS, LANES - MLA_NOPE), F32)], axis=-1)
    place = jnp.concatenate([jnp.zeros((MLA_ROPE, MLA_NOPE), F32), jnp.eye(MLA_ROPE, dtype=F32),
                             jnp.zeros((MLA_ROPE, LANES - MLA_QK), F32)], axis=1)
    place = jnp.broadcast_to(place[:, None, :], (MLA_ROPE, MLA_HEADS, LANES))
    wk = jnp.concatenate([wk_lat, place, place,
                          jnp.zeros((256 - MLA_KV_LORA - 2 * MLA_ROPE, MLA_HEADS, LANES), F32)],
                         axis=0).reshape(256, MLA_HEADS * LANES)
    gk_n, gk_r = k_qk_g[:MLA_NOPE], k_qk_g[MLA_NOPE:]
    ktab = jnp.concatenate([jnp.broadcast_to(kv_norm_g, (t_, MLA_KV_LORA)), gk_r * cos,
                            gk_r[src] * sin, jnp.zeros((t_, 256 - MLA_KV_LORA - 2 * MLA_ROPE), F32)],
                           axis=1)
    kgain = jnp.concatenate([gk_n, jnp.ones((MLA_ROPE,), F32),
                             jnp.zeros((LANES - MLA_QK,), F32)]).reshape(1, LANES)

    wv_h = wkv3[:, :, MLA_NOPE:]
    zv = jnp.zeros_like(wv_h)
    odd = (jnp.arange(MLA_HEADS) % 2 == 1)[None, :, None]
    wv = jnp.concatenate([jnp.where(odd, zv, wv_h), jnp.where(odd, wv_h, zv)],
                         axis=-1).reshape(MLA_KV_LORA, MLA_HEADS * LANES)
    return (win.astype(BF16), wq.astype(BF16), wqr.astype(BF16), tqc, tqs, ktab,
            wk.astype(BF16), wv.astype(BF16), kgain)


def _router_weights(w_group, w_expert):
    d = w_group.shape[0]
    wt = jnp.concatenate([w_group, w_expert], axis=1).T
    wt = jnp.concatenate([wt, jnp.zeros((ROUTER_ROWS - wt.shape[0], d), F32)], axis=0)
    head = wt.astype(BF16)
    rest = (wt - head.astype(F32)).astype(BF16)
    return jnp.concatenate([head, rest], axis=0)


def kernel(x, c, ctx, c_ctx, ada_w, ada_b, norm_mix_g, norm_ffn_g, hg_w_in, hg_lower_bounds, hg_out_norm_g, hg_w_out, mla_w_in, mla_q_norm_g, mla_kv_norm_g, mla_w_qb, mla_w_kvb, mla_q_qknorm_g, mla_k_qknorm_g, mla_w_out, moe_w_group, moe_w_expert, moe_w_gate, moe_w_up, moe_w_down):
    b_, seq, d = x.shape
    n_ctx = ctx.shape[1]
    tm = min(ROW_TILE, n_ctx)
    tq = min(ATT_Q_TILE, seq)
    assert n_ctx % GLA_BLOCK == 0 and seq % GLA_BLOCK == 0 and n_ctx % tm == 0 and seq % tm == 0
    assert b_ + 1 <= ADA_ROWS and seq % GRID_W == 0
    assert b_ % min(GLA_BATCH, b_) == 0 and b_ % COMBINE_PARTS == 0

    cc = jnp.concatenate([c, c_ctx[None, :], jnp.zeros((ADA_ROWS - b_ - 1, d), F32)], axis=0)
    mods = _ada_call(cc, ada_w, ada_b).reshape(DEPTH, ADA_CHUNKS, ADA_ROWS, 1, d)

    def mod(i, chunk):
        return mods[i, chunk]

    row = lambda v: v.reshape(1, -1)

    lower = jnp.cumsum(jax.nn.softmax(hg_lower_bounds.astype(F32), axis=1), axis=1)[:, 0]
    planes = _hg_in_call(
        ctx, x, mod(0, 0), mod(0, 1), row(norm_mix_g[0]), hg_w_in[0].astype(BF16), lower, tm)
    o_b = _gla_call(planes, n_ctx, reverse=True)
    og = _gla_call(planes, n_ctx, reverse=False, final_args=(o_b, row(hg_out_norm_g[0])))
    x1, h2, eid, gw, rank, cnt = _outproj_call(
        og, hg_w_out[0].astype(BF16), x, 0, ctx, mod(0, 2), mod(0, 3), mod(0, 4),
        row(norm_ffn_g[0]), _router_weights(moe_w_group[0], moe_w_expert[0]), n_ctx, tm)
    x2, hm = _moe(h2, eid, rank, cnt, gw, x1, mod(0, 5), moe_w_gate, moe_w_up, moe_w_down, 0,
                  n_ctx, tm, next_mod=(mod(1, 0), mod(1, 1), row(norm_mix_g[1])))

    win, wq, wqr, tqc, tqs, ktab, wk, wv, kgain = _mla_prepare(
        mla_w_in[0], mla_w_qb[0], mla_w_kvb[0], mla_q_qknorm_g[0], mla_k_qknorm_g[0],
        mla_kv_norm_g[0], n_ctx, seq)
    q, k, vv = _mla_in_call(hm, win, row(mla_q_norm_g[0]), wq, wqr, tqc, tqs, ktab, wk, wv, kgain,
                            n_ctx, tm)
    o_att = _attn_call(q, k, vv, tq)
    x3, h2, eid, gw, rank, cnt = _outproj_call(
        o_att, mla_w_out[0].astype(BF16), x2, n_ctx, None, mod(1, 2), mod(1, 3), mod(1, 4),
        row(norm_ffn_g[1]), _router_weights(moe_w_group[1], moe_w_expert[1]), 0, tm)
    (x4,) = _moe(h2, eid, rank, cnt, gw, x3, mod(1, 5), moe_w_gate, moe_w_up, moe_w_down, 1, 0, tm)
    return x4
```

```python
import functools

import jax
import jax.numpy as jnp
from jax import lax
from jax.experimental import pallas as pl
from jax.experimental.pallas import tpu as pltpu

F32 = jnp.float32
BF16 = jnp.bfloat16

DEPTH = 2
ADA_CHUNKS = 6
NORM_EPS = 1e-6
GRID_W = 64
HG_HEADS = 8
HG_CHUNK = 32
HG_Q, HG_KF, HG_GF, HG_KB, HG_GB, HG_V, HG_GS = range(7)
HG_PLANES = 7
MLA_HEADS = 16
MLA_Q_LORA = 256
MLA_KV_LORA = 128
MLA_NOPE = 64
MLA_ROPE = 32
MLA_V = 64
MLA_QK = MLA_NOPE + MLA_ROPE
ROPE_THETA = 10000.0
MOE_GROUPS = 4
MOE_EPG = 8
MOE_EXPERTS = MOE_GROUPS * MOE_EPG
MOE_TOP_K = 2
MOE_FF = 512

LANES = 128
ROUTER_ROWS = 48
GATE_ROWS = 8
VMEM_LIMIT = 56 * 1024 * 1024

ROW_TILE = 256
GLA_BLOCK = 128
GLA_BATCH = 8
MOE_BLOCK = 512
ATT_Q_TILE = 512
ATT_HEADS_PER_STEP = 8
ATT_KEY_CHUNKS = (256, 128)
EXPERT_PARTS = 8
COMBINE_PARTS = 2
ADA_ROWS = 40


def _cparams(sem):
    return pltpu.CompilerParams(dimension_semantics=sem, vmem_limit_bytes=VMEM_LIMIT)


def _resident(shape):
    nd = len(shape)
    return pl.BlockSpec(shape, lambda *_: (0,) * nd, pipeline_mode=pl.Buffered(1))


def _dot(a, b):
    return jnp.dot(a, b, preferred_element_type=F32)


def _dot_nt(a, b):
    return lax.dot_general(a, b, (((1,), (1,)), ((), ())), preferred_element_type=F32)


def _dot_tn(a, b):
    return lax.dot_general(a, b, (((0,), (0,)), ((), ())), preferred_element_type=F32)


def _sigmoid(x):
    return 1.0 / (1.0 + jnp.exp(-x))


def _rms(x):
    return x * lax.rsqrt(jnp.mean(x * x, axis=-1, keepdims=True) + NORM_EPS)


def _ada_kernel(c_ref, w_ref, b_ref, o_ref):
    c = c_ref[...]
    a = (c * _sigmoid(c)).astype(BF16)
    o_ref[...] = _dot(a, w_ref[...].astype(BF16)) + b_ref[...]


def _ada_call(cc, ada_w, ada_b):
    depth, d, _ = ada_w.shape
    rows = cc.shape[0]
    return pl.pallas_call(
        _ada_kernel,
        grid=(depth, ADA_CHUNKS),
        in_specs=[
            pl.BlockSpec((rows, d), lambda i, j: (0, 0)),
            pl.BlockSpec((None, d, d), lambda i, j: (i, 0, j)),
            pl.BlockSpec((None, 1, d), lambda i, j: (i, 0, j)),
        ],
        out_specs=pl.BlockSpec((None, None, rows, d), lambda i, j: (i, j, 0, 0)),
        out_shape=jax.ShapeDtypeStruct((depth, ADA_CHUNKS, rows, d), F32),
        compiler_params=_cparams(("arbitrary", "arbitrary")),
        name="ada_mod",
    )(cc, ada_w, ada_b.reshape(depth, 1, ADA_CHUNKS * d))


def _mod_spec(d, n_ctx_blocks, ctx_row, b_off=0):
    def idx(b, j):
        return (jnp.where(j < n_ctx_blocks, ctx_row, b + b_off), 0, 0)
    return pl.BlockSpec((None, 1, d), idx)


def _hg_in_kernel(xc_ref, x_ref, sh_ref, sc_ref, g_ref, w_ref, lb_ref, o_ref, *, n_ctx_blocks):
    d = x_ref.shape[-1]
    x = jnp.where(pl.program_id(1) < n_ctx_blocks, xc_ref[...], x_ref[...])
    h = _rms(x) * g_ref[...]
    h = (h * (1.0 + sc_ref[...]) + sh_ref[...]).astype(BF16)

    def proj(c):
        return _dot(h, w_ref[:, c * d:(c + 1) * d])

    p = proj(0)
    o_ref[HG_Q] = (p * _sigmoid(p)).astype(BF16)
    for c, k_plane, lg_plane in ((1, HG_KF, HG_GF), (2, HG_KB, HG_GB)):
        s = _sigmoid(proj(c))
        lb = lb_ref[c - 1:c, :]
        o_ref[k_plane] = ((1.0 - lb) * (1.0 - s)).astype(BF16)
        o_ref[lg_plane] = jnp.log(lb + (1.0 - lb) * s).astype(BF16)
    o_ref[HG_V] = proj(3).astype(BF16)
    p = proj(4)
    o_ref[HG_GS] = (p * _sigmoid(p)).astype(BF16)


def _hg_in_call(ctx, x, sh, sc, g, w_in, lb, tm):
    b_, n_ctx, d = ctx.shape
    t_ = n_ctx + x.shape[1]
    ncb = n_ctx // tm
    return pl.pallas_call(
        functools.partial(_hg_in_kernel, n_ctx_blocks=ncb),
        grid=(b_, t_ // tm),
        in_specs=[pl.BlockSpec((None, tm, d), lambda b, j: (b, jnp.minimum(j, ncb - 1), 0)),
                  pl.BlockSpec((None, tm, d), lambda b, j: (b, jnp.maximum(j - ncb, 0), 0)),
                  _mod_spec(d, ncb, b_), _mod_spec(d, ncb, b_),
                  _resident((1, d)), _resident(w_in.shape), _resident((2, d))],
        out_specs=pl.BlockSpec((HG_PLANES, None, tm, d), lambda b, j: (0, b, j, 0)),
        out_shape=jax.ShapeDtypeStruct((HG_PLANES, b_, t_, d), BF16),
        compiler_params=_cparams(("parallel", "arbitrary")),
        name="hg_in",
    )(ctx, x, sh, sc, g, w_in, lb)


def _gla_kernel(*refs, reverse, final, n_heads):
    if final:
        q_ref, k_ref, g_ref, v_ref, ob_ref, gs_ref, gn_ref, o_ref, st_ref = refs
    else:
        q_ref, k_ref, g_ref, v_ref, o_ref, st_ref = refs
    nb, tb, d = q_ref.shape
    dh = d // n_heads
    n_chunks = tb // HG_CHUNK

    @pl.when(pl.program_id(1) == 0)
    def _():
        st_ref[...] = jnp.zeros_like(st_ref)

    row = lax.broadcasted_iota(jnp.int32, (tb, tb), 0)
    col = lax.broadcasted_iota(jnp.int32, (tb, tb), 1)
    same_chunk = (row // HG_CHUNK) == (col // HG_CHUNK)
    tri = same_chunk & ((col >= row) if reverse else (col <= row))
    cum = jnp.where(tri, 1.0, 0.0).astype(BF16)
    chunk_order = range(n_chunks - 1, -1, -1) if reverse else range(n_chunks)

    qd, ki_b, e_end, k_end = [], [], [], []
    for n in range(nb):
        b = _dot(cum, g_ref[n])
        qd.append((q_ref[n].astype(F32) * jnp.exp(b)).astype(BF16))
        ki = k_ref[n].astype(F32) * jnp.exp(-b)
        ki_b.append(ki.astype(BF16))
        ee, ke = {}, {}
        for c in range(n_chunks):
            r0 = c * HG_CHUNK
            last = r0 if reverse else r0 + HG_CHUNK - 1
            ee[c] = jnp.exp(b[last:last + 1, :])
            ke[c] = (ki[r0:r0 + HG_CHUNK, :] * ee[c]).astype(BF16)
        e_end.append(ee)
        k_end.append(ke)

    chains = [(n, h) for n in range(nb) for h in range(n_heads)]
    hsl = [slice(h * dh, (h + 1) * dh) for h in range(n_heads)]
    scores = {(n, h): jnp.where(tri, _dot_nt(qd[n][:, hsl[h]], ki_b[n][:, hsl[h]]), 0.0).astype(BF16)
              for n, h in chains}
    o_intra = {(n, h): _dot(scores[n, h], v_ref[n, :, hsl[h]]) for n, h in chains}
    kv = {(n, h, c): _dot_tn(v_ref[n, c * HG_CHUNK:(c + 1) * HG_CHUNK, hsl[h]], k_end[n][c][:, hsl[h]])
          for n, h in chains for c in range(n_chunks)}
    st = {(n, h): st_ref[n, h] for n, h in chains}
    for c in chunk_order:
        rs = slice(c * HG_CHUNK, (c + 1) * HG_CHUNK)
        for n, h in chains:
            hs = hsl[h]
            o_c = o_intra[n, h][rs, :] + _dot_nt(qd[n][rs, hs], st[n, h].astype(BF16))
            st[n, h] = st[n, h] * e_end[n][c][:, hs] + kv[n, h, c]
            if final:
                o_c = o_c + ob_ref[n, rs, hs].astype(F32)
                o_c = _rms(o_c) * gn_ref[...]
                o_c = o_c * gs_ref[n, rs, hs].astype(F32)
            o_ref[n, rs, hs] = o_c.astype(o_ref.dtype)
    for n, h in chains:
        st_ref[n, h] = st[n, h]


def _gla_call(planes, n_ctx, reverse, final_args=None):
    _, b_, t_, d = planes.shape
    tb = GLA_BLOCK
    gb = min(GLA_BATCH, b_)
    nb = t_ // tb
    ncb = n_ctx // tb
    dh = d // HG_HEADS

    def blk(b, j):
        if reverse:
            jj = jnp.where(j < ncb, ncb - 1 - j, nb - 1 - (j - ncb))
        else:
            jj = j
        return (b, jj, 0)

    def plane(p):
        return pl.BlockSpec((None, gb, tb, d), lambda b, j: (p,) + blk(b, j))

    tok = pl.BlockSpec((gb, tb, d), blk)
    k_plane, g_plane = (HG_KB, HG_GB) if reverse else (HG_KF, HG_GF)
    in_specs = [plane(HG_Q), plane(k_plane), plane(g_plane), plane(HG_V)]
    args = [planes] * 4
    final = final_args is not None
    if final:
        o_b, gn = final_args
        in_specs += [tok, plane(HG_GS), _resident((1, dh))]
        args += [o_b, planes, gn]
    return pl.pallas_call(
        functools.partial(_gla_kernel, reverse=reverse, final=final, n_heads=HG_HEADS),
        grid=(b_ // gb, nb),
        in_specs=in_specs,
        out_specs=tok,
        out_shape=jax.ShapeDtypeStruct((b_, t_, d), BF16),
        scratch_shapes=[pltpu.VMEM((gb, HG_HEADS, dh, dh), F32)],
        compiler_params=_cparams(("parallel", "arbitrary")),
        name="gla_bwd" if reverse else "gla_fwd",
    )(*args)


def _outproj_kernel(*refs, n_ctx_blocks, split_residual):
    if split_residual:
        (a_ref, w_ref, xc_ref, x_ref, gt_ref, sh_ref, sc_ref, g_ref, wr_ref,
         xn_ref, h2_ref, eid_ref, gw_ref, rank_ref, cnt_ref, run_ref) = refs
    else:
        (a_ref, w_ref, x_ref, gt_ref, sh_ref, sc_ref, g_ref, wr_ref,
         xn_ref, h2_ref, eid_ref, gw_ref, rank_ref, cnt_ref, run_ref) = refs
    tm = a_ref.shape[0]
    nr = ROUTER_ROWS
    first = (pl.program_id(0) == 0) & (pl.program_id(1) == 0)

    @pl.when(first)
    def _():
        run_ref[...] = jnp.zeros_like(run_ref)

    x = x_ref[...]
    if split_residual:
        x = jnp.where(pl.program_id(1) < n_ctx_blocks, xc_ref[...], x)
    y = _dot(a_ref[...], w_ref[...])
    xn = x + gt_ref[...] * y
    xn_ref[...] = xn
    h2 = _rms(xn) * g_ref[...]
    h2 = h2 * (1.0 + sc_ref[...]) + sh_ref[...]
    h2b = h2.astype(BF16)
    h2_ref[...] = h2b
    lg2 = _dot_nt(wr_ref[...], h2b)
    lg = lg2[:nr] + lg2[nr:]

    ridx = lax.broadcasted_iota(jnp.int32, lg.shape, 0)
    neg = -jnp.inf
    gl = jnp.where(ridx < MOE_GROUPS, lg, neg)
    gmax = jnp.max(gl, axis=0, keepdims=True)
    g_idx = jnp.min(jnp.where(gl == gmax, ridx, nr), axis=0, keepdims=True)
    p_group = 1.0 / jnp.sum(jnp.exp(gl - gmax), axis=0, keepdims=True)
    lo = MOE_GROUPS + MOE_EPG * g_idx
    el = jnp.where((ridx >= lo) & (ridx < lo + MOE_EPG), lg, neg)
    m1 = jnp.max(el, axis=0, keepdims=True)
    i1 = jnp.min(jnp.where(el == m1, ridx, nr), axis=0, keepdims=True)
    el2 = jnp.where(ridx == i1, neg, el)
    m2 = jnp.max(el2, axis=0, keepdims=True)
    i2 = jnp.min(jnp.where(el2 == m2, ridx, nr), axis=0, keepdims=True)
    r21 = jnp.exp(m2 - m1)
    w1 = p_group / (1.0 + r21)
    w2 = w1 * r21

    hot1 = ridx == i1
    hot2 = ridx == i2
    f1 = jnp.where(hot1, 1.0, 0.0)
    f2 = jnp.where(hot2, 1.0, 0.0)
    r = lax.broadcasted_iota(jnp.int32, (tm, tm), 0)
    c = lax.broadcasted_iota(jnp.int32, (tm, tm), 1)
    earlier = jnp.where(r < c, 1.0, 0.0).astype(BF16)
    pre = _dot(jnp.concatenate([f1, f2], axis=0).astype(BF16), earlier)
    run = run_ref[:, 0:1]
    n1 = jnp.sum(f1, axis=1, keepdims=True)
    n2 = jnp.sum(f2, axis=1, keepdims=True)
    rank1 = jnp.sum(jnp.where(hot1, pre[:nr] + run, 0.0), axis=0, keepdims=True)
    rank2 = jnp.sum(jnp.where(hot2, pre[nr:] + (run + n1), 0.0), axis=0, keepdims=True)
    run = jnp.broadcast_to(run + n1 + n2, run_ref.shape)
    run_ref[...] = run
    cnt_ref[...] = run

    eid_ref[...] = jnp.concatenate([i1, i2], axis=0) - MOE_GROUPS
    rank_ref[...] = jnp.concatenate([rank1, rank2], axis=0).astype(jnp.int32)
    gw_ref[...] = jnp.concatenate([w1, w2, jnp.zeros((GATE_ROWS - MOE_TOP_K, tm), F32)], axis=0)


def _outproj_call(a, w, res_lat, lat_off, res_ctx, gt, sh, sc, g, wr, n_ctx, tm):
    b_, tn, d = a.shape
    ncb = n_ctx // tm
    nt = tn // tm
    lob = lat_off // tm
    split = res_ctx is not None
    tok = pl.BlockSpec((None, tm, d), lambda b, j: (b, j, 0))
    pair = pl.BlockSpec((None, None, MOE_TOP_K, tm), lambda b, j: (b, j, 0, 0))
    res_specs, res_args = [], []
    if split:
        res_specs.append(pl.BlockSpec((None, tm, d), lambda b, j: (b, jnp.minimum(j, ncb - 1), 0)))
        res_args.append(res_ctx)
        res_specs.append(pl.BlockSpec((None, tm, d), lambda b, j: (b, jnp.maximum(j - ncb, 0) + lob, 0)))
    else:
        res_specs.append(pl.BlockSpec((None, tm, d), lambda b, j: (b, j + lob, 0)))
    res_args.append(res_lat)
    return pl.pallas_call(
        functools.partial(_outproj_kernel, n_ctx_blocks=ncb, split_residual=split),
        grid=(b_, nt),
        in_specs=[tok, _resident(w.shape)] + res_specs + [
            _mod_spec(d, ncb, b_), _mod_spec(d, ncb, b_), _mod_spec(d, ncb, b_),
            _resident((1, d)), _resident(wr.shape)],
        out_specs=[tok, tok, pair,
                   pl.BlockSpec((None, None, GATE_ROWS, tm), lambda b, j: (b, j, 0, 0)), pair,
                   pl.BlockSpec((ROUTER_ROWS, LANES), lambda b, j: (0, 0))],
        out_shape=[jax.ShapeDtypeStruct((b_, tn, d), F32),
                   jax.ShapeDtypeStruct((b_, tn, d), BF16),
                   jax.ShapeDtypeStruct((b_, nt, MOE_TOP_K, tm), jnp.int32),
                   jax.ShapeDtypeStruct((b_, nt, GATE_ROWS, tm), F32),
                   jax.ShapeDtypeStruct((b_, nt, MOE_TOP_K, tm), jnp.int32),
                   jax.ShapeDtypeStruct((ROUTER_ROWS, LANES), F32)],
        scratch_shapes=[pltpu.VMEM((ROUTER_ROWS, LANES), F32)],
        compiler_params=_cparams(("arbitrary", "arbitrary")),
        name="outproj_ffnmod",
    )(a, w, *res_args, gt, sh, sc, g, wr)


def _expert_kernel(*refs, blk_off, aliased):
    if aliased:
        be_ref, nu_ref, x_ref, wg_ref, wu_ref, wd_ref, _, y_ref, wgb_ref, wub_ref, wdb_ref = refs
    else:
        be_ref, nu_ref, x_ref, wg_ref, wu_ref, wd_ref, y_ref, wgb_ref, wub_ref, wdb_ref = refs
    step = pl.program_id(0)
    i = step + blk_off
    new_expert = (step == 0) | (be_ref[i] != be_ref[jnp.maximum(i - 1, 0)])

    @pl.when(new_expert)
    def _():
        wgb_ref[...] = wg_ref[...].astype(BF16)
        wub_ref[...] = wu_ref[...].astype(BF16)
        wdb_ref[...] = wd_ref[...].astype(BF16)

    @pl.when(i < nu_ref[0])
    def _():
        rows = x_ref.shape[0] // 2
        halves = [slice(j * rows, (j + 1) * rows) for j in range(2)]
        gates = [_dot(x_ref[r, :], wgb_ref[...]) for r in halves]
        ups = [_dot(x_ref[r, :], wub_ref[...]) for r in halves]
        for j, r in enumerate(halves):
            act = (gates[j] * _sigmoid(gates[j]) * ups[j]).astype(BF16)
            y_ref[r, :] = _dot(act, wdb_ref[...]).astype(y_ref.dtype)

    @pl.when(i >= nu_ref[0])
    def _():
        y_ref[...] = jnp.zeros_like(y_ref)


def _expert_call(buf, block_expert, n_used, wg, wu, wd, layer, blk_off, r_pad, ybuf=None):
    rows, d = buf.shape
    n_part = rows // MOE_BLOCK
    ff = wg.shape[-1]
    aliased = ybuf is not None

    def x_idx(s, be, nu):
        return (jnp.clip(s, 0, jnp.maximum(nu[0] - blk_off - 1, 0)), 0)

    in_specs = [
        pl.BlockSpec((MOE_BLOCK, d), x_idx),
        pl.BlockSpec((None, None, d, ff), lambda s, be, nu: (layer, be[s + blk_off], 0, 0)),
        pl.BlockSpec((None, None, d, ff), lambda s, be, nu: (layer, be[s + blk_off], 0, 0)),
        pl.BlockSpec((None, None, ff, d), lambda s, be, nu: (layer, be[s + blk_off], 0, 0)),
    ]
    args = [block_expert, n_used, buf, wg, wu, wd]
    if aliased:
        in_specs.append(pl.BlockSpec(memory_space=pl.ANY))
        args.append(ybuf)
    grid_spec = pltpu.PrefetchScalarGridSpec(
        num_scalar_prefetch=2,
        grid=(n_part,),
        in_specs=in_specs,
        out_specs=pl.BlockSpec((MOE_BLOCK, d), lambda s, be, nu: (s + blk_off, 0)),
        scratch_shapes=[pltpu.VMEM((d, ff), BF16), pltpu.VMEM((d, ff), BF16),
                        pltpu.VMEM((ff, d), BF16)],
    )
    return pl.pallas_call(
        functools.partial(_expert_kernel, blk_off=blk_off, aliased=aliased),
        grid_spec=grid_spec,
        out_shape=jax.ShapeDtypeStruct((r_pad, d), BF16),
        input_output_aliases={len(args) - 1: 0} if aliased else {},
        compiler_params=_cparams(("arbitrary",)),
        name="moe_experts",
    )(*args)


def _combine_kernel(*refs, with_next, n_prev):
    if with_next:
        y0_ref, y1_ref, gw_ref, x_ref, gt_ref, sh_ref, sc_ref, g_ref = refs[:8]
        xo_ref, hn_ref = refs[8 + n_prev:]
    else:
        y0_ref, y1_ref, gw_ref, x_ref, gt_ref = refs[:5]
        (xo_ref,) = refs[5 + n_prev:]
    gw = gw_ref[...].T
    f = gw[:, 0:1] * y0_ref[...].astype(F32) + gw[:, 1:2] * y1_ref[...].astype(F32)
    xo = x_ref[...] + gt_ref[...] * f
    xo_ref[...] = xo
    if with_next:
        h = _rms(xo) * g_ref[...]
        hn_ref[...] = (h * (1.0 + sc_ref[...]) + sh_ref[...]).astype(BF16)


def _combine_call(y0, y1, gw, x, gt, n_ctx, tm, next_mod, b_off, prev):
    b_, tn, d = x.shape
    bp = y0.shape[0]
    ncb = n_ctx // tm
    part_tok = pl.BlockSpec((None, tm, d), lambda b, j: (b, j, 0))
    tok = pl.BlockSpec((None, tm, d), lambda b, j: (b + b_off, j, 0))
    in_specs = [part_tok, part_tok,
                pl.BlockSpec((None, None, GATE_ROWS, tm), lambda b, j: (b + b_off, j, 0, 0)),
                tok, _mod_spec(d, ncb, b_, b_off)]
    args = [y0, y1, gw, x, gt]
    out_specs = [tok]
    out_shape = [jax.ShapeDtypeStruct((b_, tn, d), F32)]
    with_next = next_mod is not None
    if with_next:
        sh, sc, g = next_mod
        in_specs += [_mod_spec(d, ncb, b_, b_off), _mod_spec(d, ncb, b_, b_off), _resident((1, d))]
        args += [sh, sc, g]
        out_specs.append(tok)
        out_shape.append(jax.ShapeDtypeStruct((b_, tn, d), BF16))
    aliases = {}
    n_prev = 0
    if prev is not None:
        n_prev = len(prev)
        for k, arr in enumerate(prev):
            aliases[len(args)] = k
            in_specs.append(pl.BlockSpec(memory_space=pl.ANY))
            args.append(arr)
    return pl.pallas_call(
        functools.partial(_combine_kernel, with_next=with_next, n_prev=n_prev),
        grid=(bp, tn // tm),
        in_specs=in_specs,
        out_specs=out_specs,
        out_shape=out_shape,
        input_output_aliases=aliases,
        compiler_params=_cparams(("parallel", "arbitrary")),
        name="moe_combine",
    )(*args)


def _mla_in_kernel(h_ref, win_ref, qg_ref, wq_ref, wqr_ref, tqc_ref, tqs_ref,
                   ktab_ref, wk_ref, wv_ref, kgain_ref, q_ref, k_ref, v_ref):
    tm = h_ref.shape[0]
    proj = _dot(h_ref[...], win_ref[...])

    cq = proj[:, :MLA_Q_LORA]
    qn = (_rms(cq) * qg_ref[...]).astype(BF16)
    p2 = proj[:, MLA_Q_LORA:]
    lane = lax.broadcasted_iota(jnp.int32, (tm, p2.shape[1]), 1)
    is_kv = lane < MLA_KV_LORA
    sq = p2 * p2
    ms_kv = jnp.sum(jnp.where(is_kv, sq, 0.0), axis=-1, keepdims=True) * (1.0 / MLA_KV_LORA)
    is_rope = (lane >= MLA_KV_LORA) & (lane < MLA_KV_LORA + MLA_ROPE)
    ss_rope = jnp.sum(jnp.where(is_rope, sq, 0.0), axis=-1, keepdims=True)
    mult = ktab_ref[...] * jnp.where(is_kv, lax.rsqrt(ms_kv + NORM_EPS), 1.0)
    lhs = (p2 * mult).astype(BF16)

    qa = _dot(qn, wq_ref[...])
    qr = _dot(qn, wqr_ref[...])
    ka = _dot(lhs, wk_ref[...])
    va = _dot(lhs[:, :MLA_KV_LORA], wv_ref[...])
    tqc = tqc_ref[...]
    tqs = tqs_ref[...]
    kgain = kgain_ref[...]
    lane1 = lax.broadcasted_iota(jnp.int32, (tm, LANES), 1)
    for h in range(MLA_HEADS):
        hs = slice(h * LANES, (h + 1) * LANES)
        a = qa[:, hs]
        rq = lax.rsqrt(jnp.sum(a * a, axis=-1, keepdims=True) * (1.0 / MLA_QK) + NORM_EPS)
        q_ref[h] = ((a * tqc + qr[:, hs] * tqs) * rq).astype(BF16)
        a = ka[:, hs]
        ssn = jnp.sum(jnp.where(lane1 < MLA_NOPE, a * a, 0.0), axis=-1, keepdims=True)
        rk = lax.rsqrt((ssn + ss_rope) * (1.0 / MLA_QK) + NORM_EPS)
        k_ref[h] = (a * kgain * rk).astype(BF16)
        ones_lane = MLA_V if h % 2 == 0 else 0
        v_ref[h] = jnp.where(lane1 == ones_lane, 1.0, va[:, hs]).astype(BF16)


def _mla_in_call(hm, win, qg, wq, wqr, tqc, tqs, ktab, wk, wv, kgain, n_ctx, tm):
    b_, t_, d = hm.shape
    ncb = n_ctx // tm
    tok = pl.BlockSpec((None, tm, d), lambda b, j: (b, j, 0))
    head = pl.BlockSpec((None, MLA_HEADS, tm, LANES), lambda b, j: (b, 0, j, 0))
    hshape = jax.ShapeDtypeStruct((b_, MLA_HEADS, t_, LANES), BF16)
    qhead = pl.BlockSpec((None, MLA_HEADS, tm, LANES), lambda b, j: (b, 0, jnp.maximum(j - ncb, 0), 0))
    qshape = jax.ShapeDtypeStruct((b_, MLA_HEADS, t_ - n_ctx, LANES), BF16)
    return pl.pallas_call(
        _mla_in_kernel,
        grid=(b_, t_ // tm),
        in_specs=[tok, _resident(win.shape), _resident(qg.shape), _resident(wq.shape),
                  _resident(wqr.shape),
                  pl.BlockSpec((tm, LANES), lambda b, j: (j, 0)),
                  pl.BlockSpec((tm, LANES), lambda b, j: (j, 0)),
                  pl.BlockSpec((tm, ktab.shape[1]), lambda b, j: (j, 0)),
                  _resident(wk.shape), _resident(wv.shape), _resident(kgain.shape)],
        out_specs=[qhead, head, head],
        out_shape=[qshape, hshape, hshape],
        compiler_params=_cparams(("parallel", "arbitrary")),
        name="mla_in",
    )(hm, win, qg, wq, wqr, tqc, tqs, ktab, wk, wv, kgain)


def _attn_kernel(q_ref, k_ref, v_ref, o_ref):
    n_heads, tq, _ = q_ref.shape
    t_ = k_ref.shape[1]
    ck = next(c for c in ATT_KEY_CHUNKS if t_ % c == 0)
    outs = []
    for hh in range(n_heads):
        q = q_ref[hh]
        m = jnp.full((tq, 1), -jnp.inf, F32)
        acc = jnp.zeros((tq, LANES), F32)
        for c in range(t_ // ck):
            ks = slice(c * ck, (c + 1) * ck)
            s = _dot_nt(q, k_ref[hh, ks, :])
            m_new = jnp.maximum(m, jnp.max(s, axis=-1, keepdims=True))
            p = jnp.exp2((s - m_new).astype(BF16))
            acc = acc * jnp.exp2(m - m_new) + _dot(p, v_ref[hh, ks, :])
            m = m_new
        ones_lane = MLA_V if hh % 2 == 0 else 0
        outs.append(acc / acc[:, ones_lane:ones_lane + 1])
    lane = lax.broadcasted_iota(jnp.int32, outs[0].shape, 1)
    for pr in range(len(outs) // 2):
        o_ref[:, pr * LANES:(pr + 1) * LANES] = jnp.where(
            lane < MLA_V, outs[2 * pr], outs[2 * pr + 1]).astype(o_ref.dtype)


def _attn_call(q, k, v, tq):
    b_, nh, l_, _ = q.shape
    t_ = k.shape[2]
    hps = ATT_HEADS_PER_STEP
    return pl.pallas_call(
        _attn_kernel,
        grid=(b_, nh // hps, l_ // tq),
        in_specs=[
            pl.BlockSpec((None, hps, tq, LANES), lambda b, h, i: (b, h, i, 0)),
            pl.BlockSpec((None, hps, t_, LANES), lambda b, h, i: (b, h, 0, 0)),
            pl.BlockSpec((None, hps, t_, LANES), lambda b, h, i: (b, h, 0, 0)),
        ],
        out_specs=pl.BlockSpec((None, tq, (hps // 2) * LANES), lambda b, h, i: (b, i, h)),
        out_shape=jax.ShapeDtypeStruct((b_, l_, (nh // 2) * LANES), BF16),
        compiler_params=_cparams(("parallel", "parallel", "arbitrary")),
        name="mla_attention",
    )(q, k, v)


def _dispatch_layout(eid, rank, counts):
    tm = eid.shape[-1]
    n_slot = eid.size
    starts = jnp.cumsum(counts) - counts
    padded = (counts + MOE_BLOCK - 1) // MOE_BLOCK * MOE_BLOCK
    pad_ends = jnp.cumsum(padded)
    pad_starts = pad_ends - padded
    n_blocks = (n_slot + MOE_BLOCK - 1) // MOE_BLOCK + MOE_EXPERTS
    block_start = jnp.arange(n_blocks, dtype=jnp.int32) * MOE_BLOCK
    block_expert = jnp.minimum(
        jnp.sum((block_start[:, None] >= pad_ends[None, :]).astype(jnp.int32), axis=1),
        MOE_EXPERTS - 1)
    n_used = pad_ends[-1:] // MOE_BLOCK
    slot_row = rank
    for e in range(MOE_EXPERTS):
        slot_row = slot_row + jnp.where(eid == e, pad_starts[e], 0)
    order = jnp.argsort(eid.reshape(n_slot)).astype(jnp.int32)
    offs = (block_start - pad_starts[block_expert])[:, None] + jnp.arange(MOE_BLOCK, dtype=jnp.int32)
    valid = offs < counts[block_expert][:, None]
    pos = jnp.minimum(starts[block_expert][:, None] + offs, n_slot - 1)
    slot = jnp.take(order, pos, mode='clip')
    tok = (slot // (MOE_TOP_K * tm)) * tm + slot % tm
    n_tok = n_slot // MOE_TOP_K
    filler = (block_start[:, None] + jnp.arange(MOE_BLOCK, dtype=jnp.int32)) % n_tok
    row_tok = jnp.where(valid, tok, filler)
    return row_tok.reshape(-1), slot_row, block_expert, n_used


def _moe(h2, eid, rank, counts, gw, x, gt, wg, wu, wd, layer, n_ctx, tm, next_mod=None):
    b_, tn, d = h2.shape
    n_tok = b_ * tn
    cnt = counts[MOE_GROUPS:MOE_GROUPS + MOE_EXPERTS, 0].astype(jnp.int32)
    row_tok, slot_row, block_expert, n_used = _dispatch_layout(eid, rank, cnt)
    r_pad = row_tok.shape[0]
    n_blocks = r_pad // MOE_BLOCK
    bounds = [n_blocks * p // EXPERT_PARTS for p in range(EXPERT_PARTS + 1)]
    h2_rows = h2.reshape(n_tok, d)
    ybuf = None
    for lo, hi in zip(bounds[:-1], bounds[1:]):
        part = jnp.take(h2_rows, row_tok[lo * MOE_BLOCK:hi * MOE_BLOCK], axis=0, mode='clip')
        ybuf = _expert_call(part, block_expert, n_used, wg, wu, wd, layer, lo, r_pad, ybuf)
    bp = b_ // COMBINE_PARTS
    outs = None
    for p in range(COMBINE_PARTS):
        rows = slot_row[p * bp:(p + 1) * bp]
        y0 = jnp.take(ybuf, rows[:, :, 0, :].reshape(bp * tn), axis=0, mode='clip').reshape(bp, tn, d)
        y1 = jnp.take(ybuf, rows[:, :, 1, :].reshape(bp * tn), axis=0, mode='clip').reshape(bp, tn, d)
        outs = _combine_call(y0, y1, gw, x, gt, n_ctx, tm, next_mod, p * bp, outs)
    return outs


def _rot_half_perm():
    half = MLA_ROPE // 2
    j = jnp.arange(MLA_ROPE)
    within = j % half
    base = j - within
    src = jnp.where(within < half // 2, base + within + half // 2, base + within - half // 2)
    sign = jnp.where(within < half // 2, -1.0, 1.0).astype(F32)
    return src, sign


def _rope_tables(n_ctx, seq):
    rows = seq // GRID_W
    row = jnp.repeat(jnp.arange(rows), GRID_W)
    col = jnp.tile(jnp.arange(GRID_W), rows)
    half = MLA_ROPE // 2
    inv_freq = ROPE_THETA ** (-jnp.arange(0, half, 2, dtype=F32) / half)
    ang = jnp.stack([row, col], axis=-1).astype(F32)[..., None] * inv_freq
    ang = jnp.concatenate([ang, ang], axis=-1).reshape(seq, MLA_ROPE)
    cos = jnp.concatenate([jnp.ones((n_ctx, MLA_ROPE), F32), jnp.cos(ang)], axis=0)
    sin = jnp.concatenate([jnp.zeros((n_ctx, MLA_ROPE), F32), jnp.sin(ang)], axis=0)
    return cos, sin


def _mla_prepare(w_in, w_qb, w_kvb, q_qk_g, k_qk_g, kv_norm_g, n_ctx, seq):
    d = w_in.shape[0]
    src, sign = _rot_half_perm()
    cos, sin = _rope_tables(n_ctx, seq)
    t_ = n_ctx + seq

    rope0 = MLA_Q_LORA + MLA_KV_LORA
    w_rope = w_in[:, rope0:rope0 + MLA_ROPE]
    win = jnp.concatenate(
        [w_in, w_rope[:, src] * sign, jnp.zeros((d, 512 - rope0 - 2 * MLA_ROPE), F32)], axis=1)

    wq3 = w_qb.reshape(MLA_Q_LORA, MLA_HEADS, MLA_QK)
    pad = jnp.zeros((MLA_Q_LORA, MLA_HEADS, LANES - MLA_QK), F32)
    wq = jnp.concatenate([wq3, pad], axis=-1).reshape(MLA_Q_LORA, MLA_HEADS * LANES)
    wq_rot = wq3[:, :, MLA_NOPE:][:, :, src] * sign
    wqr = jnp.concatenate([jnp.zeros((MLA_Q_LORA, MLA_HEADS, MLA_NOPE), F32), wq_rot, pad],
                          axis=-1).reshape(MLA_Q_LORA, MLA_HEADS * LANES)

    scale = MLA_QK ** -0.5 * 1.4426950408889634
    gq_n, gq_r = q_qk_g[:MLA_NOPE], q_qk_g[MLA_NOPE:]
    zpad = jnp.zeros((t_, LANES - MLA_QK), F32)
    tqc = jnp.concatenate([jnp.broadcast_to(gq_n, (t_, MLA_NOPE)), gq_r * cos, zpad], axis=1) * scale
    tqs = jnp.concatenate([jnp.zeros((t_, MLA_NOPE), F32), gq_r[src] * sin, zpad], axis=1) * scale

    wkv3 = w_kvb.reshape(MLA_KV_LORA, MLA_HEADS, MLA_NOPE + MLA_V)
    wk_lat = jnp.concatenate(
        [wkv3[:, :, :MLA_NOPE], jnp.zeros((MLA_KV_LORA, MLA_HEADS, LANES - MLA_NOPE), F32)], axis=-1)
    place = jnp.concatenate([jnp.zeros((MLA_ROPE, MLA_NOPE), F32), jnp.eye(MLA_ROPE, dtype=F32),
                             jnp.zeros((MLA_ROPE, LANES - MLA_QK), F32)], axis=1)
    place = jnp.broadcast_to(place[:, None, :], (MLA_ROPE, MLA_HEADS, LANES))
    wk = jnp.concatenate([wk_lat, place, place,
                          jnp.zeros((256 - MLA_KV_LORA - 2 * MLA_ROPE, MLA_HEADS, LANES), F32)],
                         axis=0).reshape(256, MLA_HEADS * LANES)
    gk_n, gk_r = k_qk_g[:MLA_NOPE], k_qk_g[MLA_NOPE:]
    ktab = jnp.concatenate([jnp.broadcast_to(kv_norm_g, (t_, MLA_KV_LORA)), gk_r * cos,
                            gk_r[src] * sin, jnp.zeros((t_, 256 - MLA_KV_LORA - 2 * MLA_ROPE), F32)],
                           axis=1)
    kgain = jnp.concatenate([gk_n, jnp.ones((MLA_ROPE,), F32),
                             jnp.zeros((LANES - MLA_QK,), F32)]).reshape(1, LANES)

    wv_h = wkv3[:, :, MLA_NOPE:]
    zv = jnp.zeros_like(wv_h)
    odd = (jnp.arange(MLA_HEADS) % 2 == 1)[None, :, None]
    wv = jnp.concatenate([jnp.where(odd, zv, wv_h), jnp.where(odd, wv_h, zv)],
                         axis=-1).reshape(MLA_KV_LORA, MLA_HEADS * LANES)
    return (win.astype(BF16), wq.astype(BF16), wqr.astype(BF16), tqc, tqs, ktab,
            wk.astype(BF16), wv.astype(BF16), kgain)


def _router_weights(w_group, w_expert):
    d = w_group.shape[0]
    wt = jnp.concatenate([w_group, w_expert], axis=1).T
    wt = jnp.concatenate([wt, jnp.zeros((ROUTER_ROWS - wt.shape[0], d), F32)], axis=0)
    head = wt.astype(BF16)
    rest = (wt - head.astype(F32)).astype(BF16)
    return jnp.concatenate([head, rest], axis=0)


def kernel(x, c, ctx, c_ctx, ada_w, ada_b, norm_mix_g, norm_ffn_g, hg_w_in, hg_lower_bounds, hg_out_norm_g, hg_w_out, mla_w_in, mla_q_norm_g, mla_kv_norm_g, mla_w_qb, mla_w_kvb, mla_q_qknorm_g, mla_k_qknorm_g, mla_w_out, moe_w_group, moe_w_expert, moe_w_gate, moe_w_up, moe_w_down):
    b_, seq, d = x.shape
    n_ctx = ctx.shape[1]
    tm = min(ROW_TILE, n_ctx)
    tq = min(ATT_Q_TILE, seq)
    assert n_ctx % GLA_BLOCK == 0 and seq % GLA_BLOCK == 0 and n_ctx % tm == 0 and seq % tm == 0
    assert b_ + 1 <= ADA_ROWS and seq % GRID_W == 0
    assert b_ % min(GLA_BATCH, b_) == 0 and b_ % COMBINE_PARTS == 0

    cc = jnp.concatenate([c, c_ctx[None, :], jnp.zeros((ADA_ROWS - b_ - 1, d), F32)], axis=0)
    mods = _ada_call(cc, ada_w, ada_b).reshape(DEPTH, ADA_CHUNKS, ADA_ROWS, 1, d)

    def mod(i, chunk):
        return mods[i, chunk]

    row = lambda v: v.reshape(1, -1)

    lower = jnp.cumsum(jax.nn.softmax(hg_lower_bounds.astype(F32), axis=1), axis=1)[:, 0]
    planes = _hg_in_call(
        ctx, x, mod(0, 0), mod(0, 1), row(norm_mix_g[0]), hg_w_in[0].astype(BF16), lower, tm)
    o_b = _gla_call(planes, n_ctx, reverse=True)
    og = _gla_call(planes, n_ctx, reverse=False, final_args=(o_b, row(hg_out_norm_g[0])))
    x1, h2, eid, gw, rank, cnt = _outproj_call(
        og, hg_w_out[0].astype(BF16), x, 0, ctx, mod(0, 2), mod(0, 3), mod(0, 4),
        row(norm_ffn_g[0]), _router_weights(moe_w_group[0], moe_w_expert[0]), n_ctx, tm)
    x2, hm = _moe(h2, eid, rank, cnt, gw, x1, mod(0, 5), moe_w_gate, moe_w_up, moe_w_down, 0,
                  n_ctx, tm, next_mod=(mod(1, 0), mod(1, 1), row(norm_mix_g[1])))

    win, wq, wqr, tqc, tqs, ktab, wk, wv, kgain = _mla_prepare(
        mla_w_in[0], mla_w_qb[0], mla_w_kvb[0], mla_q_qknorm_g[0], mla_k_qknorm_g[0],
        mla_kv_norm_g[0], n_ctx, seq)
    q, k, vv = _mla_in_call(hm, win, row(mla_q_norm_g[0]), wq, wqr, tqc, tqs, ktab, wk, wv, kgain,
                            n_ctx, tm)
    o_att = _attn_call(q, k, vv, tq)
    x3, h2, eid, gw, rank, cnt = _outproj_call(
        o_att, mla_w_out[0].astype(BF16), x2, n_ctx, None, mod(1, 2), mod(1, 3), mod(1, 4),
        row(norm_ffn_g[1]), _router_weights(moe_w_group[1], moe_w_expert[1]), 0, tm)
    (x4,) = _moe(h2, eid, rank, cnt, gw, x3, mod(1, 5), moe_w_gate, moe_w_up, moe_w_down, 1, 0, tm)
    return x4
```

```python
import functools

import jax
import jax.numpy as jnp
from jax import lax
from jax.experimental import pallas as pl
from jax.experimental.pallas import tpu as pltpu

F32 = jnp.float32
BF16 = jnp.bfloat16

DEPTH = 2
ADA_CHUNKS = 6
NORM_EPS = 1e-6
GRID_W = 64
HG_HEADS = 8
HG_CHUNK = 32
HG_Q, HG_KF, HG_GF, HG_KB, HG_GB, HG_V, HG_GS = range(7)
HG_PLANES = 7
MLA_HEADS = 16
MLA_Q_LORA = 256
MLA_KV_LORA = 128
MLA_NOPE = 64
MLA_ROPE = 32
MLA_V = 64
MLA_QK = MLA_NOPE + MLA_ROPE
ROPE_THETA = 10000.0
MOE_GROUPS = 4
MOE_EPG = 8
MOE_EXPERTS = MOE_GROUPS * MOE_EPG
MOE_TOP_K = 2
MOE_FF = 512

LANES = 128
ROUTER_ROWS = 48
GATE_ROWS = 8
VMEM_LIMIT = 56 * 1024 * 1024

ROW_TILE = 256
GLA_BLOCK = 128
GLA_BATCH = 8
MOE_BLOCK = 512
ATT_Q_TILE = 512
ATT_HEADS_PER_STEP = 8
ATT_KEY_CHUNKS = (256, 128)
EXPERT_PARTS = 4
COMBINE_PARTS = 2
ADA_ROWS = 40


def _cparams(sem):
    return pltpu.CompilerParams(dimension_semantics=sem, vmem_limit_bytes=VMEM_LIMIT)


def _resident(shape):
    nd = len(shape)
    return pl.BlockSpec(shape, lambda *_: (0,) * nd, pipeline_mode=pl.Buffered(1))


def _dot(a, b):
    return jnp.dot(a, b, preferred_element_type=F32)


def _dot_nt(a, b):
    return lax.dot_general(a, b, (((1,), (1,)), ((), ())), preferred_element_type=F32)


def _dot_tn(a, b):
    return lax.dot_general(a, b, (((0,), (0,)), ((), ())), preferred_element_type=F32)


def _sigmoid(x):
    return 1.0 / (1.0 + jnp.exp(-x))


def _rms(x):
    return x * lax.rsqrt(jnp.mean(x * x, axis=-1, keepdims=True) + NORM_EPS)


def _ada_kernel(c_ref, w_ref, b_ref, o_ref):
    c = c_ref[...]
    a = (c * _sigmoid(c)).astype(BF16)
    o_ref[...] = _dot(a, w_ref[...].astype(BF16)) + b_ref[...]


def _ada_call(cc, ada_w, ada_b):
    depth, d, _ = ada_w.shape
    rows = cc.shape[0]
    return pl.pallas_call(
        _ada_kernel,
        grid=(depth, ADA_CHUNKS),
        in_specs=[
            pl.BlockSpec((rows, d), lambda i, j: (0, 0)),
            pl.BlockSpec((None, d, d), lambda i, j: (i, 0, j)),
            pl.BlockSpec((None, 1, d), lambda i, j: (i, 0, j)),
        ],
        out_specs=pl.BlockSpec((None, None, rows, d), lambda i, j: (i, j, 0, 0)),
        out_shape=jax.ShapeDtypeStruct((depth, ADA_CHUNKS, rows, d), F32),
        compiler_params=_cparams(("arbitrary", "arbitrary")),
        name="ada_mod",
    )(cc, ada_w, ada_b.reshape(depth, 1, ADA_CHUNKS * d))


def _mod_spec(d, n_ctx_blocks, ctx_row, b_off=0):
    def idx(b, j):
        return (jnp.where(j < n_ctx_blocks, ctx_row, b + b_off), 0, 0)
    return pl.BlockSpec((None, 1, d), idx)


def _hg_in_kernel(xc_ref, x_ref, sh_ref, sc_ref, g_ref, w_ref, lb_ref, o_ref, *, n_ctx_blocks):
    d = x_ref.shape[-1]
    x = jnp.where(pl.program_id(1) < n_ctx_blocks, xc_ref[...], x_ref[...])
    h = _rms(x) * g_ref[...]
    h = (h * (1.0 + sc_ref[...]) + sh_ref[...]).astype(BF16)

    def proj(c):
        return _dot(h, w_ref[:, c * d:(c + 1) * d])

    p = proj(0)
    o_ref[HG_Q] = (p * _sigmoid(p)).astype(BF16)
    for c, k_plane, lg_plane in ((1, HG_KF, HG_GF), (2, HG_KB, HG_GB)):
        s = _sigmoid(proj(c))
        lb = lb_ref[c - 1:c, :]
        o_ref[k_plane] = ((1.0 - lb) * (1.0 - s)).astype(BF16)
        o_ref[lg_plane] = jnp.log(lb + (1.0 - lb) * s).astype(BF16)
    o_ref[HG_V] = proj(3).astype(BF16)
    p = proj(4)
    o_ref[HG_GS] = (p * _sigmoid(p)).astype(BF16)


def _hg_in_call(ctx, x, sh, sc, g, w_in, lb, tm):
    b_, n_ctx, d = ctx.shape
    t_ = n_ctx + x.shape[1]
    ncb = n_ctx // tm
    return pl.pallas_call(
        functools.partial(_hg_in_kernel, n_ctx_blocks=ncb),
        grid=(b_, t_ // tm),
        in_specs=[pl.BlockSpec((None, tm, d), lambda b, j: (b, jnp.minimum(j, ncb - 1), 0)),
                  pl.BlockSpec((None, tm, d), lambda b, j: (b, jnp.maximum(j - ncb, 0), 0)),
                  _mod_spec(d, ncb, b_), _mod_spec(d, ncb, b_),
                  _resident((1, d)), _resident(w_in.shape), _resident((2, d))],
        out_specs=pl.BlockSpec((HG_PLANES, None, tm, d), lambda b, j: (0, b, j, 0)),
        out_shape=jax.ShapeDtypeStruct((HG_PLANES, b_, t_, d), BF16),
        compiler_params=_cparams(("parallel", "arbitrary")),
        name="hg_in",
    )(ctx, x, sh, sc, g, w_in, lb)


def _gla_kernel(*refs, reverse, final, n_heads):
    if final:
        q_ref, k_ref, g_ref, v_ref, ob_ref, gs_ref, gn_ref, o_ref, st_ref = refs
    else:
        q_ref, k_ref, g_ref, v_ref, o_ref, st_ref = refs
    nb, tb, d = q_ref.shape
    dh = d // n_heads
    n_chunks = tb // HG_CHUNK

    @pl.when(pl.program_id(1) == 0)
    def _():
        st_ref[...] = jnp.zeros_like(st_ref)

    row = lax.broadcasted_iota(jnp.int32, (tb, tb), 0)
    col = lax.broadcasted_iota(jnp.int32, (tb, tb), 1)
    same_chunk = (row // HG_CHUNK) == (col // HG_CHUNK)
    tri = same_chunk & ((col >= row) if reverse else (col <= row))
    cum = jnp.where(tri, 1.0, 0.0).astype(BF16)
    chunk_order = range(n_chunks - 1, -1, -1) if reverse else range(n_chunks)

    qd, ki_b, e_end, k_end = [], [], [], []
    for n in range(nb):
        b = _dot(cum, g_ref[n])
        qd.append((q_ref[n].astype(F32) * jnp.exp(b)).astype(BF16))
        ki = k_ref[n].astype(F32) * jnp.exp(-b)
        ki_b.append(ki.astype(BF16))
        ee, ke = {}, {}
        for c in range(n_chunks):
            r0 = c * HG_CHUNK
            last = r0 if reverse else r0 + HG_CHUNK - 1
            ee[c] = jnp.exp(b[last:last + 1, :])
            ke[c] = (ki[r0:r0 + HG_CHUNK, :] * ee[c]).astype(BF16)
        e_end.append(ee)
        k_end.append(ke)

    chains = [(n, h) for n in range(nb) for h in range(n_heads)]
    hsl = [slice(h * dh, (h + 1) * dh) for h in range(n_heads)]
    scores = {(n, h): jnp.where(tri, _dot_nt(qd[n][:, hsl[h]], ki_b[n][:, hsl[h]]), 0.0).astype(BF16)
              for n, h in chains}
    o_intra = {(n, h): _dot(scores[n, h], v_ref[n, :, hsl[h]]) for n, h in chains}
    kv = {(n, h, c): _dot_tn(v_ref[n, c * HG_CHUNK:(c + 1) * HG_CHUNK, hsl[h]], k_end[n][c][:, hsl[h]])
          for n, h in chains for c in range(n_chunks)}
    st = {(n, h): st_ref[n, h] for n, h in chains}
    for c in chunk_order:
        rs = slice(c * HG_CHUNK, (c + 1) * HG_CHUNK)
        for n, h in chains:
            hs = hsl[h]
            o_c = o_intra[n, h][rs, :] + _dot_nt(qd[n][rs, hs], st[n, h].astype(BF16))
            st[n, h] = st[n, h] * e_end[n][c][:, hs] + kv[n, h, c]
            if final:
                o_c = o_c + ob_ref[n, rs, hs].astype(F32)
                o_c = _rms(o_c) * gn_ref[...]
                o_c = o_c * gs_ref[n, rs, hs].astype(F32)
            o_ref[n, rs, hs] = o_c.astype(o_ref.dtype)
    for n, h in chains:
        st_ref[n, h] = st[n, h]


def _gla_call(planes, n_ctx, reverse, final_args=None):
    _, b_, t_, d = planes.shape
    tb = GLA_BLOCK
    gb = min(GLA_BATCH, b_)
    nb = t_ // tb
    ncb = n_ctx // tb
    dh = d // HG_HEADS

    def blk(b, j):
        if reverse:
            jj = jnp.where(j < ncb, ncb - 1 - j, nb - 1 - (j - ncb))
        else:
            jj = j
        return (b, jj, 0)

    def plane(p):
        return pl.BlockSpec((None, gb, tb, d), lambda b, j: (p,) + blk(b, j))

    tok = pl.BlockSpec((gb, tb, d), blk)
    k_plane, g_plane = (HG_KB, HG_GB) if reverse else (HG_KF, HG_GF)
    in_specs = [plane(HG_Q), plane(k_plane), plane(g_plane), plane(HG_V)]
    args = [planes] * 4
    final = final_args is not None
    if final:
        o_b, gn = final_args
        in_specs += [tok, plane(HG_GS), _resident((1, dh))]
        args += [o_b, planes, gn]
    return pl.pallas_call(
        functools.partial(_gla_kernel, reverse=reverse, final=final, n_heads=HG_HEADS),
        grid=(b_ // gb, nb),
        in_specs=in_specs,
        out_specs=tok,
        out_shape=jax.ShapeDtypeStruct((b_, t_, d), BF16),
        scratch_shapes=[pltpu.VMEM((gb, HG_HEADS, dh, dh), F32)],
        compiler_params=_cparams(("parallel", "arbitrary")),
        name="gla_bwd" if reverse else "gla_fwd",
    )(*args)


def _outproj_kernel(*refs, n_ctx_blocks, split_residual):
    if split_residual:
        (a_ref, w_ref, xc_ref, x_ref, gt_ref, sh_ref, sc_ref, g_ref, wr_ref,
         xn_ref, h2_ref, eid_ref, gw_ref, rank_ref, cnt_ref, run_ref) = refs
    else:
        (a_ref, w_ref, x_ref, gt_ref, sh_ref, sc_ref, g_ref, wr_ref,
         xn_ref, h2_ref, eid_ref, gw_ref, rank_ref, cnt_ref, run_ref) = refs
    tm = a_ref.shape[0]
    nr = ROUTER_ROWS
    first = (pl.program_id(0) == 0) & (pl.program_id(1) == 0)

    @pl.when(first)
    def _():
        run_ref[...] = jnp.zeros_like(run_ref)

    x = x_ref[...]
    if split_residual:
        x = jnp.where(pl.program_id(1) < n_ctx_blocks, xc_ref[...], x)
    y = _dot(a_ref[...], w_ref[...])
    xn = x + gt_ref[...] * y
    xn_ref[...] = xn
    h2 = _rms(xn) * g_ref[...]
    h2 = h2 * (1.0 + sc_ref[...]) + sh_ref[...]
    h2b = h2.astype(BF16)
    h2_ref[...] = h2b
    lg2 = _dot_nt(wr_ref[...], h2b)
    lg = lg2[:nr] + lg2[nr:]

    ridx = lax.broadcasted_iota(jnp.int32, lg.shape, 0)
    neg = -jnp.inf
    gl = jnp.where(ridx < MOE_GROUPS, lg, neg)
    gmax = jnp.max(gl, axis=0, keepdims=True)
    g_idx = jnp.min(jnp.where(gl == gmax, ridx, nr), axis=0, keepdims=True)
    p_group = 1.0 / jnp.sum(jnp.exp(gl - gmax), axis=0, keepdims=True)
    lo = MOE_GROUPS + MOE_EPG * g_idx
    el = jnp.where((ridx >= lo) & (ridx < lo + MOE_EPG), lg, neg)
    m1 = jnp.max(el, axis=0, keepdims=True)
    i1 = jnp.min(jnp.where(el == m1, ridx, nr), axis=0, keepdims=True)
    el2 = jnp.where(ridx == i1, neg, el)
    m2 = jnp.max(el2, axis=0, keepdims=True)
    i2 = jnp.min(jnp.where(el2 == m2, ridx, nr), axis=0, keepdims=True)
    r21 = jnp.exp(m2 - m1)
    w1 = p_group / (1.0 + r21)
    w2 = w1 * r21

    hot1 = ridx == i1
    hot2 = ridx == i2
    f1 = jnp.where(hot1, 1.0, 0.0)
    f2 = jnp.where(hot2, 1.0, 0.0)
    r = lax.broadcasted_iota(jnp.int32, (tm, tm), 0)
    c = lax.broadcasted_iota(jnp.int32, (tm, tm), 1)
    earlier = jnp.where(r < c, 1.0, 0.0).astype(BF16)
    pre = _dot(jnp.concatenate([f1, f2], axis=0).astype(BF16), earlier)
    run = run_ref[:, 0:1]
    n1 = jnp.sum(f1, axis=1, keepdims=True)
    n2 = jnp.sum(f2, axis=1, keepdims=True)
    rank1 = jnp.sum(jnp.where(hot1, pre[:nr] + run, 0.0), axis=0, keepdims=True)
    rank2 = jnp.sum(jnp.where(hot2, pre[nr:] + (run + n1), 0.0), axis=0, keepdims=True)
    run = jnp.broadcast_to(run + n1 + n2, run_ref.shape)
    run_ref[...] = run
    cnt_ref[...] = run

    eid_ref[...] = jnp.concatenate([i1, i2], axis=0) - MOE_GROUPS
    rank_ref[...] = jnp.concatenate([rank1, rank2], axis=0).astype(jnp.int32)
    gw_ref[...] = jnp.concatenate([w1, w2, jnp.zeros((GATE_ROWS - MOE_TOP_K, tm), F32)], axis=0)


def _outproj_call(a, w, res_lat, lat_off, res_ctx, gt, sh, sc, g, wr, n_ctx, tm):
    b_, tn, d = a.shape
    ncb = n_ctx // tm
    nt = tn // tm
    lob = lat_off // tm
    split = res_ctx is not None
    tok = pl.BlockSpec((None, tm, d), lambda b, j: (b, j, 0))
    pair = pl.BlockSpec((None, None, MOE_TOP_K, tm), lambda b, j: (b, j, 0, 0))
    res_specs, res_args = [], []
    if split:
        res_specs.append(pl.BlockSpec((None, tm, d), lambda b, j: (b, jnp.minimum(j, ncb - 1), 0)))
        res_args.append(res_ctx)
        res_specs.append(pl.BlockSpec((None, tm, d), lambda b, j: (b, jnp.maximum(j - ncb, 0) + lob, 0)))
    else:
        res_specs.append(pl.BlockSpec((None, tm, d), lambda b, j: (b, j + lob, 0)))
    res_args.append(res_lat)
    return pl.pallas_call(
        functools.partial(_outproj_kernel, n_ctx_blocks=ncb, split_residual=split),
        grid=(b_, nt),
        in_specs=[tok, _resident(w.shape)] + res_specs + [
            _mod_spec(d, ncb, b_), _mod_spec(d, ncb, b_), _mod_spec(d, ncb, b_),
            _resident((1, d)), _resident(wr.shape)],
        out_specs=[tok, tok, pair,
                   pl.BlockSpec((None, None, GATE_ROWS, tm), lambda b, j: (b, j, 0, 0)), pair,
                   pl.BlockSpec((ROUTER_ROWS, LANES), lambda b, j: (0, 0))],
        out_shape=[jax.ShapeDtypeStruct((b_, tn, d), F32),
                   jax.ShapeDtypeStruct((b_, tn, d), BF16),
                   jax.ShapeDtypeStruct((b_, nt, MOE_TOP_K, tm), jnp.int32),
                   jax.ShapeDtypeStruct((b_, nt, GATE_ROWS, tm), F32),
                   jax.ShapeDtypeStruct((b_, nt, MOE_TOP_K, tm), jnp.int32),
                   jax.ShapeDtypeStruct((ROUTER_ROWS, LANES), F32)],
        scratch_shapes=[pltpu.VMEM((ROUTER_ROWS, LANES), F32)],
        compiler_params=_cparams(("arbitrary", "arbitrary")),
        name="outproj_ffnmod",
    )(a, w, *res_args, gt, sh, sc, g, wr)


def _expert_kernel(*refs, blk_off, aliased):
    if aliased:
        be_ref, nu_ref, x_ref, wg_ref, wu_ref, wd_ref, _, y_ref, wgb_ref, wub_ref, wdb_ref = refs
    else:
        be_ref, nu_ref, x_ref, wg_ref, wu_ref, wd_ref, y_ref, wgb_ref, wub_ref, wdb_ref = refs
    step = pl.program_id(0)
    i = step + blk_off
    new_expert = (step == 0) | (be_ref[i] != be_ref[jnp.maximum(i - 1, 0)])

    @pl.when(new_expert)
    def _():
        wgb_ref[...] = wg_ref[...].astype(BF16)
        wub_ref[...] = wu_ref[...].astype(BF16)
        wdb_ref[...] = wd_ref[...].astype(BF16)

    @pl.when(i < nu_ref[0])
    def _():
        rows = x_ref.shape[0] // 2
        halves = [slice(j * rows, (j + 1) * rows) for j in range(2)]
        gates = [_dot(x_ref[r, :], wgb_ref[...]) for r in halves]
        ups = [_dot(x_ref[r, :], wub_ref[...]) for r in halves]
        for j, r in enumerate(halves):
            act = (gates[j] * _sigmoid(gates[j]) * ups[j]).astype(BF16)
            y_ref[r, :] = _dot(act, wdb_ref[...]).astype(y_ref.dtype)

    @pl.when(i >= nu_ref[0])
    def _():
        y_ref[...] = jnp.zeros_like(y_ref)


def _expert_call(buf, block_expert, n_used, wg, wu, wd, layer, blk_off, r_pad, ybuf=None):
    rows, d = buf.shape
    n_part = rows // MOE_BLOCK
    ff = wg.shape[-1]
    aliased = ybuf is not None

    def x_idx(s, be, nu):
        return (jnp.clip(s, 0, jnp.maximum(nu[0] - blk_off - 1, 0)), 0)

    in_specs = [
        pl.BlockSpec((MOE_BLOCK, d), x_idx),
        pl.BlockSpec((None, None, d, ff), lambda s, be, nu: (layer, be[s + blk_off], 0, 0)),
        pl.BlockSpec((None, None, d, ff), lambda s, be, nu: (layer, be[s + blk_off], 0, 0)),
        pl.BlockSpec((None, None, ff, d), lambda s, be, nu: (layer, be[s + blk_off], 0, 0)),
    ]
    args = [block_expert, n_used, buf, wg, wu, wd]
    if aliased:
        in_specs.append(pl.BlockSpec(memory_space=pl.ANY))
        args.append(ybuf)
    grid_spec = pltpu.PrefetchScalarGridSpec(
        num_scalar_prefetch=2,
        grid=(n_part,),
        in_specs=in_specs,
        out_specs=pl.BlockSpec((MOE_BLOCK, d), lambda s, be, nu: (s + blk_off, 0)),
        scratch_shapes=[pltpu.VMEM((d, ff), BF16), pltpu.VMEM((d, ff), BF16),
                        pltpu.VMEM((ff, d), BF16)],
    )
    return pl.pallas_call(
        functools.partial(_expert_kernel, blk_off=blk_off, aliased=aliased),
        grid_spec=grid_spec,
        out_shape=jax.ShapeDtypeStruct((r_pad, d), BF16),
        input_output_aliases={len(args) - 1: 0} if aliased else {},
        compiler_params=_cparams(("arbitrary",)),
        name="moe_experts",
    )(*args)


def _combine_kernel(*refs, with_next, n_prev):
    if with_next:
        y0_ref, y1_ref, gw_ref, x_ref, gt_ref, sh_ref, sc_ref, g_ref = refs[:8]
        xo_ref, hn_ref = refs[8 + n_prev:]
    else:
        y0_ref, y1_ref, gw_ref, x_ref, gt_ref = refs[:5]
        (xo_ref,) = refs[5 + n_prev:]
    gw = gw_ref[...].T
    f = gw[:, 0:1] * y0_ref[...].astype(F32) + gw[:, 1:2] * y1_ref[...].astype(F32)
    xo = x_ref[...] + gt_ref[...] * f
    xo_ref[...] = xo
    if with_next:
        h = _rms(xo) * g_ref[...]
        hn_ref[...] = (h * (1.0 + sc_ref[...]) + sh_ref[...]).astype(BF16)


def _combine_call(y01, gw, x, gt, n_ctx, tm, next_mod, b_off, prev):
    b_, tn, d = x.shape
    bp = y01.shape[1]
    y0 = y1 = y01
    ncb = n_ctx // tm
    tok = pl.BlockSpec((None, tm, d), lambda b, j: (b + b_off, j, 0))
    in_specs = [pl.BlockSpec((None, None, tm, d), lambda b, j: (0, b, j, 0)),
                pl.BlockSpec((None, None, tm, d), lambda b, j: (1, b, j, 0)),
                pl.BlockSpec((None, None, GATE_ROWS, tm), lambda b, j: (b + b_off, j, 0, 0)),
                tok, _mod_spec(d, ncb, b_, b_off)]
    args = [y0, y1, gw, x, gt]
    out_specs = [tok]
    out_shape = [jax.ShapeDtypeStruct((b_, tn, d), F32)]
    with_next = next_mod is not None
    if with_next:
        sh, sc, g = next_mod
        in_specs += [_mod_spec(d, ncb, b_, b_off), _mod_spec(d, ncb, b_, b_off), _resident((1, d))]
        args += [sh, sc, g]
        out_specs.append(tok)
        out_shape.append(jax.ShapeDtypeStruct((b_, tn, d), BF16))
    aliases = {}
    n_prev = 0
    if prev is not None:
        n_prev = len(prev)
        for k, arr in enumerate(prev):
            aliases[len(args)] = k
            in_specs.append(pl.BlockSpec(memory_space=pl.ANY))
            args.append(arr)
    return pl.pallas_call(
        functools.partial(_combine_kernel, with_next=with_next, n_prev=n_prev),
        grid=(bp, tn // tm),
        in_specs=in_specs,
        out_specs=out_specs,
        out_shape=out_shape,
        input_output_aliases=aliases,
        compiler_params=_cparams(("parallel", "arbitrary")),
        name="moe_combine",
    )(*args)


def _mla_in_kernel(h_ref, win_ref, qg_ref, wq_ref, wqr_ref, tqc_ref, tqs_ref,
                   ktab_ref, wk_ref, wv_ref, kgain_ref, q_ref, k_ref, v_ref):
    tm = h_ref.shape[0]
    proj = _dot(h_ref[...], win_ref[...])

    cq = proj[:, :MLA_Q_LORA]
    qn = (_rms(cq) * qg_ref[...]).astype(BF16)
    p2 = proj[:, MLA_Q_LORA:]
    lane = lax.broadcasted_iota(jnp.int32, (tm, p2.shape[1]), 1)
    is_kv = lane < MLA_KV_LORA
    sq = p2 * p2
    ms_kv = jnp.sum(jnp.where(is_kv, sq, 0.0), axis=-1, keepdims=True) * (1.0 / MLA_KV_LORA)
    is_rope = (lane >= MLA_KV_LORA) & (lane < MLA_KV_LORA + MLA_ROPE)
    ss_rope = jnp.sum(jnp.where(is_rope, sq, 0.0), axis=-1, keepdims=True)
    mult = ktab_ref[...] * jnp.where(is_kv, lax.rsqrt(ms_kv + NORM_EPS), 1.0)
    lhs = (p2 * mult).astype(BF16)

    qa = _dot(qn, wq_ref[...])
    qr = _dot(qn, wqr_ref[...])
    ka = _dot(lhs, wk_ref[...])
    va = _dot(lhs[:, :MLA_KV_LORA], wv_ref[...])
    tqc = tqc_ref[...]
    tqs = tqs_ref[...]
    kgain = kgain_ref[...]
    lane1 = lax.broadcasted_iota(jnp.int32, (tm, LANES), 1)
    for h in range(MLA_HEADS):
        hs = slice(h * LANES, (h + 1) * LANES)
        a = qa[:, hs]
        rq = lax.rsqrt(jnp.sum(a * a, axis=-1, keepdims=True) * (1.0 / MLA_QK) + NORM_EPS)
        q_ref[h] = ((a * tqc + qr[:, hs] * tqs) * rq).astype(BF16)
        a = ka[:, hs]
        ssn = jnp.sum(jnp.where(lane1 < MLA_NOPE, a * a, 0.0), axis=-1, keepdims=True)
        rk = lax.rsqrt((ssn + ss_rope) * (1.0 / MLA_QK) + NORM_EPS)
        k_ref[h] = (a * kgain * rk).astype(BF16)
        ones_lane = MLA_V if h % 2 == 0 else 0
        v_ref[h] = jnp.where(lane1 == ones_lane, 1.0, va[:, hs]).astype(BF16)


def _mla_in_call(hm, win, qg, wq, wqr, tqc, tqs, ktab, wk, wv, kgain, n_ctx, tm):
    b_, t_, d = hm.shape
    ncb = n_ctx // tm
    tok = pl.BlockSpec((None, tm, d), lambda b, j: (b, j, 0))
    head = pl.BlockSpec((None, MLA_HEADS, tm, LANES), lambda b, j: (b, 0, j, 0))
    hshape = jax.ShapeDtypeStruct((b_, MLA_HEADS, t_, LANES), BF16)
    qhead = pl.BlockSpec((None, MLA_HEADS, tm, LANES), lambda b, j: (b, 0, jnp.maximum(j - ncb, 0), 0))
    qshape = jax.ShapeDtypeStruct((b_, MLA_HEADS, t_ - n_ctx, LANES), BF16)
    return pl.pallas_call(
        _mla_in_kernel,
        grid=(b_, t_ // tm),
        in_specs=[tok, _resident(win.shape), _resident(qg.shape), _resident(wq.shape),
                  _resident(wqr.shape),
                  pl.BlockSpec((tm, LANES), lambda b, j: (j, 0)),
                  pl.BlockSpec((tm, LANES), lambda b, j: (j, 0)),
                  pl.BlockSpec((tm, ktab.shape[1]), lambda b, j: (j, 0)),
                  _resident(wk.shape), _resident(wv.shape), _resident(kgain.shape)],
        out_specs=[qhead, head, head],
        out_shape=[qshape, hshape, hshape],
        compiler_params=_cparams(("parallel", "arbitrary")),
        name="mla_in",
    )(hm, win, qg, wq, wqr, tqc, tqs, ktab, wk, wv, kgain)


def _attn_kernel(q_ref, k_ref, v_ref, o_ref):
    n_heads, tq, _ = q_ref.shape
    t_ = k_ref.shape[1]
    ck = next(c for c in ATT_KEY_CHUNKS if t_ % c == 0)
    outs = []
    for hh in range(n_heads):
        q = q_ref[hh]
        m = jnp.full((tq, 1), -jnp.inf, F32)
        acc = jnp.zeros((tq, LANES), F32)
        for c in range(t_ // ck):
            ks = slice(c * ck, (c + 1) * ck)
            s = _dot_nt(q, k_ref[hh, ks, :])
            m_new = jnp.maximum(m, jnp.max(s, axis=-1, keepdims=True))
            p = jnp.exp2((s - m_new).astype(BF16))
            acc = acc * jnp.exp2(m - m_new) + _dot(p, v_ref[hh, ks, :])
            m = m_new
        ones_lane = MLA_V if hh % 2 == 0 else 0
        outs.append(acc / acc[:, ones_lane:ones_lane + 1])
    lane = lax.broadcasted_iota(jnp.int32, outs[0].shape, 1)
    for pr in range(len(outs) // 2):
        o_ref[:, pr * LANES:(pr + 1) * LANES] = jnp.where(
            lane < MLA_V, outs[2 * pr], outs[2 * pr + 1]).astype(o_ref.dtype)


def _attn_call(q, k, v, tq):
    b_, nh, l_, _ = q.shape
    t_ = k.shape[2]
    hps = ATT_HEADS_PER_STEP
    return pl.pallas_call(
        _attn_kernel,
        grid=(b_, nh // hps, l_ // tq),
        in_specs=[
            pl.BlockSpec((None, hps, tq, LANES), lambda b, h, i: (b, h, i, 0)),
            pl.BlockSpec((None, hps, t_, LANES), lambda b, h, i: (b, h, 0, 0)),
            pl.BlockSpec((None, hps, t_, LANES), lambda b, h, i: (b, h, 0, 0)),
        ],
        out_specs=pl.BlockSpec((None, tq, (hps // 2) * LANES), lambda b, h, i: (b, i, h)),
        out_shape=jax.ShapeDtypeStruct((b_, l_, (nh // 2) * LANES), BF16),
        compiler_params=_cparams(("parallel", "parallel", "arbitrary")),
        name="mla_attention",
    )(q, k, v)


def _dispatch_layout(eid, rank, counts):
    tm = eid.shape[-1]
    n_slot = eid.size
    starts = jnp.cumsum(counts) - counts
    padded = (counts + MOE_BLOCK - 1) // MOE_BLOCK * MOE_BLOCK
    pad_ends = jnp.cumsum(padded)
    pad_starts = pad_ends - padded
    n_blocks = (n_slot + MOE_BLOCK - 1) // MOE_BLOCK + MOE_EXPERTS
    block_start = jnp.arange(n_blocks, dtype=jnp.int32) * MOE_BLOCK
    block_expert = jnp.minimum(
        jnp.sum((block_start[:, None] >= pad_ends[None, :]).astype(jnp.int32), axis=1),
        MOE_EXPERTS - 1)
    n_used = pad_ends[-1:] // MOE_BLOCK
    slot_row = rank
    for e in range(MOE_EXPERTS):
        slot_row = slot_row + jnp.where(eid == e, pad_starts[e], 0)
    order = jnp.argsort(eid.reshape(n_slot)).astype(jnp.int32)
    offs = (block_start - pad_starts[block_expert])[:, None] + jnp.arange(MOE_BLOCK, dtype=jnp.int32)
    valid = offs < counts[block_expert][:, None]
    pos = jnp.minimum(starts[block_expert][:, None] + offs, n_slot - 1)
    slot = jnp.take(order, pos, mode='clip')
    tok = (slot // (MOE_TOP_K * tm)) * tm + slot % tm
    n_tok = n_slot // MOE_TOP_K
    filler = (block_start[:, None] + jnp.arange(MOE_BLOCK, dtype=jnp.int32)) % n_tok
    row_tok = jnp.where(valid, tok, filler)
    return row_tok.reshape(-1), slot_row, block_expert, n_used


def _moe(h2, eid, rank, counts, gw, x, gt, wg, wu, wd, layer, n_ctx, tm, next_mod=None):
    b_, tn, d = h2.shape
    n_tok = b_ * tn
    cnt = counts[MOE_GROUPS:MOE_GROUPS + MOE_EXPERTS, 0].astype(jnp.int32)
    row_tok, slot_row, block_expert, n_used = _dispatch_layout(eid, rank, cnt)
    r_pad = row_tok.shape[0]
    n_blocks = r_pad // MOE_BLOCK
    bounds = [n_blocks * p // EXPERT_PARTS for p in range(EXPERT_PARTS + 1)]
    h2_rows = h2.reshape(n_tok, d)
    ybuf = None
    for lo, hi in zip(bounds[:-1], bounds[1:]):
        part = jnp.take(h2_rows, row_tok[lo * MOE_BLOCK:hi * MOE_BLOCK], axis=0, mode='clip')
        ybuf = _expert_call(part, block_expert, n_used, wg, wu, wd, layer, lo, r_pad, ybuf)
    bp = b_ // COMBINE_PARTS
    outs = None
    for p in range(COMBINE_PARTS):
        rows = slot_row[p * bp:(p + 1) * bp]
        both = jnp.moveaxis(rows, 2, 0).reshape(MOE_TOP_K * bp * tn)
        y01 = jnp.take(ybuf, both, axis=0, mode='clip').reshape(MOE_TOP_K, bp, tn, d)
        outs = _combine_call(y01, gw, x, gt, n_ctx, tm, next_mod, p * bp, outs)
    return outs


def _rot_half_perm():
    half = MLA_ROPE // 2
    j = jnp.arange(MLA_ROPE)
    within = j % half
    base = j - within
    src = jnp.where(within < half // 2, base + within + half // 2, base + within - half // 2)
    sign = jnp.where(within < half // 2, -1.0, 1.0).astype(F32)
    return src, sign


def _rope_tables(n_ctx, seq):
    rows = seq // GRID_W
    row = jnp.repeat(jnp.arange(rows), GRID_W)
    col = jnp.tile(jnp.arange(GRID_W), rows)
    half = MLA_ROPE // 2
    inv_freq = ROPE_THETA ** (-jnp.arange(0, half, 2, dtype=F32) / half)
    ang = jnp.stack([row, col], axis=-1).astype(F32)[..., None] * inv_freq
    ang = jnp.concatenate([ang, ang], axis=-1).reshape(seq, MLA_ROPE)
    cos = jnp.concatenate([jnp.ones((n_ctx, MLA_ROPE), F32), jnp.cos(ang)], axis=0)
    sin = jnp.concatenate([jnp.zeros((n_ctx, MLA_ROPE), F32), jnp.sin(ang)], axis=0)
    return cos, sin


def _mla_prepare(w_in, w_qb, w_kvb, q_qk_g, k_qk_g, kv_norm_g, n_ctx, seq):
    d = w_in.shape[0]
    src, sign = _rot_half_perm()
    cos, sin = _rope_tables(n_ctx, seq)
    t_ = n_ctx + seq

    rope0 = MLA_Q_LORA + MLA_KV_LORA
    w_rope = w_in[:, rope0:rope0 + MLA_ROPE]
    win = jnp.concatenate(
        [w_in, w_rope[:, src] * sign, jnp.zeros((d, 512 - rope0 - 2 * MLA_ROPE), F32)], axis=1)

    wq3 = w_qb.reshape(MLA_Q_LORA, MLA_HEADS, MLA_QK)
    pad = jnp.zeros((MLA_Q_LORA, MLA_HEADS, LANES - MLA_QK), F32)
    wq = jnp.concatenate([wq3, pad], axis=-1).reshape(MLA_Q_LORA, MLA_HEADS * LANES)
    wq_rot = wq3[:, :, MLA_NOPE:][:, :, src] * sign
    wqr = jnp.concatenate([jnp.zeros((MLA_Q_LORA, MLA_HEADS, MLA_NOPE), F32), wq_rot, pad],
                          axis=-1).reshape(MLA_Q_LORA, MLA_HEADS * LANES)

    scale = MLA_QK ** -0.5 * 1.4426950408889634
    gq_n, gq_r = q_qk_g[:MLA_NOPE], q_qk_g[MLA_NOPE:]
    zpad = jnp.zeros((t_, LANES - MLA_QK), F32)
    tqc = jnp.concatenate([jnp.broadcast_to(gq_n, (t_, MLA_NOPE)), gq_r * cos, zpad], axis=1) * scale
    tqs = jnp.concatenate([jnp.zeros((t_, MLA_NOPE), F32), gq_r[src] * sin, zpad], axis=1) * scale

    wkv3 = w_kvb.reshape(MLA_KV_LORA, MLA_HEADS, MLA_NOPE + MLA_V)
    wk_lat = jnp.concatenate(
        [wkv3[:, :, :MLA_NOPE], jnp.zeros((MLA_KV_LORA, MLA_HEADS, LANES - MLA_NOPE), F32)], axis=-1)
    place = jnp.concatenate([jnp.zeros((MLA_ROPE, MLA_NOPE), F32), jnp.eye(MLA_ROPE, dtype=F32),
                             jnp.zeros((MLA_ROPE, LANES - MLA_QK), F32)], axis=1)
    place = jnp.broadcast_to(place[:, None, :], (MLA_ROPE, MLA_HEADS, LANES))
    wk = jnp.concatenate([wk_lat, place, place,
                          jnp.zeros((256 - MLA_KV_LORA - 2 * MLA_ROPE, MLA_HEADS, LANES), F32)],
                         axis=0).reshape(256, MLA_HEADS * LANES)
    gk_n, gk_r = k_qk_g[:MLA_NOPE], k_qk_g[MLA_NOPE:]
    ktab = jnp.concatenate([jnp.broadcast_to(kv_norm_g, (t_, MLA_KV_LORA)), gk_r * cos,
                            gk_r[src] * sin, jnp.zeros((t_, 256 - MLA_KV_LORA - 2 * MLA_ROPE), F32)],
                           axis=1)
    kgain = jnp.concatenate([gk_n, jnp.ones((MLA_ROPE,), F32),
                             jnp.zeros((LANES - MLA_QK,), F32)]).reshape(1, LANES)

    wv_h = wkv3[:, :, MLA_NOPE:]
    zv = jnp.zeros_like(wv_h)
    odd = (jnp.arange(MLA_HEADS) % 2 == 1)[None, :, None]
    wv = jnp.concatenate([jnp.where(odd, zv, wv_h), jnp.where(odd, wv_h, zv)],
                         axis=-1).reshape(MLA_KV_LORA, MLA_HEADS * LANES)
    return (win.astype(BF16), wq.astype(BF16), wqr.astype(BF16), tqc, tqs, ktab,
            wk.astype(BF16), wv.astype(BF16), kgain)


def _router_weights(w_group, w_expert):
    d = w_group.shape[0]
    wt = jnp.concatenate([w_group, w_expert], axis=1).T
    wt = jnp.concatenate([wt, jnp.zeros((ROUTER_ROWS - wt.shape[0], d), F32)], axis=0)
    head = wt.astype(BF16)
    rest = (wt - head.astype(F32)).astype(BF16)
    return jnp.concatenate([head, rest], axis=0)


def kernel(x, c, ctx, c_ctx, ada_w, ada_b, norm_mix_g, norm_ffn_g, hg_w_in, hg_lower_bounds, hg_out_norm_g, hg_w_out, mla_w_in, mla_q_norm_g, mla_kv_norm_g, mla_w_qb, mla_w_kvb, mla_q_qknorm_g, mla_k_qknorm_g, mla_w_out, moe_w_group, moe_w_expert, moe_w_gate, moe_w_up, moe_w_down):
    b_, seq, d = x.shape
    n_ctx = ctx.shape[1]
    tm = min(ROW_TILE, n_ctx)
    tq = min(ATT_Q_TILE, seq)
    assert n_ctx % GLA_BLOCK == 0 and seq % GLA_BLOCK == 0 and n_ctx % tm == 0 and seq % tm == 0
    assert b_ + 1 <= ADA_ROWS and seq % GRID_W == 0
    assert b_ % min(GLA_BATCH, b_) == 0 and b_ % COMBINE_PARTS == 0

    cc = jnp.concatenate([c, c_ctx[None, :], jnp.zeros((ADA_ROWS - b_ - 1, d), F32)], axis=0)
    mods = _ada_call(cc, ada_w, ada_b).reshape(DEPTH, ADA_CHUNKS, ADA_ROWS, 1, d)

    def mod(i, chunk):
        return mods[i, chunk]

    row = lambda v: v.reshape(1, -1)

    lower = jnp.cumsum(jax.nn.softmax(hg_lower_bounds.astype(F32), axis=1), axis=1)[:, 0]
    planes = _hg_in_call(
        ctx, x, mod(0, 0), mod(0, 1), row(norm_mix_g[0]), hg_w_in[0].astype(BF16), lower, tm)
    o_b = _gla_call(planes, n_ctx, reverse=True)
    og = _gla_call(planes, n_ctx, reverse=False, final_args=(o_b, row(hg_out_norm_g[0])))
    x1, h2, eid, gw, rank, cnt = _outproj_call(
        og, hg_w_out[0].astype(BF16), x, 0, ctx, mod(0, 2), mod(0, 3), mod(0, 4),
        row(norm_ffn_g[0]), _router_weights(moe_w_group[0], moe_w_expert[0]), n_ctx, tm)
    x2, hm = _moe(h2, eid, rank, cnt, gw, x1, mod(0, 5), moe_w_gate, moe_w_up, moe_w_down, 0,
                  n_ctx, tm, next_mod=(mod(1, 0), mod(1, 1), row(norm_mix_g[1])))

    win, wq, wqr, tqc, tqs, ktab, wk, wv, kgain = _mla_prepare(
        mla_w_in[0], mla_w_qb[0], mla_w_kvb[0], mla_q_qknorm_g[0], mla_k_qknorm_g[0],
        mla_kv_norm_g[0], n_ctx, seq)
    q, k, vv = _mla_in_call(hm, win, row(mla_q_norm_g[0]), wq, wqr, tqc, tqs, ktab, wk, wv, kgain,
                            n_ctx, tm)
    o_att = _attn_call(q, k, vv, tq)
    x3, h2, eid, gw, rank, cnt = _outproj_call(
        o_att, mla_w_out[0].astype(BF16), x2, n_ctx, None, mod(1, 2), mod(1, 3), mod(1, 4),
        row(norm_ffn_g[1]), _router_weights(moe_w_group[1], moe_w_expert[1]), 0, tm)
    (x4,) = _moe(h2, eid, rank, cnt, gw, x3, mod(1, 5), moe_w_gate, moe_w_up, moe_w_down, 1, 0, tm)
    return x4
```
